```python
import jax, jax.numpy as jnp
from jax import lax
import numpy as np

D_MODEL = 1024
BATCH = 16
SEQ = 4096
DEPTH = 4
DEC_BATCH = 8
DEC_SEQ = 32
PAST_LEN = 4096

CHUNK = 64
Q_BLOCK = 128
D_FF = 2816
N_SUB = 3
MLA_HEADS = 8
MLA_NOPE = 64
MLA_ROPE = 32
MLA_V = 64
Q_RANK = 256
KV_RANK = 128
MLA_IN = Q_RANK + KV_RANK + MLA_ROPE
RET_HEADS = 8
RET_DK = 64
RET_DV = 64
RET_IN = 2 * RET_HEADS * RET_DK + 2 * RET_HEADS * RET_DV
D_IN = MLA_IN + RET_IN
D_MIX = MLA_HEADS * MLA_V + RET_HEADS * RET_DV
ROPE_BASE = 10000.0
EPS = 1e-6
MLA_SCALE = (MLA_NOPE + MLA_ROPE) ** -0.5
RET_SCALE = RET_DK ** -0.5
NEG_INF = -1e30

kernel_name = "hybrid_mla_retention_streaming_step"


def rms_norm(x, g=None):
    xf = x.astype(jnp.float32)
    y = xf * lax.rsqrt(jnp.mean(jnp.square(xf), axis=-1, keepdims=True) + EPS)
    if g is not None:
        y = y * g.astype(jnp.float32)
    return y.astype(x.dtype)


def rope(x, pos):
    half = x.shape[-1] // 2
    inv = ROPE_BASE ** (-jnp.arange(half, dtype=jnp.float32) / half)
    ang = pos.astype(jnp.float32)[:, None] * inv[None, :]
    cos = jnp.cos(ang)[:, None, :]
    sin = jnp.sin(ang)[:, None, :]
    xf = x.astype(jnp.float32)
    x1, x2 = xf[..., :half], xf[..., half:]
    return jnp.concatenate([x1 * cos - x2 * sin, x2 * cos + x1 * sin], axis=-1).astype(x.dtype)


def swiglu(h, w1, w3, w2):
    return (jax.nn.silu(h @ w1) * (h @ w3)) @ w2


def modulated_pre(x, g, mods, j):
    return rms_norm(x, g) * (1.0 + mods[:, 3 * j + 1, None]) + mods[:, 3 * j, None]


def gated_post(x, o, g, mods, j, coef):
    return x + coef * mods[:, 3 * j + 2, None] * rms_norm(o, g)


def mla_project(z, pos, q_norm_g, kv_norm_g, w_uq, w_uk):
    b, s, _ = z.shape
    q_lat = rms_norm(z[..., :Q_RANK], q_norm_g)
    kv_lat = rms_norm(z[..., Q_RANK:Q_RANK + KV_RANK], kv_norm_g)
    k_pe = rope(z[..., Q_RANK + KV_RANK:MLA_IN][:, :, None, :], pos)[:, :, 0, :]
    q = (q_lat @ w_uq).reshape(b, s, MLA_HEADS, MLA_NOPE + MLA_ROPE)
    q_pe = rope(q[..., MLA_NOPE:], pos)
    q_abs = jnp.einsum('bshn,rhn->bshr', q[..., :MLA_NOPE],
                       w_uk.reshape(KV_RANK, MLA_HEADS, MLA_NOPE))
    return q_abs, q_pe, kv_lat, k_pe


def mla_attend(q_abs, q_pe, kv_lat, k_pe, q_pos, k_pos):
    s = jnp.einsum('bqhr,bkr->bhqk', q_abs, kv_lat) + jnp.einsum('bqhd,bkd->bhqk', q_pe, k_pe)
    s = s.astype(jnp.float32) * MLA_SCALE
    allowed = (k_pos[None, :] // CHUNK) <= (q_pos[:, None] // CHUNK)
    p = jax.nn.softmax(jnp.where(allowed, s, NEG_INF), axis=-1).astype(kv_lat.dtype)
    return jnp.einsum('bhqk,bkr->bqhr', p, kv_lat)


def mla_output(o_lat, w_uv):
    b, s = o_lat.shape[:2]
    o = jnp.einsum('bshr,rhv->bshv', o_lat, w_uv.reshape(KV_RANK, MLA_HEADS, MLA_V))
    return o.reshape(b, s, MLA_HEADS * MLA_V)


def ret_project(z, pos):
    b, s, _ = z.shape
    hk, hv = RET_HEADS * RET_DK, RET_HEADS * RET_DV
    q = rope(z[..., :hk].reshape(b, s, RET_HEADS, RET_DK), pos)
    k = rope(z[..., hk:2 * hk].reshape(b, s, RET_HEADS, RET_DK), pos) * RET_SCALE
    v = z[..., 2 * hk:2 * hk + hv].reshape(b, s, RET_HEADS, RET_DV)
    g = z[..., 2 * hk + hv:]
    return q, k, v, g


def retention_block(state, q, k, v):
    t = q.shape[1]
    log_g = jnp.log1p(-jnp.exp2(-5.0 - jnp.arange(RET_HEADS, dtype=jnp.float32)))
    n = jnp.arange(t, dtype=jnp.float32)
    diff = n[:, None] - n[None, :]
    decay = jnp.where(diff >= 0.0, jnp.exp(jnp.maximum(diff, 0.0)[None] * log_g[:, None, None]), 0.0)
    qf, kf, vf = q.astype(jnp.float32), k.astype(jnp.float32), v.astype(jnp.float32)
    scores = jnp.einsum('bnhd,bmhd->bhnm', qf, kf) * decay[None]
    o = jnp.einsum('bhnm,bmhv->bnhv', scores, vf)
    cross = jnp.exp((n + 1.0)[:, None] * log_g[None, :])
    o = o + jnp.einsum('bnhd,bhdv->bnhv', qf, state) * cross[None, :, :, None]
    k_dec = kf * jnp.exp((t - 1.0 - n)[:, None] * log_g[None, :])[None, :, :, None]
    new_state = jnp.exp(t * log_g)[None, :, None, None] * state + jnp.einsum('bmhd,bmhv->bhdv', k_dec, vf)
    return new_state, o


def ret_output(o, g):
    b, s = o.shape[:2]
    return rms_norm(o).reshape(b, s, RET_HEADS * RET_DV).astype(g.dtype) * jax.nn.silu(g)


def setup_inputs(seed: int = 0) -> dict:
    key = jax.random.key(seed)
    ks = jax.random.split(key, 24)
    f32 = jnp.float32
    nrm = lambda k, shape, scale: jax.random.normal(k, shape, f32) * scale
    return {
        'x_prompt': nrm(ks[0], (BATCH, SEQ, D_MODEL), 1.0),
        'x_sample': nrm(ks[1], (DEC_BATCH, DEC_SEQ, D_MODEL), 1.0),
        'cache_kv_latent': nrm(ks[2], (DEPTH, DEC_BATCH, PAST_LEN, KV_RANK), 1.0),
        'cache_k_rope': nrm(ks[3], (DEPTH, DEC_BATCH, PAST_LEN, MLA_ROPE), 1.0),
        'state_ret': nrm(ks[4], (DEPTH, DEC_BATCH, RET_HEADS, RET_DK, RET_DV), 0.5),
        'c_prompt': nrm(ks[5], (BATCH, D_MODEL), 1.0),
        'c_sample': nrm(ks[6], (DEC_BATCH, D_MODEL), 1.0),
        'ada_w': nrm(ks[7], (DEPTH, D_MODEL, 3 * N_SUB * D_MODEL), 0.3 * D_MODEL ** -0.5),
        'ada_b': nrm(ks[8], (DEPTH, 3 * N_SUB * D_MODEL), 0.02),
        'norm_pre': 1.0 + nrm(ks[9], (DEPTH, N_SUB, D_MODEL), 0.05),
        'norm_post': 1.0 + nrm(ks[10], (DEPTH, N_SUB, D_MODEL), 0.05),
        'ffn_w1': nrm(ks[11], (DEPTH, 2, D_MODEL, D_FF), D_MODEL ** -0.5),
        'ffn_w3': nrm(ks[12], (DEPTH, 2, D_MODEL, D_FF), D_MODEL ** -0.5),
        'ffn_w2': nrm(ks[13], (DEPTH, 2, D_FF, D_MODEL), D_FF ** -0.5),
        'w_in': nrm(ks[14], (DEPTH, D_MODEL, D_IN), D_MODEL ** -0.5),
        'q_norm': 1.0 + nrm(ks[15], (DEPTH, Q_RANK), 0.05),
        'kv_norm': 1.0 + nrm(ks[16], (DEPTH, KV_RANK), 0.05),
        'w_uq': nrm(ks[17], (DEPTH, Q_RANK, MLA_HEADS * (MLA_NOPE + MLA_ROPE)), Q_RANK ** -0.5),
        'w_uk': nrm(ks[18], (DEPTH, KV_RANK, MLA_HEADS * MLA_NOPE), KV_RANK ** -0.5),
        'w_uv': nrm(ks[19], (DEPTH, KV_RANK, MLA_HEADS * MLA_V), KV_RANK ** -0.5),
        'w_out': nrm(ks[20], (DEPTH, D_MIX, D_MODEL), D_MIX ** -0.5),
    }


def reference(x_prompt, x_sample, cache_kv_latent, cache_k_rope, state_ret, c_prompt, c_sample,
              ada_w, ada_b, norm_pre, norm_post, ffn_w1, ffn_w3, ffn_w2,
              w_in, q_norm, kv_norm, w_uq, w_uk, w_uv, w_out):

    def mixer_prompt(h, l):
        b, s, _ = h.shape
        pos = jnp.arange(s, dtype=jnp.int32)
        z = h @ w_in[l]
        q_abs, q_pe, kv_lat, k_pe = mla_project(z[..., :MLA_IN], pos, q_norm[l], kv_norm[l], w_uq[l], w_uk[l])
        nb = s // Q_BLOCK
        qa = q_abs.reshape(b, nb, Q_BLOCK, MLA_HEADS, KV_RANK).swapaxes(0, 1)
        qp = q_pe.reshape(b, nb, Q_BLOCK, MLA_HEADS, MLA_ROPE).swapaxes(0, 1)
        qpos = pos.reshape(nb, Q_BLOCK)
        o_lat = lax.map(lambda a: mla_attend(a[0], a[1], kv_lat, k_pe, a[2], pos), (qa, qp, qpos))
        o_lat = o_lat.swapaxes(0, 1).reshape(b, s, MLA_HEADS, KV_RANK)
        o_mla = mla_output(o_lat, w_uv[l])
        q, k, v, g = ret_project(z[..., MLA_IN:], pos)
        nc = s // CHUNK
        to_blocks = lambda t: t.reshape(b, nc, CHUNK, *t.shape[2:]).swapaxes(0, 1)
        r0 = jnp.zeros((b, RET_HEADS, RET_DK, RET_DV), jnp.float32)
        r_fin, o_ret = lax.scan(lambda r, xs: retention_block(r, xs[0], xs[1], xs[2]), r0,
                                (to_blocks(q), to_blocks(k), to_blocks(v)))
        o_ret = ret_output(o_ret.swapaxes(0, 1).reshape(b, s, RET_HEADS, RET_DV), g)
        out = jnp.concatenate([o_mla, o_ret], axis=-1) @ w_out[l]
        return out, (kv_lat, k_pe, r_fin.astype(h.dtype))

    def mixer_sample(h, l):
        t = h.shape[1]
        pos = PAST_LEN + jnp.arange(t, dtype=jnp.int32)
        z = h @ w_in[l]
        q_abs, q_pe, kv_lat, k_pe = mla_project(z[..., :MLA_IN], pos, q_norm[l], kv_norm[l], w_uq[l], w_uk[l])
        kv_all = jnp.concatenate([cache_kv_latent[l].astype(kv_lat.dtype), kv_lat], axis=1)
        kpe_all = jnp.concatenate([cache_k_rope[l].astype(k_pe.dtype), k_pe], axis=1)
        k_pos = jnp.arange(PAST_LEN + t, dtype=jnp.int32)
        o_mla = mla_output(mla_attend(q_abs, q_pe, kv_all, kpe_all, pos, k_pos), w_uv[l])
        q, k, v, g = ret_project(z[..., MLA_IN:], pos)
        r_new, o_ret = retention_block(state_ret[l].astype(jnp.float32), q, k, v)
        o_ret = ret_output(o_ret, g)
        out = jnp.concatenate([o_mla, o_ret], axis=-1) @ w_out[l]
        return out, (kv_lat, k_pe, r_new.astype(h.dtype))

    def layer(x, c, l, mixer):
        mods = (jax.nn.silu(c) @ ada_w[l] + ada_b[l]).reshape(c.shape[0], 3 * N_SUB, D_MODEL)
        h = modulated_pre(x, norm_pre[l, 0], mods, 0)
        x = gated_post(x, swiglu(h, ffn_w1[l, 0], ffn_w3[l, 0], ffn_w2[l, 0]), norm_post[l, 0], mods, 0, 0.5)
        h = modulated_pre(x, norm_pre[l, 1], mods, 1)
        o, st = mixer(h, l)
        x = gated_post(x, o, norm_post[l, 1], mods, 1, 1.0)
        h = modulated_pre(x, norm_pre[l, 2], mods, 2)
        x = gated_post(x, swiglu(h, ffn_w1[l, 1], ffn_w3[l, 1], ffn_w2[l, 1]), norm_post[l, 2], mods, 2, 0.5)
        return x, st

    y_prompt, y_sample = x_prompt, x_sample
    lat_p, kpe_p, ret_p, lat_s, kpe_s, ret_s = [], [], [], [], [], []
    for l in range(DEPTH):
        y_prompt, (a, b_, r) = layer(y_prompt, c_prompt, l, mixer_prompt)
        lat_p.append(a); kpe_p.append(b_); ret_p.append(r)
        y_sample, (a, b_, r) = layer(y_sample, c_sample, l, mixer_sample)
        lat_s.append(a); kpe_s.append(b_); ret_s.append(r)
    new_kv_latent_prompt = jnp.stack(lat_p)
    new_k_rope_prompt = jnp.stack(kpe_p)
    new_state_ret_prompt = jnp.stack(ret_p)
    new_kv_latent_sample = jnp.stack(lat_s)
    new_k_rope_sample = jnp.stack(kpe_s)
    new_state_ret_sample = jnp.stack(ret_s)
    return (y_prompt, y_sample, new_kv_latent_prompt, new_k_rope_prompt, new_state_ret_prompt,
            new_kv_latent_sample, new_k_rope_sample, new_state_ret_sample)
```

```python
import functools

import jax
import jax.numpy as jnp
import numpy as np
from jax import lax
from jax.experimental import pallas as pl
from jax.experimental.pallas import tpu as pltpu

CHUNK = 64
N_SUB = 3
MLA_HEADS = 8
MLA_NOPE = 64
MLA_ROPE = 32
MLA_V = 64
Q_RANK = 256
KV_RANK = 128
RET_HEADS = 8
RET_DK = 64
RET_DV = 64
ROPE_BASE = 10000.0
EPS = 1e-6
MLA_SCALE = (MLA_NOPE + MLA_ROPE) ** -0.5
RET_SCALE = RET_DK ** -0.5
NEG_INF = -1e30

QK_DIM = KV_RANK + MLA_ROPE
RET_W = RET_HEADS * RET_DK
LANES = 128
KPE_BLOCK = LANES
W_IN_COLS = Q_RANK + KV_RANK + KPE_BLOCK + 4 * RET_W

F32 = jnp.float32
BF16 = jnp.bfloat16

VMEM_LIMIT = 56 * 1024 * 1024


def _params(*sem):
    return pltpu.CompilerParams(dimension_semantics=sem, vmem_limit_bytes=VMEM_LIMIT)


def _resident(block_shape, index_map):
    return pl.BlockSpec(block_shape, index_map, pipeline_mode=pl.Buffered(1))


def _rms(x):
    return x * lax.rsqrt(jnp.mean(x * x, axis=-1, keepdims=True) + EPS)


def _silu(x):
    return x * jax.nn.sigmoid(x)


def _dot(a, b):
    return jnp.dot(a, b, preferred_element_type=F32)


def _dot_nt(a, b):
    return lax.dot_general(a, b, (((1,), (1,)), ((), ())), preferred_element_type=F32)


def _dot_tn(a, b):
    return lax.dot_general(a, b, (((0,), (0,)), ((), ())), preferred_element_type=F32)


def _ada_kernel(c_ref, w_ref, b_ref, o_ref):
    sc = _silu(c_ref[...])
    o_ref[0] = jnp.dot(sc, w_ref[0], preferred_element_type=F32,
                       precision=lax.Precision.HIGHEST) + b_ref[0]


def _ada_mods(c_all, ada_w, ada_b):
    depth, d, cols = ada_w.shape
    nb = c_all.shape[0]
    n_col = cols // d
    out = pl.pallas_call(
        _ada_kernel,
        grid=(depth, n_col),
        in_specs=[
            pl.BlockSpec((nb, d), lambda l, j: (0, 0)),
            pl.BlockSpec((1, d, d), lambda l, j: (l, 0, j)),
            pl.BlockSpec((1, 1, d), lambda l, j: (l, 0, j)),
        ],
        out_specs=pl.BlockSpec((1, nb, d), lambda l, j: (l, 0, j)),
        out_shape=jax.ShapeDtypeStruct((depth, nb, cols), F32),
        compiler_params=_params("parallel", "parallel"),
        name="ada_mods",
    )(c_all, ada_w, ada_b.reshape(depth, 1, cols))
    return out.reshape(depth, nb, n_col, d)


def _ffn_kernel(x_ref, sh_ref, sc_ref, gt_ref, gpre_ref, gpost_ref, w1_ref, w3_ref, w2_ref, o_ref,
                *, coef, f_chunk):
    x = x_ref[0]
    h = (_rms(x) * gpre_ref[...] * (1.0 + sc_ref[0]) + sh_ref[0]).astype(BF16)
    d_ff = w1_ref.shape[1]
    o = None
    for c0 in range(0, d_ff, f_chunk):
        a = _dot(h, w1_ref[:, c0:c0 + f_chunk])
        b = _dot(h, w3_ref[:, c0:c0 + f_chunk])
        g = (_silu(a) * b).astype(BF16)
        part = _dot(g, w2_ref[c0:c0 + f_chunk, :])
        o = part if o is None else o + part
    o_ref[0] = x + (coef * gt_ref[0]) * (_rms(o) * gpost_ref[...])


def _mod_spec(mod, tm):
    if mod.shape[1] == 1:
        return pl.BlockSpec((1, 1, mod.shape[2]), lambda b, s: (b, 0, 0))
    return pl.BlockSpec((1, tm, mod.shape[2]), lambda b, s: (b, s, 0))


def _ffn(x, shift, scale, gate, g_pre, g_post, w1, w3, w2, l, k, tm):
    nseq, length, d = x.shape
    d_ff = w1.shape[-1]
    f_chunk = d_ff // 2 if (d_ff // 2) % LANES == 0 else d_ff
    w_up = _resident((None, None, d, d_ff), lambda b, s: (l, k, 0, 0))
    w_dn = _resident((None, None, d_ff, d), lambda b, s: (l, k, 0, 0))
    vec = pl.BlockSpec((1, d), lambda b, s: (0, 0))
    tile = pl.BlockSpec((1, tm, d), lambda b, s: (b, s, 0))
    return pl.pallas_call(
        functools.partial(_ffn_kernel, coef=0.5, f_chunk=f_chunk),
        grid=(nseq, length // tm),
        in_specs=[tile, _mod_spec(shift, tm), _mod_spec(scale, tm), _mod_spec(gate, tm),
                  vec, vec, w_up, w_up, w_dn],
        out_specs=tile,
        out_shape=jax.ShapeDtypeStruct(x.shape, F32),
        compiler_params=_params("parallel", "parallel"),
        name="ffn",
    )(x, shift, scale, gate, g_pre, g_post, w1, w3, w2)


def _proj_kernel(x_ref, sh_ref, sc_ref, gpre_ref, win_ref, qn_ref, kvn_ref, wuq_ref, wuk_ref,
                 cq_ref, sq_ref, ck_ref, sk_ref, rc_ref, rsa_ref, rsb_ref,
                 q_out, k_out, kv_out, kpe_out, rq_out, rk_out, rv_out, rg_out):
    x = x_ref[0]
    h = (_rms(x) * gpre_ref[...] * (1.0 + sc_ref[0]) + sh_ref[0]).astype(BF16)
    z = _dot(h, win_ref[...])

    q_lat = (_rms(z[:, :Q_RANK]) * qn_ref[...]).astype(BF16)
    kv_lat = _rms(z[:, Q_RANK:Q_RANK + KV_RANK]) * kvn_ref[...]
    c0 = Q_RANK + KV_RANK
    k_pe = (z[:, c0:c0 + MLA_ROPE] * ck_ref[...]
            + z[:, c0 + MLA_ROPE:c0 + 2 * MLA_ROPE] * sk_ref[...])
    kv_out[0] = kv_lat
    kpe_out[0] = k_pe
    k_out[0] = jnp.concatenate([kv_lat, k_pe], axis=-1).astype(BF16)

    n_nope = MLA_HEADS * MLA_NOPE
    n_pe = MLA_HEADS * MLA_ROPE
    qq = _dot(q_lat, wuq_ref[...])
    q_pe = qq[:, n_nope:n_nope + n_pe] * cq_ref[...] + qq[:, n_nope + n_pe:] * sq_ref[...]
    q_abs = _dot(qq[:, :n_nope].astype(BF16), wuk_ref[...])
    for hd in range(MLA_HEADS):
        qh = jnp.concatenate([q_abs[:, hd * KV_RANK:(hd + 1) * KV_RANK],
                              q_pe[:, hd * MLA_ROPE:(hd + 1) * MLA_ROPE]], axis=-1)
        q_out[0, hd] = (qh * MLA_SCALE).astype(BF16)

    r0 = c0 + KPE_BLOCK
    rc, rsa, rsb = rc_ref[...], rsa_ref[...], rsb_ref[...]
    for off, out, scale in ((r0, rq_out, 1.0), (r0 + RET_W, rk_out, RET_SCALE)):
        for c in range(RET_W // LANES):
            xc = z[:, off + c * LANES:off + (c + 1) * LANES]
            y = (xc * rc + pltpu.roll(xc, LANES - RET_DK // 2, 1) * rsa
                 + pltpu.roll(xc, RET_DK // 2, 1) * rsb)
            out[0, :, c * LANES:(c + 1) * LANES] = (y * scale).astype(BF16)
    rv_out[0] = z[:, r0 + 2 * RET_W:r0 + 3 * RET_W].astype(BF16)
    rg_out[0] = z[:, r0 + 3 * RET_W:r0 + 4 * RET_W].astype(BF16)


def _proj(x, shift, scale, g_pre, w_in, q_norm, kv_norm, w_uq, w_uk, tabs, l, tm):
    nseq, length, d = x.shape
    grid = (nseq, length // tm)
    vec = lambda n: pl.BlockSpec((1, n), lambda b, s: (0, 0))
    tile = lambda n: pl.BlockSpec((1, tm, n), lambda b, s: (b, s, 0))
    tab = lambda n: pl.BlockSpec((tm, n), lambda b, s: (s, 0))
    wl = lambda r, c: _resident((None, r, c), lambda b, s: (l, 0, 0))
    n_q = MLA_HEADS * KV_RANK
    bf = lambda n: jax.ShapeDtypeStruct((nseq, length, n), BF16)
    return pl.pallas_call(
        _proj_kernel,
        grid=grid,
        in_specs=[tile(d), _mod_spec(shift, tm), _mod_spec(scale, tm), vec(d),
                  wl(d, W_IN_COLS), vec(Q_RANK), vec(KV_RANK),
                  wl(Q_RANK, w_uq.shape[-1]), wl(MLA_HEADS * MLA_NOPE, n_q),
                  tab(MLA_HEADS * MLA_ROPE), tab(MLA_HEADS * MLA_ROPE), tab(MLA_ROPE), tab(MLA_ROPE),
                  tab(LANES), tab(LANES), tab(LANES)],
        out_specs=[pl.BlockSpec((1, MLA_HEADS, tm, QK_DIM), lambda b, s: (b, 0, s, 0)),
                   tile(QK_DIM), tile(KV_RANK), tile(MLA_ROPE),
                   tile(RET_W), tile(RET_W), tile(RET_W), tile(RET_W)],
        out_shape=[jax.ShapeDtypeStruct((nseq, MLA_HEADS, length, QK_DIM), BF16),
                   bf(QK_DIM),
                   jax.ShapeDtypeStruct((nseq, length, KV_RANK), F32),
                   jax.ShapeDtypeStruct((nseq, length, MLA_ROPE), F32),
                   bf(RET_W), bf(RET_W), bf(RET_W), bf(RET_W)],
        compiler_params=_params("parallel", "parallel"),
        name="mixer_proj",
    )(x, shift, scale, g_pre, w_in, q_norm, kv_norm, w_uq, w_uk, *tabs)


def _rope_tables(pos):
    pos = pos.astype(F32)[:, None]

    def cs(half):
        inv = ROPE_BASE ** (-jnp.arange(half, dtype=F32) / half)
        ang = pos * inv[None, :]
        return jnp.cos(ang), jnp.sin(ang)

    c16, s16 = cs(MLA_ROPE // 2)
    ck = jnp.concatenate([c16, c16], axis=-1)
    sk = jnp.concatenate([s16, s16], axis=-1)
    cq = jnp.tile(ck, (1, MLA_HEADS))
    sq = jnp.tile(sk, (1, MLA_HEADS))
    c32, s32 = cs(RET_DK // 2)
    zero = jnp.zeros_like(s32)
    reps = LANES // RET_DK
    rc = jnp.tile(jnp.concatenate([c32, c32], axis=-1), (1, reps))
    rsa = jnp.tile(jnp.concatenate([-s32, zero], axis=-1), (1, reps))
    rsb = jnp.tile(jnp.concatenate([zero, s32], axis=-1), (1, reps))
    return cq, sq, ck, sk, rc, rsa, rsb


def _attn_kernel(q_ref, k_ref, o_ref, m_scr, l_scr, acc_scr, *, tq, tk):
    i = pl.program_id(1)
    rows = MLA_HEADS * tq
    q = q_ref[0].reshape(rows, QK_DIM)
    m_scr[...] = jnp.full(m_scr.shape, NEG_INF, F32)
    l_scr[...] = jnp.zeros(l_scr.shape, F32)
    acc_scr[...] = jnp.zeros(acc_scr.shape, F32)
    n_full = (i * tq) // tk

    def step(j, masked):
        start = pl.multiple_of(j * tk, tk)
        kt = k_ref[0, pl.ds(start, tk), :]
        s = _dot_nt(q, kt)
        if masked:
            s = _chunk_mask(s, tq, i * tq, j * tk)
        m_prev = m_scr[...]
        m_new = jnp.maximum(m_prev, jnp.max(s, axis=1, keepdims=True))
        alpha = jnp.exp(m_prev - m_new)
        p = jnp.exp(s - m_new[:, :1])
        l_scr[...] = alpha * l_scr[...] + jnp.sum(p, axis=1, keepdims=True)
        acc_scr[...] = acc_scr[...] * alpha + _dot(p.astype(BF16), kt[:, :KV_RANK])
        m_scr[...] = m_new

    def body(j, carry):
        step(j, False)
        return carry

    lax.fori_loop(0, n_full, body, 0)
    step(n_full, True)

    o = (acc_scr[...] / l_scr[...]).astype(BF16)
    for hd in range(MLA_HEADS):
        o_ref[0, :, hd * KV_RANK:(hd + 1) * KV_RANK] = o[hd * tq:(hd + 1) * tq]


def _attention_prompt(q, k, tq, tk):
    nseq, _, length, _ = q.shape
    rows = MLA_HEADS * tq
    return pl.pallas_call(
        functools.partial(_attn_kernel, tq=tq, tk=tk),
        grid=(nseq, length // tq),
        in_specs=[pl.BlockSpec((1, MLA_HEADS, tq, QK_DIM), lambda b, i: (b, 0, i, 0)),
                  pl.BlockSpec((1, length, QK_DIM), lambda b, i: (b, 0, 0))],
        out_specs=pl.BlockSpec((1, tq, MLA_HEADS * KV_RANK), lambda b, i: (b, i, 0)),
        out_shape=jax.ShapeDtypeStruct((nseq, length, MLA_HEADS * KV_RANK), BF16),
        scratch_shapes=[pltpu.VMEM((rows, LANES), F32), pltpu.VMEM((rows, LANES), F32),
                        pltpu.VMEM((rows, KV_RANK), F32)],
        compiler_params=_params("parallel", "arbitrary"),
        name="mla_attention_prompt",
    )(q, k)


def _chunk_mask(s, t, q_start, k_start):
    n = s.shape[1]
    qpos = q_start + lax.broadcasted_iota(jnp.int32, (t, n), 0)
    kpos = k_start + lax.broadcasted_iota(jnp.int32, (t, n), 1)
    allowed = (kpos // CHUNK) <= (qpos // CHUNK)
    return jnp.where(allowed[None], s.reshape(MLA_HEADS, t, n), NEG_INF).reshape(s.shape)


def _attn_sample_kernel(q_ref, ckv_ref, ckpe_ref, kn_ref, o_ref, *, t, past, mask_cache, mask_new):
    rows = MLA_HEADS * t
    q = q_ref[0].reshape(rows, QK_DIM)
    ckv = ckv_ref[0].astype(BF16)
    ckpe = ckpe_ref[0].astype(BF16)
    kn = kn_ref[0]
    s_c = _dot_nt(q[:, :KV_RANK], ckv) + _dot_nt(q[:, KV_RANK:], ckpe)
    s_n = _dot_nt(q, kn)
    if mask_cache:
        s_c = _chunk_mask(s_c, t, past, 0)
    if mask_new:
        s_n = _chunk_mask(s_n, t, past, past)
    m = jnp.maximum(jnp.max(s_c, axis=1, keepdims=True), jnp.max(s_n, axis=1, keepdims=True))
    p_c = jnp.exp(s_c - m)
    p_n = jnp.exp(s_n - m)
    denom = jnp.sum(p_c, axis=1, keepdims=True) + jnp.sum(p_n, axis=1, keepdims=True)
    o = _dot(p_c.astype(BF16), ckv) + _dot(p_n.astype(BF16), kn[:, :KV_RANK])
    o = (o / denom).astype(BF16)
    for hd in range(MLA_HEADS):
        o_ref[0, :, hd * KV_RANK:(hd + 1) * KV_RANK] = o[hd * t:(hd + 1) * t]


def _attention_sample(q, cache_kv, cache_kpe, k_new, l):
    nb, _, t, _ = q.shape
    past = cache_kv.shape[2]
    q_chunk = (past + np.arange(t)) // CHUNK
    k_chunk = np.arange(past + t) // CHUNK
    allowed = k_chunk[None, :] <= q_chunk[:, None]
    mask_cache = not allowed[:, :past].all()
    mask_new = not allowed[:, past:].all()
    return pl.pallas_call(
        functools.partial(_attn_sample_kernel, t=t, past=past, mask_cache=mask_cache, mask_new=mask_new),
        grid=(nb,),
        in_specs=[pl.BlockSpec((1, MLA_HEADS, t, QK_DIM), lambda b: (b, 0, 0, 0)),
                  pl.BlockSpec((None, 1, past, KV_RANK), lambda b: (l, b, 0, 0)),
                  pl.BlockSpec((None, 1, past, MLA_ROPE), lambda b: (l, b, 0, 0)),
                  pl.BlockSpec((1, t, QK_DIM), lambda b: (b, 0, 0))],
        out_specs=pl.BlockSpec((1, t, MLA_HEADS * KV_RANK), lambda b: (b, 0, 0)),
        out_shape=jax.ShapeDtypeStruct((nb, t, MLA_HEADS * KV_RANK), BF16),
        compiler_params=_params("parallel"),
        name="mla_attention_sample",
    )(q, cache_kv, cache_kpe, k_new)


def _ret_kernel(rq_ref, rk_ref, rv_ref, rg_ref, s0_ref, dec_ref, cross_ref, kdec_ref, gt_ref,
                o_ref, sout_ref, st_scr):
    c = pl.program_id(1)

    @pl.when(c == 0)
    def _():
        st_scr[...] = s0_ref[0]

    for hd in range(RET_HEADS):
        ks = slice(hd * RET_DK, (hd + 1) * RET_DK)
        vs = slice(hd * RET_DV, (hd + 1) * RET_DV)
        qh, kh, vh = rq_ref[0, :, ks], rk_ref[0, :, ks], rv_ref[0, :, vs]
        st = st_scr[hd]
        scores = (_dot_nt(qh, kh) * dec_ref[hd]).astype(BF16)
        o = _dot(scores, vh) + _dot(qh, st.astype(BF16)) * cross_ref[hd]
        k_dec = (kh.astype(F32) * kdec_ref[hd]).astype(BF16)
        st_scr[hd] = gt_ref[hd] * st + _dot_tn(k_dec, vh)
        g = rg_ref[0, :, vs].astype(F32)
        o_ref[0, :, vs] = (_rms(o) * _silu(g)).astype(BF16)

    @pl.when(c == pl.num_programs(1) - 1)
    def _():
        sout_ref[0] = st_scr[...]


def _ret_tables(t):
    log_g = jnp.log1p(-jnp.exp2(-5.0 - jnp.arange(RET_HEADS, dtype=F32)))
    n = jnp.arange(t, dtype=F32)
    diff = n[:, None] - n[None, :]
    dec = jnp.where(diff >= 0.0, jnp.exp(jnp.maximum(diff, 0.0)[None] * log_g[:, None, None]), 0.0)
    cross = jnp.exp((n + 1.0)[None, :] * log_g[:, None])
    kdec = jnp.exp((t - 1.0 - n)[None, :] * log_g[:, None])
    gt = jnp.exp(t * log_g)
    cross = jnp.broadcast_to(cross[:, :, None], (RET_HEADS, t, RET_DV))
    kdec = jnp.broadcast_to(kdec[:, :, None], (RET_HEADS, t, RET_DK))
    gt = jnp.broadcast_to(gt[:, None, None], (RET_HEADS, RET_DK, RET_DV))
    return dec, cross, kdec, gt


def _retention(rq, rk, rv, rg, state0, tabs, t):
    nseq, length, _ = rq.shape
    tile = pl.BlockSpec((1, t, RET_W), lambda b, c: (b, c, 0))
    st = pl.BlockSpec((1, RET_HEADS, RET_DK, RET_DV), lambda b, c: (b, 0, 0, 0))
    const = lambda a: pl.BlockSpec(a.shape, lambda b, c: (0, 0, 0))
    return pl.pallas_call(
        _ret_kernel,
        grid=(nseq, length // t),
        in_specs=[tile, tile, tile, tile, st] + [const(a) for a in tabs],
        out_specs=[tile, st],
        out_shape=[jax.ShapeDtypeStruct((nseq, length, RET_W), BF16),
                   jax.ShapeDtypeStruct(state0.shape, F32)],
        scratch_shapes=[pltpu.VMEM((RET_HEADS, RET_DK, RET_DV), F32)],
        compiler_params=_params("parallel", "arbitrary"),
        name="retention",
    )(rq, rk, rv, rg, state0, *tabs)


def _out_kernel(x_ref, ol_ref, or_ref, gt_ref, gpost_ref, wuv_ref, wo_ref, o_ref):
    n_mla = MLA_HEADS * MLA_V
    o_mla = _dot(ol_ref[0], wuv_ref[...]).astype(BF16)
    mix = _dot(o_mla, wo_ref[:n_mla, :]) + _dot(or_ref[0], wo_ref[n_mla:, :])
    o_ref[0] = x_ref[0] + gt_ref[0] * (_rms(mix) * gpost_ref[...])


def _mixer_out(x, o_lat, o_ret, gate, g_post, w_uv, w_out, l, tm):
    nseq, length, d = x.shape
    tile = lambda n: pl.BlockSpec((1, tm, n), lambda b, s: (b, s, 0))
    wl = lambda r, c: _resident((None, r, c), lambda b, s: (l, 0, 0))
    return pl.pallas_call(
        _out_kernel,
        grid=(nseq, length // tm),
        in_specs=[tile(d), tile(o_lat.shape[-1]), tile(RET_W), _mod_spec(gate, tm),
                  pl.BlockSpec((1, d), lambda b, s: (0, 0)),
                  wl(*w_uv.shape[1:]), wl(*w_out.shape[1:])],
        out_specs=tile(d),
        out_shape=jax.ShapeDtypeStruct(x.shape, F32),
        compiler_params=_params("parallel", "parallel"),
        name="mixer_out",
    )(x, o_lat, o_ret, gate, g_post, w_uv, w_out)


def _prep_w_in(w_in):
    c0 = Q_RANK + KV_RANK
    half = MLA_ROPE // 2
    kpe = w_in[..., c0:c0 + MLA_ROPE]
    kpe_rot = jnp.concatenate([-kpe[..., half:], kpe[..., :half]], axis=-1)
    pad = jnp.zeros(w_in.shape[:-1] + (KPE_BLOCK - 2 * MLA_ROPE,), w_in.dtype)
    return jnp.concatenate([w_in[..., :c0], kpe, kpe_rot, pad, w_in[..., c0 + MLA_ROPE:]],
                           axis=-1).astype(BF16)


def _prep_w_uq(w_uq):
    depth = w_uq.shape[0]
    half = MLA_ROPE // 2
    w = w_uq.reshape(depth, Q_RANK, MLA_HEADS, MLA_NOPE + MLA_ROPE)
    nope = w[..., :MLA_NOPE].reshape(depth, Q_RANK, MLA_HEADS * MLA_NOPE)
    pe = w[..., MLA_NOPE:]
    rot = jnp.concatenate([-pe[..., half:], pe[..., :half]], axis=-1)
    flat = lambda a: a.reshape(depth, Q_RANK, MLA_HEADS * MLA_ROPE)
    return jnp.concatenate([nope, flat(pe), flat(rot)], axis=-1).astype(BF16)


def _prep_w_uk(w_uk):
    depth = w_uk.shape[0]
    w = w_uk.reshape(depth, KV_RANK, MLA_HEADS, MLA_NOPE)
    eye = jnp.eye(MLA_HEADS, dtype=w_uk.dtype)
    bd = jnp.einsum('lrhn,hg->lhngr', w, eye)
    return bd.reshape(depth, MLA_HEADS * MLA_NOPE, MLA_HEADS * KV_RANK).astype(BF16)


def _prep_w_uv(w_uv):
    depth = w_uv.shape[0]
    w = w_uv.reshape(depth, KV_RANK, MLA_HEADS, MLA_V)
    eye = jnp.eye(MLA_HEADS, dtype=w_uv.dtype)
    bd = jnp.einsum('lrhv,hg->lhrgv', w, eye)
    return bd.reshape(depth, MLA_HEADS * KV_RANK, MLA_HEADS * MLA_V).astype(BF16)


def _pick_tile(length, target):
    t = min(length, target)
    while length % t:
        t //= 2
    return t


def kernel(x_prompt, x_sample, cache_kv_latent, cache_k_rope, state_ret, c_prompt, c_sample,
           ada_w, ada_b, norm_pre, norm_post, ffn_w1, ffn_w3, ffn_w2,
           w_in, q_norm, kv_norm, w_uq, w_uk, w_uv, w_out):
    depth = ada_w.shape[0]
    nb_p, seq, d = x_prompt.shape
    nb_s, dec_seq, _ = x_sample.shape
    past = cache_kv_latent.shape[2]
    n_tok_s = nb_s * dec_seq

    w1, w3, w2 = ffn_w1.astype(BF16), ffn_w3.astype(BF16), ffn_w2.astype(BF16)
    w_in_b = _prep_w_in(w_in)
    w_uq_b = _prep_w_uq(w_uq)
    w_uk_b = _prep_w_uk(w_uk)
    w_uv_b = _prep_w_uv(w_uv)
    w_out_b = w_out.astype(BF16)

    mods = _ada_mods(jnp.concatenate([c_prompt, c_sample], axis=0), ada_w, ada_b)

    tm_p = _pick_tile(seq, 512)
    tq = _pick_tile(seq, 128)
    tk = _pick_tile(seq, 512)
    t_ret = _pick_tile(seq, 256)
    tabs_p = _rope_tables(jnp.arange(seq, dtype=jnp.int32))
    tabs_s = _rope_tables(jnp.tile(past + jnp.arange(dec_seq, dtype=jnp.int32), nb_s))
    ret_tabs_p = _ret_tables(t_ret)
    ret_tabs_s = _ret_tables(dec_seq)
    zero_state = jnp.zeros((nb_p, RET_HEADS, RET_DK, RET_DV), F32)

    def mod_p(l, r):
        return mods[l, :nb_p, r][:, None, :]

    def mod_s(l, r):
        return jnp.repeat(mods[l, nb_p:, r], dec_seq, axis=0)[None]

    y_p = x_prompt
    y_s = x_sample.reshape(1, n_tok_s, d)
    lat_p, kpe_p, ret_p, lat_s, kpe_s, ret_s = [], [], [], [], [], []

    for l in range(depth):
        gp = lambda j: norm_pre[l, j][None, :]
        go = lambda j: norm_post[l, j][None, :]
        qn, kvn = q_norm[l][None, :], kv_norm[l][None, :]

        y_p = _ffn(y_p, mod_p(l, 0), mod_p(l, 1), mod_p(l, 2), gp(0), go(0), w1, w3, w2, l, 0, tm_p)
        q, k, kv, kpe, rq, rk, rv, rg = _proj(y_p, mod_p(l, 3), mod_p(l, 4), gp(1), w_in_b, qn, kvn,
                                              w_uq_b, w_uk_b, tabs_p, l, tm_p)
        o_lat = _attention_prompt(q, k, tq, tk)
        o_ret, r_fin = _retention(rq, rk, rv, rg, zero_state, ret_tabs_p, t_ret)
        y_p = _mixer_out(y_p, o_lat, o_ret, mod_p(l, 5), go(1), w_uv_b, w_out_b, l, tm_p)
        y_p = _ffn(y_p, mod_p(l, 6), mod_p(l, 7), mod_p(l, 8), gp(2), go(2), w1, w3, w2, l, 1, tm_p)
        lat_p.append(kv); kpe_p.append(kpe); ret_p.append(r_fin)

        y_s = _ffn(y_s, mod_s(l, 0), mod_s(l, 1), mod_s(l, 2), gp(0), go(0), w1, w3, w2, l, 0, n_tok_s)
        q, k, kv, kpe, rq, rk, rv, rg = _proj(y_s, mod_s(l, 3), mod_s(l, 4), gp(1), w_in_b, qn, kvn,
                                              w_uq_b, w_uk_b, tabs_s, l, n_tok_s)
        per_seq = lambda a: a.reshape(nb_s, dec_seq, a.shape[-1])
        q = q.reshape(MLA_HEADS, nb_s, dec_seq, QK_DIM).swapaxes(0, 1)
        o_lat = _attention_sample(q, cache_kv_latent, cache_k_rope, per_seq(k), l)
        o_ret, r_new = _retention(per_seq(rq), per_seq(rk), per_seq(rv), per_seq(rg),
                                  state_ret[l], ret_tabs_s, dec_seq)
        y_s = _mixer_out(y_s, o_lat.reshape(1, n_tok_s, -1), o_ret.reshape(1, n_tok_s, -1),
                         mod_s(l, 5), go(1), w_uv_b, w_out_b, l, n_tok_s)
        y_s = _ffn(y_s, mod_s(l, 6), mod_s(l, 7), mod_s(l, 8), gp(2), go(2), w1, w3, w2, l, 1, n_tok_s)
        lat_s.append(per_seq(kv)); kpe_s.append(per_seq(kpe)); ret_s.append(r_new)

    return (y_p, y_s.reshape(nb_s, dec_seq, d),
            jnp.stack(lat_p), jnp.stack(kpe_p), jnp.stack(ret_p),
            jnp.stack(lat_s), jnp.stack(kpe_s), jnp.stack(ret_s))
```

```python
import functools

import jax
import jax.numpy as jnp
import numpy as np
from jax import lax
from jax.experimental import pallas as pl
from jax.experimental.pallas import tpu as pltpu

CHUNK = 64
N_SUB = 3
MLA_HEADS = 8
MLA_NOPE = 64
MLA_ROPE = 32
MLA_V = 64
Q_RANK = 256
KV_RANK = 128
RET_HEADS = 8
RET_DK = 64
RET_DV = 64
ROPE_BASE = 10000.0
EPS = 1e-6
MLA_SCALE = (MLA_NOPE + MLA_ROPE) ** -0.5
Q_SCALE = MLA_SCALE * float(np.log2(np.e))
MAX_GROUPS = 8
RET_SCALE = RET_DK ** -0.5
NEG_INF = -1e30

QK_DIM = KV_RANK + MLA_ROPE
RET_W = RET_HEADS * RET_DK
LANES = 128
KPE_BLOCK = LANES
BF16_SUBLANES = 16
VT_ROWS = KV_RANK + BF16_SUBLANES
W_IN_COLS = Q_RANK + KV_RANK + KPE_BLOCK + 4 * RET_W

F32 = jnp.float32
BF16 = jnp.bfloat16

VMEM_LIMIT = 56 * 1024 * 1024


def _params(*sem):
    return pltpu.CompilerParams(dimension_semantics=sem, vmem_limit_bytes=VMEM_LIMIT)


def _resident(block_shape, index_map):
    return pl.BlockSpec(block_shape, index_map, pipeline_mode=pl.Buffered(1))


def _rms(x):
    return x * lax.rsqrt(jnp.mean(x * x, axis=-1, keepdims=True) + EPS)


def _silu(x):
    return x * jax.nn.sigmoid(x)


def _dot(a, b):
    return jnp.dot(a, b, preferred_element_type=F32)


def _dot_nt(a, b):
    return lax.dot_general(a, b, (((1,), (1,)), ((), ())), preferred_element_type=F32)


def _dot_tn(a, b):
    return lax.dot_general(a, b, (((0,), (0,)), ((), ())), preferred_element_type=F32)


def _ada_kernel(c_ref, w_ref, b_ref, o_ref):
    sc = _silu(c_ref[...])
    o_ref[0] = jnp.dot(sc, w_ref[0], preferred_element_type=F32,
                       precision=lax.Precision.HIGHEST) + b_ref[0]


def _ada_mods(c_all, ada_w, ada_b):
    depth, d, cols = ada_w.shape
    nb = c_all.shape[0]
    n_col = cols // d
    out = pl.pallas_call(
        _ada_kernel,
        grid=(depth, n_col),
        in_specs=[
            pl.BlockSpec((nb, d), lambda l, j: (0, 0)),
            pl.BlockSpec((1, d, d), lambda l, j: (l, 0, j)),
            pl.BlockSpec((1, 1, d), lambda l, j: (l, 0, j)),
        ],
        out_specs=pl.BlockSpec((1, nb, d), lambda l, j: (l, 0, j)),
        out_shape=jax.ShapeDtypeStruct((depth, nb, cols), F32),
        compiler_params=_params("parallel", "parallel"),
        name="ada_mods",
    )(c_all, ada_w, ada_b.reshape(depth, 1, cols))
    return out.reshape(depth, nb, n_col, d)


def _ffn_kernel(x_ref, sh_ref, sc_ref, gt_ref, gpre_ref, gpost_ref, w1_ref, w3_ref, w2_ref, o_ref,
                *, coef, f_chunk):
    x = x_ref[0]
    h = (_rms(x) * gpre_ref[...] * (1.0 + sc_ref[0]) + sh_ref[0]).astype(BF16)
    d_ff = w1_ref.shape[1]
    o = None
    for c0 in range(0, d_ff, f_chunk):
        a = _dot(h, w1_ref[:, c0:c0 + f_chunk])
        b = _dot(h, w3_ref[:, c0:c0 + f_chunk])
        g = (_silu(a) * b).astype(BF16)
        part = _dot(g, w2_ref[c0:c0 + f_chunk, :])
        o = part if o is None else o + part
    o_ref[0] = x + (coef * gt_ref[0]) * (_rms(o) * gpost_ref[...])


def _mod_spec(mod, tm):
    if mod.shape[1] == 1:
        return pl.BlockSpec((1, 1, mod.shape[2]), lambda b, s: (b, 0, 0))
    return pl.BlockSpec((1, tm, mod.shape[2]), lambda b, s: (b, s, 0))


def _ffn(x, shift, scale, gate, g_pre, g_post, w1, w3, w2, l, k, tm):
    nseq, length, d = x.shape
    d_ff = w1.shape[-1]
    f_chunk = d_ff // 2 if (d_ff // 2) % LANES == 0 else d_ff
    w_up = _resident((None, None, d, d_ff), lambda b, s: (l, k, 0, 0))
    w_dn = _resident((None, None, d_ff, d), lambda b, s: (l, k, 0, 0))
    vec = pl.BlockSpec((1, d), lambda b, s: (0, 0))
    tile = pl.BlockSpec((1, tm, d), lambda b, s: (b, s, 0))
    return pl.pallas_call(
        functools.partial(_ffn_kernel, coef=0.5, f_chunk=f_chunk),
        grid=(nseq, length // tm),
        in_specs=[tile, _mod_spec(shift, tm), _mod_spec(scale, tm), _mod_spec(gate, tm),
                  vec, vec, w_up, w_up, w_dn],
        out_specs=tile,
        out_shape=jax.ShapeDtypeStruct(x.shape, F32),
        compiler_params=_params("parallel", "parallel"),
        name="ffn",
    )(x, shift, scale, gate, g_pre, g_post, w1, w3, w2)


def _proj_kernel(x_ref, sh_ref, sc_ref, gpre_ref, win_ref, qn_ref, kvn_ref, wuq_ref, wuk_ref,
                 cq_ref, sq_ref, ck_ref, sk_ref, rc_ref, rsa_ref, rsb_ref,
                 q_out, k_out, vt_out, kv_out, kpe_out, rq_out, rk_out, rv_out, rg_out):
    x = x_ref[0]
    h = (_rms(x) * gpre_ref[...] * (1.0 + sc_ref[0]) + sh_ref[0]).astype(BF16)
    z = _dot(h, win_ref[...])

    q_lat = (_rms(z[:, :Q_RANK]) * qn_ref[...]).astype(BF16)
    kv_lat = _rms(z[:, Q_RANK:Q_RANK + KV_RANK]) * kvn_ref[...]
    c0 = Q_RANK + KV_RANK
    k_pe = (z[:, c0:c0 + MLA_ROPE] * ck_ref[...]
            + z[:, c0 + MLA_ROPE:c0 + 2 * MLA_ROPE] * sk_ref[...])
    kv_out[0] = kv_lat
    kpe_out[0] = k_pe
    k_out[0] = jnp.concatenate([kv_lat, k_pe], axis=-1).astype(BF16)
    ones = jnp.ones((VT_ROWS - KV_RANK, kv_lat.shape[0]), F32)
    vt_out[0, 0] = jnp.concatenate([kv_lat.T, ones], axis=0).astype(BF16)

    n_nope = MLA_HEADS * MLA_NOPE
    n_pe = MLA_HEADS * MLA_ROPE
    qq = _dot(q_lat, wuq_ref[...])
    q_pe = qq[:, n_nope:n_nope + n_pe] * cq_ref[...] + qq[:, n_nope + n_pe:] * sq_ref[...]
    q_abs = _dot(qq[:, :n_nope].astype(BF16), wuk_ref[...])
    for hd in range(MLA_HEADS):
        qh = jnp.concatenate([q_abs[:, hd * KV_RANK:(hd + 1) * KV_RANK],
                              q_pe[:, hd * MLA_ROPE:(hd + 1) * MLA_ROPE]], axis=-1)
        q_out[0, hd] = (qh * Q_SCALE).astype(BF16)

    r0 = c0 + KPE_BLOCK
    rc, rsa, rsb = rc_ref[...], rsa_ref[...], rsb_ref[...]
    for off, out, scale in ((r0, rq_out, 1.0), (r0 + RET_W, rk_out, RET_SCALE)):
        for c in range(RET_W // LANES):
            xc = z[:, off + c * LANES:off + (c + 1) * LANES]
            y = (xc * rc + pltpu.roll(xc, LANES - RET_DK // 2, 1) * rsa
                 + pltpu.roll(xc, RET_DK // 2, 1) * rsb)
            out[0, :, c * LANES:(c + 1) * LANES] = (y * scale).astype(BF16)
    rv_out[0] = z[:, r0 + 2 * RET_W:r0 + 3 * RET_W].astype(BF16)
    rg_out[0] = z[:, r0 + 3 * RET_W:r0 + 4 * RET_W].astype(BF16)


def _proj(x, shift, scale, g_pre, w_in, q_norm, kv_norm, w_uq, w_uk, tabs, l, tm):
    nseq, length, d = x.shape
    grid = (nseq, length // tm)
    vec = lambda n: pl.BlockSpec((1, n), lambda b, s: (0, 0))
    tile = lambda n: pl.BlockSpec((1, tm, n), lambda b, s: (b, s, 0))
    tab = lambda n: pl.BlockSpec((tm, n), lambda b, s: (s, 0))
    wl = lambda r, c: _resident((None, r, c), lambda b, s: (l, 0, 0))
    n_q = MLA_HEADS * KV_RANK
    bf = lambda n: jax.ShapeDtypeStruct((nseq, length, n), BF16)
    return pl.pallas_call(
        _proj_kernel,
        grid=grid,
        in_specs=[tile(d), _mod_spec(shift, tm), _mod_spec(scale, tm), vec(d),
                  wl(d, W_IN_COLS), vec(Q_RANK), vec(KV_RANK),
                  wl(Q_RANK, w_uq.shape[-1]), wl(MLA_HEADS * MLA_NOPE, n_q),
                  tab(MLA_HEADS * MLA_ROPE), tab(MLA_HEADS * MLA_ROPE), tab(MLA_ROPE), tab(MLA_ROPE),
                  tab(LANES), tab(LANES), tab(LANES)],
        out_specs=[pl.BlockSpec((1, MLA_HEADS, tm, QK_DIM), lambda b, s: (b, 0, s, 0)),
                   tile(QK_DIM),
                   pl.BlockSpec((1, 1, VT_ROWS, tm), lambda b, s: (b, s, 0, 0)),
                   tile(KV_RANK), tile(MLA_ROPE),
                   tile(RET_W), tile(RET_W), tile(RET_W), tile(RET_W)],
        out_shape=[jax.ShapeDtypeStruct((nseq, MLA_HEADS, length, QK_DIM), BF16),
                   bf(QK_DIM),
                   jax.ShapeDtypeStruct((nseq, length // tm, VT_ROWS, tm), BF16),
                   jax.ShapeDtypeStruct((nseq, length, KV_RANK), F32),
                   jax.ShapeDtypeStruct((nseq, length, MLA_ROPE), F32),
                   bf(RET_W), bf(RET_W), bf(RET_W), bf(RET_W)],
        compiler_params=_params("parallel", "parallel"),
        name="mixer_proj",
    )(x, shift, scale, g_pre, w_in, q_norm, kv_norm, w_uq, w_uk, *tabs)


def _rope_tables(pos):
    pos = pos.astype(F32)[:, None]

    def cs(half):
        inv = ROPE_BASE ** (-jnp.arange(half, dtype=F32) / half)
        ang = pos * inv[None, :]
        return jnp.cos(ang), jnp.sin(ang)

    c16, s16 = cs(MLA_ROPE // 2)
    ck = jnp.concatenate([c16, c16], axis=-1)
    sk = jnp.concatenate([s16, s16], axis=-1)
    cq = jnp.tile(ck, (1, MLA_HEADS))
    sq = jnp.tile(sk, (1, MLA_HEADS))
    c32, s32 = cs(RET_DK // 2)
    zero = jnp.zeros_like(s32)
    reps = LANES // RET_DK
    rc = jnp.tile(jnp.concatenate([c32, c32], axis=-1), (1, reps))
    rsa = jnp.tile(jnp.concatenate([-s32, zero], axis=-1), (1, reps))
    rsb = jnp.tile(jnp.concatenate([zero, s32], axis=-1), (1, reps))
    return cq, sq, ck, sk, rc, rsa, rsb


def _attn_kernel(q_ref, k_ref, vt_ref, o_ref, m_scr, alpha_scr, p_scr, acc_scr, *, tq, tk, cb):
    i = pl.program_id(1)
    rows = MLA_HEADS * tq
    q = q_ref[0].reshape(rows, QK_DIM)
    m_scr[...] = jnp.full(m_scr.shape, NEG_INF, F32)
    acc_scr[...] = jnp.zeros(acc_scr.shape, F32)
    n_full = (i * tq) // tk

    def scores(j):
        start = pl.multiple_of(j * tk, tk)
        return _dot_nt(k_ref[0, pl.ds(start, tk), :], q)

    def values(j):
        for c0 in range(0, rows, cb):
            cs = slice(c0, c0 + cb)
            acc_scr[:, cs] = acc_scr[:, cs] * alpha_scr[:, cs] + _dot(vt_ref[0, j], p_scr[:, cs])

    def softmax(j, s, masked):
        if masked:
            kpos = j * tk + lax.broadcasted_iota(jnp.int32, (tk, tq), 0)
            qpos = i * tq + lax.broadcasted_iota(jnp.int32, (tk, tq), 1)
            allowed = (kpos // CHUNK) <= (qpos // CHUNK)
            allowed = jnp.concatenate([allowed] * (cb // tq), axis=1)
        for c0 in range(0, rows, cb):
            cs = slice(c0, c0 + cb)
            sc = s[:, cs]
            if masked:
                sc = jnp.where(allowed, sc, NEG_INF)
            m_prev = m_scr[:, cs]
            m_part = jnp.max(sc.reshape(MAX_GROUPS, tk // MAX_GROUPS, cb), axis=0)
            m_new = jnp.maximum(m_prev, jnp.max(m_part, axis=0, keepdims=True))
            alpha_scr[:, cs] = jnp.exp2(m_prev - m_new)
            p_scr[:, cs] = jnp.exp2((sc - m_new).astype(BF16))
            m_scr[:, cs] = m_new

    def pipelined(j, masked):
        s = scores(j)
        values(j - 1)
        softmax(j, s, masked)

    @pl.when(n_full > 0)
    def _():
        softmax(0, scores(0), False)

        def body(j, carry):
            pipelined(j, False)
            return carry

        lax.fori_loop(1, n_full, body, 0)
        pipelined(n_full, True)

    @pl.when(n_full == 0)
    def _():
        softmax(0, scores(0), True)

    values(n_full)

    acc = acc_scr[...]
    o_t = acc[:KV_RANK] / acc[KV_RANK:KV_RANK + 1]
    for hd in range(MLA_HEADS):
        o_ref[0, :, hd * KV_RANK:(hd + 1) * KV_RANK] = o_t[:, hd * tq:(hd + 1) * tq].T.astype(BF16)


def _attention_prompt(q, k, vt, tq, tk):
    nseq, _, length, _ = q.shape
    rows = MLA_HEADS * tq
    assert vt.shape[-1] == tk and tk % tq == 0
    cb = 2 * tq
    return pl.pallas_call(
        functools.partial(_attn_kernel, tq=tq, tk=tk, cb=cb),
        grid=(nseq, length // tq),
        in_specs=[pl.BlockSpec((1, MLA_HEADS, tq, QK_DIM), lambda b, i: (b, 0, i, 0)),
                  pl.BlockSpec((1, length, QK_DIM), lambda b, i: (b, 0, 0)),
                  pl.BlockSpec((1, length // tk, VT_ROWS, tk), lambda b, i: (b, 0, 0, 0))],
        out_specs=pl.BlockSpec((1, tq, MLA_HEADS * KV_RANK), lambda b, i: (b, i, 0)),
        out_shape=jax.ShapeDtypeStruct((nseq, length, MLA_HEADS * KV_RANK), BF16),
        scratch_shapes=[pltpu.VMEM((1, rows), F32), pltpu.VMEM((1, rows), F32),
                        pltpu.VMEM((tk, rows), BF16), pltpu.VMEM((VT_ROWS, rows), F32)],
        compiler_params=_params("parallel", "arbitrary"),
        name="mla_attention_prompt",
    )(q, k, vt)


def _chunk_mask(s, t, q_start, k_start):
    n = s.shape[1]
    qpos = q_start + lax.broadcasted_iota(jnp.int32, (t, n), 0)
    kpos = k_start + lax.broadcasted_iota(jnp.int32, (t, n), 1)
    allowed = (kpos // CHUNK) <= (qpos // CHUNK)
    return jnp.where(allowed[None], s.reshape(MLA_HEADS, t, n), NEG_INF).reshape(s.shape)


def _attn_sample_kernel(q_ref, ckv_ref, ckpe_ref, kn_ref, o_ref, *, t, past, mask_cache, mask_new):
    rows = MLA_HEADS * t
    q = q_ref[0].reshape(rows, QK_DIM)
    ckv = ckv_ref[0].astype(BF16)
    ckpe = ckpe_ref[0].astype(BF16)
    kn = kn_ref[0]
    s_c = _dot_nt(q[:, :KV_RANK], ckv) + _dot_nt(q[:, KV_RANK:], ckpe)
    s_n = _dot_nt(q, kn)
    if mask_cache:
        s_c = _chunk_mask(s_c, t, past, 0)
    if mask_new:
        s_n = _chunk_mask(s_n, t, past, past)
    m = jnp.maximum(jnp.max(s_c, axis=1, keepdims=True), jnp.max(s_n, axis=1, keepdims=True))
    p_c = jnp.exp2(s_c - m)
    p_n = jnp.exp2(s_n - m)
    denom = jnp.sum(p_c, axis=1, keepdims=True) + jnp.sum(p_n, axis=1, keepdims=True)
    o = _dot(p_c.astype(BF16), ckv) + _dot(p_n.astype(BF16), kn[:, :KV_RANK])
    o = (o / denom).astype(BF16)
    for hd in range(MLA_HEADS):
        o_ref[0, :, hd * KV_RANK:(hd + 1) * KV_RANK] = o[hd * t:(hd + 1) * t]


def _attention_sample(q, cache_kv, cache_kpe, k_new, l):
    nb, _, t, _ = q.shape
    past = cache_kv.shape[2]
    q_chunk = (past + np.arange(t)) // CHUNK
    k_chunk = np.arange(past + t) // CHUNK
    allowed = k_chunk[None, :] <= q_chunk[:, None]
    mask_cache = not allowed[:, :past].all()
    mask_new = not allowed[:, past:].all()
    return pl.pallas_call(
        functools.partial(_attn_sample_kernel, t=t, past=past, mask_cache=mask_cache, mask_new=mask_new),
        grid=(nb,),
        in_specs=[pl.BlockSpec((1, MLA_HEADS, t, QK_DIM), lambda b: (b, 0, 0, 0)),
                  pl.BlockSpec((None, 1, past, KV_RANK), lambda b: (l, b, 0, 0)),
                  pl.BlockSpec((None, 1, past, MLA_ROPE), lambda b: (l, b, 0, 0)),
                  pl.BlockSpec((1, t, QK_DIM), lambda b: (b, 0, 0))],
        out_specs=pl.BlockSpec((1, t, MLA_HEADS * KV_RANK), lambda b: (b, 0, 0)),
        out_shape=jax.ShapeDtypeStruct((nb, t, MLA_HEADS * KV_RANK), BF16),
        compiler_params=_params("parallel"),
        name="mla_attention_sample",
    )(q, cache_kv, cache_kpe, k_new)


def _ret_kernel(rq_ref, rk_ref, rv_ref, rg_ref, s0_ref, dec_ref, cross_ref, kdec_ref, gt_ref,
                o_ref, sout_ref, st_scr):
    c = pl.program_id(1)

    @pl.when(c == 0)
    def _():
        st_scr[...] = s0_ref[0]

    heads = range(RET_HEADS)
    ks = [slice(hd * RET_DK, (hd + 1) * RET_DK) for hd in heads]
    vs = [slice(hd * RET_DV, (hd + 1) * RET_DV) for hd in heads]
    qs = [rq_ref[0, :, ks[hd]] for hd in heads]
    kk = [rk_ref[0, :, ks[hd]] for hd in heads]
    vv = [rv_ref[0, :, vs[hd]] for hd in heads]
    st = [st_scr[hd] for hd in heads]
    scores = [_dot_nt(qs[hd], kk[hd]) for hd in heads]
    cross = [_dot(qs[hd], st[hd].astype(BF16)) for hd in heads]
    for hd in heads:
        k_dec = (kk[hd].astype(F32) * kdec_ref[hd]).astype(BF16)
        st_scr[hd] = gt_ref[hd] * st[hd] + _dot_tn(k_dec, vv[hd])
    for hd in heads:
        sc = (scores[hd] * dec_ref[hd]).astype(BF16)
        o = _dot(sc, vv[hd]) + cross[hd] * cross_ref[hd]
        g = rg_ref[0, :, vs[hd]].astype(F32)
        o_ref[0, :, vs[hd]] = (_rms(o) * _silu(g)).astype(BF16)

    @pl.when(c == pl.num_programs(1) - 1)
    def _():
        sout_ref[0] = st_scr[...]


def _ret_tables(t):
    log_g = jnp.log1p(-jnp.exp2(-5.0 - jnp.arange(RET_HEADS, dtype=F32)))
    n = jnp.arange(t, dtype=F32)
    diff = n[:, None] - n[None, :]
    dec = jnp.where(diff >= 0.0, jnp.exp(jnp.maximum(diff, 0.0)[None] * log_g[:, None, None]), 0.0)
    cross = jnp.exp((n + 1.0)[None, :] * log_g[:, None])
    kdec = jnp.exp((t - 1.0 - n)[None, :] * log_g[:, None])
    gt = jnp.exp(t * log_g)
    cross = jnp.broadcast_to(cross[:, :, None], (RET_HEADS, t, RET_DV))
    kdec = jnp.broadcast_to(kdec[:, :, None], (RET_HEADS, t, RET_DK))
    gt = jnp.broadcast_to(gt[:, None, None], (RET_HEADS, RET_DK, RET_DV))
    return dec, cross, kdec, gt


def _retention(rq, rk, rv, rg, state0, tabs, t):
    nseq, length, _ = rq.shape
    tile = pl.BlockSpec((1, t, RET_W), lambda b, c: (b, c, 0))
    st = pl.BlockSpec((1, RET_HEADS, RET_DK, RET_DV), lambda b, c: (b, 0, 0, 0))
    const = lambda a: pl.BlockSpec(a.shape, lambda b, c: (0, 0, 0))
    return pl.pallas_call(
        _ret_kernel,
        grid=(nseq, length // t),
        in_specs=[tile, tile, tile, tile, st] + [const(a) for a in tabs],
        out_specs=[tile, st],
        out_shape=[jax.ShapeDtypeStruct((nseq, length, RET_W), BF16),
                   jax.ShapeDtypeStruct(state0.shape, F32)],
        scratch_shapes=[pltpu.VMEM((RET_HEADS, RET_DK, RET_DV), F32)],
        compiler_params=_params("parallel", "arbitrary"),
        name="retention",
    )(rq, rk, rv, rg, state0, *tabs)


def _out_kernel(x_ref, ol_ref, or_ref, gt_ref, gpost_ref, wuv_ref, wo_ref, o_ref):
    n_mla = MLA_HEADS * MLA_V
    o_mla = _dot(ol_ref[0], wuv_ref[...]).astype(BF16)
    mix = _dot(o_mla, wo_ref[:n_mla, :]) + _dot(or_ref[0], wo_ref[n_mla:, :])
    o_ref[0] = x_ref[0] + gt_ref[0] * (_rms(mix) * gpost_ref[...])


def _mixer_out(x, o_lat, o_ret, gate, g_post, w_uv, w_out, l, tm):
    nseq, length, d = x.shape
    tile = lambda n: pl.BlockSpec((1, tm, n), lambda b, s: (b, s, 0))
    wl = lambda r, c: _resident((None, r, c), lambda b, s: (l, 0, 0))
    return pl.pallas_call(
        _out_kernel,
        grid=(nseq, length // tm),
        in_specs=[tile(d), tile(o_lat.shape[-1]), tile(RET_W), _mod_spec(gate, tm),
                  pl.BlockSpec((1, d), lambda b, s: (0, 0)),
                  wl(*w_uv.shape[1:]), wl(*w_out.shape[1:])],
        out_specs=tile(d),
        out_shape=jax.ShapeDtypeStruct(x.shape, F32),
        compiler_params=_params("parallel", "parallel"),
        name="mixer_out",
    )(x, o_lat, o_ret, gate, g_post, w_uv, w_out)


def _prep_w_in(w_in):
    c0 = Q_RANK + KV_RANK
    half = MLA_ROPE // 2
    kpe = w_in[..., c0:c0 + MLA_ROPE]
    kpe_rot = jnp.concatenate([-kpe[..., half:], kpe[..., :half]], axis=-1)
    pad = jnp.zeros(w_in.shape[:-1] + (KPE_BLOCK - 2 * MLA_ROPE,), w_in.dtype)
    return jnp.concatenate([w_in[..., :c0], kpe, kpe_rot, pad, w_in[..., c0 + MLA_ROPE:]],
                           axis=-1).astype(BF16)


def _prep_w_uq(w_uq):
    depth = w_uq.shape[0]
    half = MLA_ROPE // 2
    w = w_uq.reshape(depth, Q_RANK, MLA_HEADS, MLA_NOPE + MLA_ROPE)
    nope = w[..., :MLA_NOPE].reshape(depth, Q_RANK, MLA_HEADS * MLA_NOPE)
    pe = w[..., MLA_NOPE:]
    rot = jnp.concatenate([-pe[..., half:], pe[..., :half]], axis=-1)
    flat = lambda a: a.reshape(depth, Q_RANK, MLA_HEADS * MLA_ROPE)
    return jnp.concatenate([nope, flat(pe), flat(rot)], axis=-1).astype(BF16)


def _prep_w_uk(w_uk):
    depth = w_uk.shape[0]
    w = w_uk.reshape(depth, KV_RANK, MLA_HEADS, MLA_NOPE)
    eye = jnp.eye(MLA_HEADS, dtype=w_uk.dtype)
    bd = jnp.einsum('lrhn,hg->lhngr', w, eye)
    return bd.reshape(depth, MLA_HEADS * MLA_NOPE, MLA_HEADS * KV_RANK).astype(BF16)


def _prep_w_uv(w_uv):
    depth = w_uv.shape[0]
    w = w_uv.reshape(depth, KV_RANK, MLA_HEADS, MLA_V)
    eye = jnp.eye(MLA_HEADS, dtype=w_uv.dtype)
    bd = jnp.einsum('lrhv,hg->lhrgv', w, eye)
    return bd.reshape(depth, MLA_HEADS * KV_RANK, MLA_HEADS * MLA_V).astype(BF16)


def _pick_tile(length, target):
    t = min(length, target)
    while length % t:
        t //= 2
    return t


def kernel(x_prompt, x_sample, cache_kv_latent, cache_k_rope, state_ret, c_prompt, c_sample,
           ada_w, ada_b, norm_pre, norm_post, ffn_w1, ffn_w3, ffn_w2,
           w_in, q_norm, kv_norm, w_uq, w_uk, w_uv, w_out):
    depth = ada_w.shape[0]
    nb_p, seq, d = x_prompt.shape
    nb_s, dec_seq, _ = x_sample.shape
    past = cache_kv_latent.shape[2]
    n_tok_s = nb_s * dec_seq

    w1, w3, w2 = ffn_w1.astype(BF16), ffn_w3.astype(BF16), ffn_w2.astype(BF16)
    w_in_b = _prep_w_in(w_in)
    w_uq_b = _prep_w_uq(w_uq)
    w_uk_b = _prep_w_uk(w_uk)
    w_uv_b = _prep_w_uv(w_uv)
    w_out_b = w_out.astype(BF16)

    mods = _ada_mods(jnp.concatenate([c_prompt, c_sample], axis=0), ada_w, ada_b)

    tm_p = _pick_tile(seq, 512)
    tq = _pick_tile(seq, 128)
    t_ret = _pick_tile(seq, 256)
    tabs_p = _rope_tables(jnp.arange(seq, dtype=jnp.int32))
    tabs_s = _rope_tables(jnp.tile(past + jnp.arange(dec_seq, dtype=jnp.int32), nb_s))
    ret_tabs_p = _ret_tables(t_ret)
    ret_tabs_s = _ret_tables(dec_seq)
    zero_state = jnp.zeros((nb_p, RET_HEADS, RET_DK, RET_DV), F32)

    def mod_p(l, r):
        return mods[l, :nb_p, r][:, None, :]

    def mod_s(l, r):
        return jnp.repeat(mods[l, nb_p:, r], dec_seq, axis=0)[None]

    y_p = x_prompt
    y_s = x_sample.reshape(1, n_tok_s, d)
    lat_p, kpe_p, ret_p, lat_s, kpe_s, ret_s = [], [], [], [], [], []

    for l in range(depth):
        gp = lambda j: norm_pre[l, j][None, :]
        go = lambda j: norm_post[l, j][None, :]
        qn, kvn = q_norm[l][None, :], kv_norm[l][None, :]

        y_p = _ffn(y_p, mod_p(l, 0), mod_p(l, 1), mod_p(l, 2), gp(0), go(0), w1, w3, w2, l, 0, tm_p)
        q, k, vt, kv, kpe, rq, rk, rv, rg = _proj(y_p, mod_p(l, 3), mod_p(l, 4), gp(1), w_in_b, qn, kvn,
                                                  w_uq_b, w_uk_b, tabs_p, l, tm_p)
        o_lat = _attention_prompt(q, k, vt, tq, tm_p)
        o_ret, r_fin = _retention(rq, rk, rv, rg, zero_state, ret_tabs_p, t_ret)
        y_p = _mixer_out(y_p, o_lat, o_ret, mod_p(l, 5), go(1), w_uv_b, w_out_b, l, tm_p)
        y_p = _ffn(y_p, mod_p(l, 6), mod_p(l, 7), mod_p(l, 8), gp(2), go(2), w1, w3, w2, l, 1, tm_p)
        lat_p.append(kv); kpe_p.append(kpe); ret_p.append(r_fin)

        y_s = _ffn(y_s, mod_s(l, 0), mod_s(l, 1), mod_s(l, 2), gp(0), go(0), w1, w3, w2, l, 0, n_tok_s)
        q, k, _, kv, kpe, rq, rk, rv, rg = _proj(y_s, mod_s(l, 3), mod_s(l, 4), gp(1), w_in_b, qn, kvn,
                                                 w_uq_b, w_uk_b, tabs_s, l, n_tok_s)
        per_seq = lambda a: a.reshape(nb_s, dec_seq, a.shape[-1])
        q = q.reshape(MLA_HEADS, nb_s, dec_seq, QK_DIM).swapaxes(0, 1)
        o_lat = _attention_sample(q, cache_kv_latent, cache_k_rope, per_seq(k), l)
        o_ret, r_new = _retention(per_seq(rq), per_seq(rk), per_seq(rv), per_seq(rg),
                                  state_ret[l], ret_tabs_s, dec_seq)
        y_s = _mixer_out(y_s, o_lat.reshape(1, n_tok_s, -1), o_ret.reshape(1, n_tok_s, -1),
                         mod_s(l, 5), go(1), w_uv_b, w_out_b, l, n_tok_s)
        y_s = _ffn(y_s, mod_s(l, 6), mod_s(l, 7), mod_s(l, 8), gp(2), go(2), w1, w3, w2, l, 1, n_tok_s)
        lat_s.append(per_seq(kv)); kpe_s.append(per_seq(kpe)); ret_s.append(r_new)

    return (y_p, y_s.reshape(nb_s, dec_seq, d),
            jnp.stack(lat_p), jnp.stack(kpe_p), jnp.stack(ret_p),
            jnp.stack(lat_s), jnp.stack(kpe_s), jnp.stack(ret_s))
```

```python
import functools

import jax
import jax.numpy as jnp
import numpy as np
from jax import lax
from jax.experimental import pallas as pl
from jax.experimental.pallas import tpu as pltpu

CHUNK = 64
N_SUB = 3
MLA_HEADS = 8
MLA_NOPE = 64
MLA_ROPE = 32
MLA_V = 64
Q_RANK = 256
KV_RANK = 128
RET_HEADS = 8
RET_DK = 64
RET_DV = 64
ROPE_BASE = 10000.0
EPS = 1e-6
MLA_SCALE = (MLA_NOPE + MLA_ROPE) ** -0.5
Q_SCALE = MLA_SCALE * float(np.log2(np.e))
MAX_GROUPS = 8
MXU_DIM = 256
FFN_CHUNK = 3 * MXU_DIM
FFN_SUB_ROWS = 512
RET_SCALE = RET_DK ** -0.5
NEG_INF = -1e30

QK_DIM = KV_RANK + MLA_ROPE
RET_W = RET_HEADS * RET_DK
LANES = 128
KPE_BLOCK = LANES
BF16_SUBLANES = 16
VT_ROWS = KV_RANK + BF16_SUBLANES
W_IN_COLS = Q_RANK + KV_RANK + KPE_BLOCK + 4 * RET_W

F32 = jnp.float32
BF16 = jnp.bfloat16

VMEM_LIMIT = 56 * 1024 * 1024


def _params(*sem):
    return pltpu.CompilerParams(dimension_semantics=sem, vmem_limit_bytes=VMEM_LIMIT)


def _resident(block_shape, index_map):
    return pl.BlockSpec(block_shape, index_map, pipeline_mode=pl.Buffered(1))


def _rms(x):
    return x * lax.rsqrt(jnp.mean(x * x, axis=-1, keepdims=True) + EPS)


def _silu(x):
    return x * jax.nn.sigmoid(x)


def _dot(a, b):
    return jnp.dot(a, b, preferred_element_type=F32)


def _dot_nt(a, b):
    return lax.dot_general(a, b, (((1,), (1,)), ((), ())), preferred_element_type=F32)


def _dot_tn(a, b):
    return lax.dot_general(a, b, (((0,), (0,)), ((), ())), preferred_element_type=F32)


def _ada_kernel(c_ref, w_ref, b_ref, o_ref):
    sc = _silu(c_ref[...])
    o_ref[0] = jnp.dot(sc, w_ref[0], preferred_element_type=F32,
                       precision=lax.Precision.HIGHEST) + b_ref[0]


def _ada_mods(c_all, ada_w, ada_b):
    depth, d, cols = ada_w.shape
    nb = c_all.shape[0]
    n_col = cols // d
    out = pl.pallas_call(
        _ada_kernel,
        grid=(depth, n_col),
        in_specs=[
            pl.BlockSpec((nb, d), lambda l, j: (0, 0)),
            pl.BlockSpec((1, d, d), lambda l, j: (l, 0, j)),
            pl.BlockSpec((1, 1, d), lambda l, j: (l, 0, j)),
        ],
        out_specs=pl.BlockSpec((1, nb, d), lambda l, j: (l, 0, j)),
        out_shape=jax.ShapeDtypeStruct((depth, nb, cols), F32),
        compiler_params=_params("parallel", "parallel"),
        name="ada_mods",
    )(c_all, ada_w, ada_b.reshape(depth, 1, cols))
    return out.reshape(depth, nb, n_col, d)


def _ffn_kernel(x_ref, sh_ref, sc_ref, gt_ref, gpre_ref, gpost_ref, w1_ref, w3_ref, w2_ref, o_ref,
                *, coef, f_chunk, sub):
    d_ff = w1_ref.shape[1]
    starts = range(0, x_ref.shape[1], sub)
    rows = lambda ref, r0: ref[0] if ref.shape[1] == 1 else ref[0, r0:r0 + sub]
    hs = [(_rms(x_ref[0, r0:r0 + sub]) * gpre_ref[...] * (1.0 + rows(sc_ref, r0))
           + rows(sh_ref, r0)).astype(BF16) for r0 in starts]
    for h, r0 in zip(hs, starts):
        o, pending = None, None
        for c0 in range(0, d_ff, f_chunk):
            c1 = min(c0 + f_chunk, d_ff)
            a = _dot(h, w1_ref[:, c0:c1])
            b = _dot(h, w3_ref[:, c0:c1])
            if pending is not None:
                part = _dot(pending[0], w2_ref[pending[1]:pending[2], :])
                o = part if o is None else o + part
            pending = ((_silu(a) * b).astype(BF16), c0, c1)
        part = _dot(pending[0], w2_ref[pending[1]:pending[2], :])
        o = part if o is None else o + part
        o_ref[0, r0:r0 + sub] = (x_ref[0, r0:r0 + sub]
                                 + (coef * rows(gt_ref, r0)) * (_rms(o) * gpost_ref[...]))


def _mod_spec(mod, tm):
    if mod.shape[1] == 1:
        return pl.BlockSpec((1, 1, mod.shape[2]), lambda b, s: (b, 0, 0))
    return pl.BlockSpec((1, tm, mod.shape[2]), lambda b, s: (b, s, 0))


def _ffn(x, shift, scale, gate, g_pre, g_post, w1, w3, w2, l, k, tm):
    nseq, length, d = x.shape
    d_ff = w1.shape[-1]
    f_chunk = min(d_ff, FFN_CHUNK)
    w_up = _resident((None, None, d, d_ff), lambda b, s: (l, k, 0, 0))
    w_dn = _resident((None, None, d_ff, d), lambda b, s: (l, k, 0, 0))
    vec = pl.BlockSpec((1, d), lambda b, s: (0, 0))
    tile = pl.BlockSpec((1, tm, d), lambda b, s: (b, s, 0))
    return pl.pallas_call(
        functools.partial(_ffn_kernel, coef=0.5, f_chunk=f_chunk, sub=min(tm, FFN_SUB_ROWS)),
        grid=(nseq, length // tm),
        in_specs=[tile, _mod_spec(shift, tm), _mod_spec(scale, tm), _mod_spec(gate, tm),
                  vec, vec, w_up, w_up, w_dn],
        out_specs=tile,
        out_shape=jax.ShapeDtypeStruct(x.shape, F32),
        compiler_params=_params("parallel", "parallel"),
        name="ffn",
    )(x, shift, scale, gate, g_pre, g_post, w1, w3, w2)


def _proj_kernel(x_ref, sh_ref, sc_ref, gpre_ref, win_ref, qn_ref, kvn_ref, wuq_ref, wuk_ref,
                 cq_ref, sq_ref, ck_ref, sk_ref, rc_ref, rsa_ref, rsb_ref,
                 q_out, k_out, vt_out, kv_out, kpe_out, rq_out, rk_out, rv_out, rg_out):
    x = x_ref[0]
    h = (_rms(x) * gpre_ref[...] * (1.0 + sc_ref[0]) + sh_ref[0]).astype(BF16)
    z = _dot(h, win_ref[...])

    q_lat = (_rms(z[:, :Q_RANK]) * qn_ref[...]).astype(BF16)
    kv_lat = _rms(z[:, Q_RANK:Q_RANK + KV_RANK]) * kvn_ref[...]
    c0 = Q_RANK + KV_RANK
    k_pe = (z[:, c0:c0 + MLA_ROPE] * ck_ref[...]
            + z[:, c0 + MLA_ROPE:c0 + 2 * MLA_ROPE] * sk_ref[...])
    kv_out[0] = kv_lat
    kpe_out[0] = k_pe
    k_out[0] = jnp.concatenate([kv_lat, k_pe], axis=-1).astype(BF16)
    ones = jnp.ones((VT_ROWS - KV_RANK, kv_lat.shape[0]), F32)
    vt_out[0, 0] = jnp.concatenate([kv_lat.T, ones], axis=0).astype(BF16)

    n_nope = MLA_HEADS * MLA_NOPE
    n_pe = MLA_HEADS * MLA_ROPE
    qq = _dot(q_lat, wuq_ref[...])
    q_pe = qq[:, n_nope:n_nope + n_pe] * cq_ref[...] + qq[:, n_nope + n_pe:] * sq_ref[...]
    q_abs = _dot(qq[:, :n_nope].astype(BF16), wuk_ref[...])
    for hd in range(MLA_HEADS):
        qh = jnp.concatenate([q_abs[:, hd * KV_RANK:(hd + 1) * KV_RANK],
                              q_pe[:, hd * MLA_ROPE:(hd + 1) * MLA_ROPE]], axis=-1)
        q_out[0, hd] = (qh * Q_SCALE).astype(BF16)

    r0 = c0 + KPE_BLOCK
    rc, rsa, rsb = rc_ref[...], rsa_ref[...], rsb_ref[...]
    for off, out, scale in ((r0, rq_out, 1.0), (r0 + RET_W, rk_out, RET_SCALE)):
        for c in range(RET_W // LANES):
            xc = z[:, off + c * LANES:off + (c + 1) * LANES]
            y = (xc * rc + pltpu.roll(xc, LANES - RET_DK // 2, 1) * rsa
                 + pltpu.roll(xc, RET_DK // 2, 1) * rsb)
            out[0, :, c * LANES:(c + 1) * LANES] = (y * scale).astype(BF16)
    rv_out[0] = z[:, r0 + 2 * RET_W:r0 + 3 * RET_W].astype(BF16)
    rg_out[0] = z[:, r0 + 3 * RET_W:r0 + 4 * RET_W].astype(BF16)


def _proj(x, shift, scale, g_pre, w_in, q_norm, kv_norm, w_uq, w_uk, tabs, l, tm):
    nseq, length, d = x.shape
    grid = (nseq, length // tm)
    vec = lambda n: pl.BlockSpec((1, n), lambda b, s: (0, 0))
    tile = lambda n: pl.BlockSpec((1, tm, n), lambda b, s: (b, s, 0))
    tab = lambda n: pl.BlockSpec((tm, n), lambda b, s: (s, 0))
    wl = lambda r, c: _resident((None, r, c), lambda b, s: (l, 0, 0))
    n_q = MLA_HEADS * KV_RANK
    bf = lambda n: jax.ShapeDtypeStruct((nseq, length, n), BF16)
    return pl.pallas_call(
        _proj_kernel,
        grid=grid,
        in_specs=[tile(d), _mod_spec(shift, tm), _mod_spec(scale, tm), vec(d),
                  wl(d, W_IN_COLS), vec(Q_RANK), vec(KV_RANK),
                  wl(Q_RANK, w_uq.shape[-1]), wl(MLA_HEADS * MLA_NOPE, n_q),
                  tab(MLA_HEADS * MLA_ROPE), tab(MLA_HEADS * MLA_ROPE), tab(MLA_ROPE), tab(MLA_ROPE),
                  tab(LANES), tab(LANES), tab(LANES)],
        out_specs=[pl.BlockSpec((1, MLA_HEADS, tm, QK_DIM), lambda b, s: (b, 0, s, 0)),
                   tile(QK_DIM),
                   pl.BlockSpec((1, 1, VT_ROWS, tm), lambda b, s: (b, s, 0, 0)),
                   tile(KV_RANK), tile(MLA_ROPE),
                   tile(RET_W), tile(RET_W), tile(RET_W), tile(RET_W)],
        out_shape=[jax.ShapeDtypeStruct((nseq, MLA_HEADS, length, QK_DIM), BF16),
                   bf(QK_DIM),
                   jax.ShapeDtypeStruct((nseq, length // tm, VT_ROWS, tm), BF16),
                   jax.ShapeDtypeStruct((nseq, length, KV_RANK), F32),
                   jax.ShapeDtypeStruct((nseq, length, MLA_ROPE), F32),
                   bf(RET_W), bf(RET_W), bf(RET_W), bf(RET_W)],
        compiler_params=_params("parallel", "parallel"),
        name="mixer_proj",
    )(x, shift, scale, g_pre, w_in, q_norm, kv_norm, w_uq, w_uk, *tabs)


def _rope_tables(pos):
    pos = pos.astype(F32)[:, None]

    def cs(half):
        inv = ROPE_BASE ** (-jnp.arange(half, dtype=F32) / half)
        ang = pos * inv[None, :]
        return jnp.cos(ang), jnp.sin(ang)

    c16, s16 = cs(MLA_ROPE // 2)
    ck = jnp.concatenate([c16, c16], axis=-1)
    sk = jnp.concatenate([s16, s16], axis=-1)
    cq = jnp.tile(ck, (1, MLA_HEADS))
    sq = jnp.tile(sk, (1, MLA_HEADS))
    c32, s32 = cs(RET_DK // 2)
    zero = jnp.zeros_like(s32)
    reps = LANES // RET_DK
    rc = jnp.tile(jnp.concatenate([c32, c32], axis=-1), (1, reps))
    rsa = jnp.tile(jnp.concatenate([-s32, zero], axis=-1), (1, reps))
    rsb = jnp.tile(jnp.concatenate([zero, s32], axis=-1), (1, reps))
    return cq, sq, ck, sk, rc, rsa, rsb


def _attn_kernel(q_ref, k_ref, vt_ref, o_ref, m_scr, alpha_scr, p_scr, acc_scr, *, tq, tk, cb):
    i = pl.program_id(1)
    rows = MLA_HEADS * tq
    q = q_ref[0].reshape(rows, QK_DIM)
    m_scr[...] = jnp.full(m_scr.shape, NEG_INF, F32)
    acc_scr[...] = jnp.zeros(acc_scr.shape, F32)
    n_full = (i * tq) // tk

    def scores(j):
        start = pl.multiple_of(j * tk, tk)
        return _dot_nt(k_ref[0, pl.ds(start, tk), :], q)

    def values(j):
        for c0 in range(0, rows, cb):
            cs = slice(c0, c0 + cb)
            acc_scr[:, cs] = acc_scr[:, cs] * alpha_scr[:, cs] + _dot(vt_ref[0, j], p_scr[:, cs])

    def softmax(j, s, masked):
        if masked:
            kpos = j * tk + lax.broadcasted_iota(jnp.int32, (tk, tq), 0)
            qpos = i * tq + lax.broadcasted_iota(jnp.int32, (tk, tq), 1)
            allowed = (kpos // CHUNK) <= (qpos // CHUNK)
            allowed = jnp.concatenate([allowed] * (cb // tq), axis=1)
        for c0 in range(0, rows, cb):
            cs = slice(c0, c0 + cb)
            sc = s[:, cs]
            if masked:
                sc = jnp.where(allowed, sc, NEG_INF)
            m_prev = m_scr[:, cs]
            m_part = jnp.max(sc.reshape(MAX_GROUPS, tk // MAX_GROUPS, cb), axis=0)
            m_new = jnp.maximum(m_prev, jnp.max(m_part, axis=0, keepdims=True))
            alpha_scr[:, cs] = jnp.exp2(m_prev - m_new)
            p_scr[:, cs] = jnp.exp2((sc - m_new).astype(BF16))
            m_scr[:, cs] = m_new

    def pipelined(j, masked):
        s = scores(j)
        values(j - 1)
        softmax(j, s, masked)

    @pl.when(n_full > 0)
    def _():
        softmax(0, scores(0), False)

        def body(j, carry):
            pipelined(j, False)
            return carry

        lax.fori_loop(1, n_full, body, 0)
        pipelined(n_full, True)

    @pl.when(n_full == 0)
    def _():
        softmax(0, scores(0), True)

    values(n_full)

    acc = acc_scr[...]
    o_t = acc[:KV_RANK] / acc[KV_RANK:KV_RANK + 1]
    for hd in range(MLA_HEADS):
        o_ref[0, :, hd * KV_RANK:(hd + 1) * KV_RANK] = o_t[:, hd * tq:(hd + 1) * tq].T.astype(BF16)


def _attention_prompt(q, k, vt, tq, tk):
    nseq, _, length, _ = q.shape
    rows = MLA_HEADS * tq
    assert vt.shape[-1] == tk and tk % tq == 0
    cb = max(tq, MXU_DIM)
    return pl.pallas_call(
        functools.partial(_attn_kernel, tq=tq, tk=tk, cb=cb),
        grid=(nseq, length // tq),
        in_specs=[pl.BlockSpec((1, MLA_HEADS, tq, QK_DIM), lambda b, i: (b, 0, i, 0)),
                  pl.BlockSpec((1, length, QK_DIM), lambda b, i: (b, 0, 0)),
                  pl.BlockSpec((1, length // tk, VT_ROWS, tk), lambda b, i: (b, 0, 0, 0))],
        out_specs=pl.BlockSpec((1, tq, MLA_HEADS * KV_RANK), lambda b, i: (b, i, 0)),
        out_shape=jax.ShapeDtypeStruct((nseq, length, MLA_HEADS * KV_RANK), BF16),
        scratch_shapes=[pltpu.VMEM((1, rows), F32), pltpu.VMEM((1, rows), F32),
                        pltpu.VMEM((tk, rows), BF16), pltpu.VMEM((VT_ROWS, rows), F32)],
        compiler_params=_params("parallel", "arbitrary"),
        name="mla_attention_prompt",
    )(q, k, vt)


def _chunk_mask(s, t, q_start, k_start):
    n = s.shape[1]
    qpos = q_start + lax.broadcasted_iota(jnp.int32, (t, n), 0)
    kpos = k_start + lax.broadcasted_iota(jnp.int32, (t, n), 1)
    allowed = (kpos // CHUNK) <= (qpos // CHUNK)
    return jnp.where(allowed[None], s.reshape(MLA_HEADS, t, n), NEG_INF).reshape(s.shape)


def _attn_sample_kernel(q_ref, ckv_ref, ckpe_ref, kn_ref, o_ref, *, t, past, mask_cache, mask_new):
    rows = MLA_HEADS * t
    q = q_ref[0].reshape(rows, QK_DIM)
    ckv = ckv_ref[0].astype(BF16)
    ckpe = ckpe_ref[0].astype(BF16)
    kn = kn_ref[0]
    s_c = _dot_nt(q[:, :KV_RANK], ckv) + _dot_nt(q[:, KV_RANK:], ckpe)
    s_n = _dot_nt(q, kn)
    if mask_cache:
        s_c = _chunk_mask(s_c, t, past, 0)
    if mask_new:
        s_n = _chunk_mask(s_n, t, past, past)
    m = jnp.maximum(jnp.max(s_c, axis=1, keepdims=True), jnp.max(s_n, axis=1, keepdims=True))
    p_c = jnp.exp2(s_c - m)
    p_n = jnp.exp2(s_n - m)
    denom = jnp.sum(p_c, axis=1, keepdims=True) + jnp.sum(p_n, axis=1, keepdims=True)
    o = _dot(p_c.astype(BF16), ckv) + _dot(p_n.astype(BF16), kn[:, :KV_RANK])
    o = (o / denom).astype(BF16)
    for hd in range(MLA_HEADS):
        o_ref[0, :, hd * KV_RANK:(hd + 1) * KV_RANK] = o[hd * t:(hd + 1) * t]


def _attention_sample(q, cache_kv, cache_kpe, k_new, l):
    nb, _, t, _ = q.shape
    past = cache_kv.shape[2]
    q_chunk = (past + np.arange(t)) // CHUNK
    k_chunk = np.arange(past + t) // CHUNK
    allowed = k_chunk[None, :] <= q_chunk[:, None]
    mask_cache = not allowed[:, :past].all()
    mask_new = not allowed[:, past:].all()
    return pl.pallas_call(
        functools.partial(_attn_sample_kernel, t=t, past=past, mask_cache=mask_cache, mask_new=mask_new),
        grid=(nb,),
        in_specs=[pl.BlockSpec((1, MLA_HEADS, t, QK_DIM), lambda b: (b, 0, 0, 0)),
                  pl.BlockSpec((None, 1, past, KV_RANK), lambda b: (l, b, 0, 0)),
                  pl.BlockSpec((None, 1, past, MLA_ROPE), lambda b: (l, b, 0, 0)),
                  pl.BlockSpec((1, t, QK_DIM), lambda b: (b, 0, 0))],
        out_specs=pl.BlockSpec((1, t, MLA_HEADS * KV_RANK), lambda b: (b, 0, 0)),
        out_shape=jax.ShapeDtypeStruct((nb, t, MLA_HEADS * KV_RANK), BF16),
        compiler_params=_params("parallel"),
        name="mla_attention_sample",
    )(q, cache_kv, cache_kpe, k_new)


def _ret_kernel(rq_ref, rk_ref, rv_ref, s0_ref, dec_ref, cross_ref, kdec_ref, gt_ref,
                o_ref, sout_ref, st_scr):
    c = pl.program_id(1)

    @pl.when(c == 0)
    def _():
        st_scr[...] = s0_ref[0]

    heads = range(RET_HEADS)
    ks = [slice(hd * RET_DK, (hd + 1) * RET_DK) for hd in heads]
    vs = [slice(hd * RET_DV, (hd + 1) * RET_DV) for hd in heads]
    qs = [rq_ref[0, :, ks[hd]] for hd in heads]
    kk = [rk_ref[0, :, ks[hd]] for hd in heads]
    vv = [rv_ref[0, :, vs[hd]] for hd in heads]
    st = [st_scr[hd] for hd in heads]
    scores = [_dot_nt(qs[hd], kk[hd]) for hd in heads]
    cross = [_dot(qs[hd], st[hd].astype(BF16)) for hd in heads]
    for hd in heads:
        k_dec = (kk[hd].astype(F32) * kdec_ref[hd]).astype(BF16)
        st_scr[hd] = gt_ref[hd] * st[hd] + _dot_tn(k_dec, vv[hd])
    for hd in heads:
        sc = (scores[hd] * dec_ref[hd]).astype(BF16)
        o_ref[0, :, vs[hd]] = (_dot(sc, vv[hd]) + cross[hd] * cross_ref[hd]).astype(BF16)

    @pl.when(c == pl.num_programs(1) - 1)
    def _():
        sout_ref[0] = st_scr[...]


def _ret_tables(t):
    log_g = jnp.log1p(-jnp.exp2(-5.0 - jnp.arange(RET_HEADS, dtype=F32)))
    n = jnp.arange(t, dtype=F32)
    diff = n[:, None] - n[None, :]
    dec = jnp.where(diff >= 0.0, jnp.exp(jnp.maximum(diff, 0.0)[None] * log_g[:, None, None]), 0.0)
    cross = jnp.exp((n + 1.0)[None, :] * log_g[:, None])
    kdec = jnp.exp((t - 1.0 - n)[None, :] * log_g[:, None])
    gt = jnp.exp(t * log_g)
    cross = jnp.broadcast_to(cross[:, :, None], (RET_HEADS, t, RET_DV))
    kdec = jnp.broadcast_to(kdec[:, :, None], (RET_HEADS, t, RET_DK))
    gt = jnp.broadcast_to(gt[:, None, None], (RET_HEADS, RET_DK, RET_DV))
    return dec, cross, kdec, gt


def _retention(rq, rk, rv, state0, tabs, t):
    nseq, length, _ = rq.shape
    tile = pl.BlockSpec((1, t, RET_W), lambda b, c: (b, c, 0))
    st = pl.BlockSpec((1, RET_HEADS, RET_DK, RET_DV), lambda b, c: (b, 0, 0, 0))
    const = lambda a: pl.BlockSpec(a.shape, lambda b, c: (0, 0, 0))
    return pl.pallas_call(
        _ret_kernel,
        grid=(nseq, length // t),
        in_specs=[tile, tile, tile, st] + [const(a) for a in tabs],
        out_specs=[tile, st],
        out_shape=[jax.ShapeDtypeStruct((nseq, length, RET_W), BF16),
                   jax.ShapeDtypeStruct(state0.shape, F32)],
        scratch_shapes=[pltpu.VMEM((RET_HEADS, RET_DK, RET_DV), F32)],
        compiler_params=_params("parallel", "arbitrary"),
        name="retention",
    )(rq, rk, rv, state0, *tabs)


def _out_kernel(x_ref, ol_ref, or_ref, rg_ref, gt_ref, gpost_ref, wuv_ref, wo_ref, grp_ref, o_ref):
    n_mla = MLA_HEADS * MLA_V
    o_mla = _dot(ol_ref[0], wuv_ref[...]).astype(BF16)
    mix = _dot(o_mla, wo_ref[:n_mla, :])
    o = or_ref[0].astype(F32)
    sq = o * o
    hi = sq.astype(BF16)
    lo = (sq - hi.astype(F32)).astype(BF16)
    ms = _dot(hi, grp_ref[...]) + _dot(lo, grp_ref[...])
    o_ret = (o * lax.rsqrt(ms + EPS) * _silu(rg_ref[0].astype(F32))).astype(BF16)
    mix = mix + _dot(o_ret, wo_ref[n_mla:, :])
    o_ref[0] = x_ref[0] + gt_ref[0] * (_rms(mix) * gpost_ref[...])


def _mixer_out(x, o_lat, o_ret, rg, gate, g_post, w_uv, w_out, l, tm):
    nseq, length, d = x.shape
    tile = lambda n: pl.BlockSpec((1, tm, n), lambda b, s: (b, s, 0))
    wl = lambda r, c: _resident((None, r, c), lambda b, s: (l, 0, 0))
    head = jnp.arange(RET_W) // RET_DV
    group_mean = ((head[:, None] == head[None, :]).astype(F32) / RET_DV).astype(BF16)
    return pl.pallas_call(
        _out_kernel,
        grid=(nseq, length // tm),
        in_specs=[tile(d), tile(o_lat.shape[-1]), tile(RET_W), tile(RET_W), _mod_spec(gate, tm),
                  pl.BlockSpec((1, d), lambda b, s: (0, 0)),
                  wl(*w_uv.shape[1:]), wl(*w_out.shape[1:]),
                  _resident((RET_W, RET_W), lambda b, s: (0, 0))],
        out_specs=tile(d),
        out_shape=jax.ShapeDtypeStruct(x.shape, F32),
        compiler_params=_params("parallel", "parallel"),
        name="mixer_out",
    )(x, o_lat, o_ret, rg, gate, g_post, w_uv, w_out, group_mean)


def _prep_w_in(w_in):
    c0 = Q_RANK + KV_RANK
    half = MLA_ROPE // 2
    kpe = w_in[..., c0:c0 + MLA_ROPE]
    kpe_rot = jnp.concatenate([-kpe[..., half:], kpe[..., :half]], axis=-1)
    pad = jnp.zeros(w_in.shape[:-1] + (KPE_BLOCK - 2 * MLA_ROPE,), w_in.dtype)
    return jnp.concatenate([w_in[..., :c0], kpe, kpe_rot, pad, w_in[..., c0 + MLA_ROPE:]],
                           axis=-1).astype(BF16)


def _prep_w_uq(w_uq):
    depth = w_uq.shape[0]
    half = MLA_ROPE // 2
    w = w_uq.reshape(depth, Q_RANK, MLA_HEADS, MLA_NOPE + MLA_ROPE)
    nope = w[..., :MLA_NOPE].reshape(depth, Q_RANK, MLA_HEADS * MLA_NOPE)
    pe = w[..., MLA_NOPE:]
    rot = jnp.concatenate([-pe[..., half:], pe[..., :half]], axis=-1)
    flat = lambda a: a.reshape(depth, Q_RANK, MLA_HEADS * MLA_ROPE)
    return jnp.concatenate([nope, flat(pe), flat(rot)], axis=-1).astype(BF16)


def _prep_w_uk(w_uk):
    depth = w_uk.shape[0]
    w = w_uk.reshape(depth, KV_RANK, MLA_HEADS, MLA_NOPE)
    eye = jnp.eye(MLA_HEADS, dtype=w_uk.dtype)
    bd = jnp.einsum('lrhn,hg->lhngr', w, eye)
    return bd.reshape(depth, MLA_HEADS * MLA_NOPE, MLA_HEADS * KV_RANK).astype(BF16)


def _prep_w_uv(w_uv):
    depth = w_uv.shape[0]
    w = w_uv.reshape(depth, KV_RANK, MLA_HEADS, MLA_V)
    eye = jnp.eye(MLA_HEADS, dtype=w_uv.dtype)
    bd = jnp.einsum('lrhv,hg->lhrgv', w, eye)
    return bd.reshape(depth, MLA_HEADS * KV_RANK, MLA_HEADS * MLA_V).astype(BF16)


def _pick_tile(length, target):
    t = min(length, target)
    while length % t:
        t //= 2
    return t


def kernel(x_prompt, x_sample, cache_kv_latent, cache_k_rope, state_ret, c_prompt, c_sample,
           ada_w, ada_b, norm_pre, norm_post, ffn_w1, ffn_w3, ffn_w2,
           w_in, q_norm, kv_norm, w_uq, w_uk, w_uv, w_out):
    depth = ada_w.shape[0]
    nb_p, seq, d = x_prompt.shape
    nb_s, dec_seq, _ = x_sample.shape
    past = cache_kv_latent.shape[2]
    n_tok_s = nb_s * dec_seq

    w1, w3, w2 = ffn_w1.astype(BF16), ffn_w3.astype(BF16), ffn_w2.astype(BF16)
    w_in_b = _prep_w_in(w_in)
    w_uq_b = _prep_w_uq(w_uq)
    w_uk_b = _prep_w_uk(w_uk)
    w_uv_b = _prep_w_uv(w_uv)
    w_out_b = w_out.astype(BF16)

    mods = _ada_mods(jnp.concatenate([c_prompt, c_sample], axis=0), ada_w, ada_b)

    tm_p = _pick_tile(seq, 512)
    tm_ffn = _pick_tile(seq, 2 * FFN_SUB_ROWS)
    tq = _pick_tile(seq, 512)
    t_ret = _pick_tile(seq, 256)
    tabs_p = _rope_tables(jnp.arange(seq, dtype=jnp.int32))
    tabs_s = _rope_tables(jnp.tile(past + jnp.arange(dec_seq, dtype=jnp.int32), nb_s))
    ret_tabs_p = _ret_tables(t_ret)
    ret_tabs_s = _ret_tables(dec_seq)
    zero_state = jnp.zeros((nb_p, RET_HEADS, RET_DK, RET_DV), F32)

    def mod_p(l, r):
        return mods[l, :nb_p, r][:, None, :]

    def mod_s(l, r):
        return jnp.repeat(mods[l, nb_p:, r], dec_seq, axis=0)[None]

    y_p = x_prompt
    y_s = x_sample.reshape(1, n_tok_s, d)
    lat_p, kpe_p, ret_p, lat_s, kpe_s, ret_s = [], [], [], [], [], []

    for l in range(depth):
        gp = lambda j: norm_pre[l, j][None, :]
        go = lambda j: norm_post[l, j][None, :]
        qn, kvn = q_norm[l][None, :], kv_norm[l][None, :]

        y_p = _ffn(y_p, mod_p(l, 0), mod_p(l, 1), mod_p(l, 2), gp(0), go(0), w1, w3, w2, l, 0, tm_ffn)
        q, k, vt, kv, kpe, rq, rk, rv, rg = _proj(y_p, mod_p(l, 3), mod_p(l, 4), gp(1), w_in_b, qn, kvn,
                                                  w_uq_b, w_uk_b, tabs_p, l, tm_p)
        o_lat = _attention_prompt(q, k, vt, tq, tm_p)
        o_ret, r_fin = _retention(rq, rk, rv, zero_state, ret_tabs_p, t_ret)
        y_p = _mixer_out(y_p, o_lat, o_ret, rg, mod_p(l, 5), go(1), w_uv_b, w_out_b, l, tm_p)
        y_p = _ffn(y_p, mod_p(l, 6), mod_p(l, 7), mod_p(l, 8), gp(2), go(2), w1, w3, w2, l, 1, tm_ffn)
        lat_p.append(kv); kpe_p.append(kpe); ret_p.append(r_fin)

        y_s = _ffn(y_s, mod_s(l, 0), mod_s(l, 1), mod_s(l, 2), gp(0), go(0), w1, w3, w2, l, 0, n_tok_s)
        q, k, _, kv, kpe, rq, rk, rv, rg = _proj(y_s, mod_s(l, 3), mod_s(l, 4), gp(1), w_in_b, qn, kvn,
                                                 w_uq_b, w_uk_b, tabs_s, l, n_tok_s)
        per_seq = lambda a: a.reshape(nb_s, dec_seq, a.shape[-1])
        q = q.reshape(MLA_HEADS, nb_s, dec_seq, QK_DIM).swapaxes(0, 1)
        o_lat = _attention_sample(q, cache_kv_latent, cache_k_rope, per_seq(k), l)
        o_ret, r_new = _retention(per_seq(rq), per_seq(rk), per_seq(rv),
                                  state_ret[l], ret_tabs_s, dec_seq)
        y_s = _mixer_out(y_s, o_lat.reshape(1, n_tok_s, -1), o_ret.reshape(1, n_tok_s, -1), rg,
                         mod_s(l, 5), go(1), w_uv_b, w_out_b, l, n_tok_s)
        y_s = _ffn(y_s, mod_s(l, 6), mod_s(l, 7), mod_s(l, 8), gp(2), go(2), w1, w3, w2, l, 1, n_tok_s)
        lat_s.append(per_seq(kv)); kpe_s.append(per_seq(kpe)); ret_s.append(r_new)

    return (y_p, y_s.reshape(nb_s, dec_seq, d),
            jnp.stack(lat_p), jnp.stack(kpe_p), jnp.stack(ret_p),
            jnp.stack(lat_s), jnp.stack(kpe_s), jnp.stack(ret_s))
```

```python
import functools

import jax
import jax.numpy as jnp
import numpy as np
from jax import lax
from jax.experimental import pallas as pl
from jax.experimental.pallas import tpu as pltpu

CHUNK = 64
N_SUB = 3
MLA_HEADS = 8
MLA_NOPE = 64
MLA_ROPE = 32
MLA_V = 64
Q_RANK = 256
KV_RANK = 128
RET_HEADS = 8
RET_DK = 64
RET_DV = 64
ROPE_BASE = 10000.0
EPS = 1e-6
MLA_SCALE = (MLA_NOPE + MLA_ROPE) ** -0.5
Q_SCALE = MLA_SCALE * float(np.log2(np.e))
MAX_GROUPS = 8
MXU_DIM = 256
FFN_CHUNK = 3 * MXU_DIM
FFN_SUB_ROWS = 512
RET_SCALE = RET_DK ** -0.5
NEG_INF = -1e30

QK_DIM = KV_RANK + MLA_ROPE
RET_W = RET_HEADS * RET_DK
LANES = 128
KPE_BLOCK = LANES
BF16_SUBLANES = 16
VT_ROWS = KV_RANK + BF16_SUBLANES
W_IN_COLS = Q_RANK + KV_RANK + KPE_BLOCK + 4 * RET_W

F32 = jnp.float32
BF16 = jnp.bfloat16

VMEM_LIMIT = 56 * 1024 * 1024


def _params(*sem):
    return pltpu.CompilerParams(dimension_semantics=sem, vmem_limit_bytes=VMEM_LIMIT)


def _resident(block_shape, index_map):
    return pl.BlockSpec(block_shape, index_map, pipeline_mode=pl.Buffered(1))


def _rms(x):
    return x * lax.rsqrt(jnp.mean(x * x, axis=-1, keepdims=True) + EPS)


def _silu(x):
    return x * jax.nn.sigmoid(x)


def _dot(a, b):
    return jnp.dot(a, b, preferred_element_type=F32)


def _dot_nt(a, b):
    return lax.dot_general(a, b, (((1,), (1,)), ((), ())), preferred_element_type=F32)


def _dot_tn(a, b):
    return lax.dot_general(a, b, (((0,), (0,)), ((), ())), preferred_element_type=F32)


def _ada_kernel(c_ref, w_ref, b_ref, o_ref):
    sc = _silu(c_ref[...])
    o_ref[0] = jnp.dot(sc, w_ref[0], preferred_element_type=F32,
                       precision=lax.Precision.HIGHEST) + b_ref[0]


def _ada_mods(c_all, ada_w, ada_b):
    depth, d, cols = ada_w.shape
    nb = c_all.shape[0]
    n_col = cols // d
    out = pl.pallas_call(
        _ada_kernel,
        grid=(depth, n_col),
        in_specs=[
            pl.BlockSpec((nb, d), lambda l, j: (0, 0)),
            pl.BlockSpec((1, d, d), lambda l, j: (l, 0, j)),
            pl.BlockSpec((1, 1, d), lambda l, j: (l, 0, j)),
        ],
        out_specs=pl.BlockSpec((1, nb, d), lambda l, j: (l, 0, j)),
        out_shape=jax.ShapeDtypeStruct((depth, nb, cols), F32),
        compiler_params=_params("parallel", "parallel"),
        name="ada_mods",
    )(c_all, ada_w, ada_b.reshape(depth, 1, cols))
    return out.reshape(depth, nb, n_col, d)


def _ffn_kernel(x_ref, sh_ref, sc_ref, gt_ref, gpre_ref, gpost_ref, w1_ref, w3_ref, w2_ref, o_ref,
                *, coef, f_chunk, sub):
    d_ff = w1_ref.shape[1]
    starts = list(range(0, x_ref.shape[1], sub))
    rows = lambda ref, r0: ref[0] if ref.shape[1] == 1 else ref[0, r0:r0 + sub]
    hs = [(_rms(x_ref[0, r0:r0 + sub]) * gpre_ref[...] * (1.0 + rows(sc_ref, r0))
           + rows(sh_ref, r0)).astype(BF16) for r0 in starts]
    outs = [None] * len(starts)
    pending = [None] * len(starts)

    def down(i, c0, c1):
        part = _dot(pending[i], w2_ref[c0:c1, :])
        outs[i] = part if outs[i] is None else outs[i] + part

    prev = None
    for c0 in range(0, d_ff, f_chunk):
        c1 = min(c0 + f_chunk, d_ff)
        gated = []
        for i, h in enumerate(hs):
            a = _dot(h, w1_ref[:, c0:c1])
            b = _dot(h, w3_ref[:, c0:c1])
            gated.append((_silu(a) * b).astype(BF16))
        if prev is not None:
            for i in range(len(hs)):
                down(i, *prev)
        pending, prev = gated, (c0, c1)
    for i, r0 in enumerate(starts):
        down(i, *prev)
        o_ref[0, r0:r0 + sub] = (x_ref[0, r0:r0 + sub]
                                 + (coef * rows(gt_ref, r0)) * (_rms(outs[i]) * gpost_ref[...]))


def _mod_spec(mod, tm):
    if mod.shape[1] == 1:
        return pl.BlockSpec((1, 1, mod.shape[2]), lambda b, s: (b, 0, 0))
    return pl.BlockSpec((1, tm, mod.shape[2]), lambda b, s: (b, s, 0))


def _ffn(x, shift, scale, gate, g_pre, g_post, w1, w3, w2, l, k, tm):
    nseq, length, d = x.shape
    d_ff = w1.shape[-1]
    f_chunk = min(d_ff, FFN_CHUNK)
    w_up = _resident((None, None, d, d_ff), lambda b, s: (l, k, 0, 0))
    w_dn = _resident((None, None, d_ff, d), lambda b, s: (l, k, 0, 0))
    vec = pl.BlockSpec((1, d), lambda b, s: (0, 0))
    tile = pl.BlockSpec((1, tm, d), lambda b, s: (b, s, 0))
    return pl.pallas_call(
        functools.partial(_ffn_kernel, coef=0.5, f_chunk=f_chunk, sub=min(tm, FFN_SUB_ROWS)),
        grid=(nseq, length // tm),
        in_specs=[tile, _mod_spec(shift, tm), _mod_spec(scale, tm), _mod_spec(gate, tm),
                  vec, vec, w_up, w_up, w_dn],
        out_specs=tile,
        out_shape=jax.ShapeDtypeStruct(x.shape, F32),
        compiler_params=_params("parallel", "parallel"),
        name="ffn",
    )(x, shift, scale, gate, g_pre, g_post, w1, w3, w2)


def _proj_kernel(x_ref, sh_ref, sc_ref, gpre_ref, win_ref, qn_ref, kvn_ref, wuq_ref,
                 cq_ref, sq_ref, ck_ref, sk_ref, rc_ref, rsa_ref, rsb_ref,
                 q_out, k_out, vt_out, kv_out, kpe_out, rq_out, rk_out, rv_out, rg_out):
    x = x_ref[0]
    h = (_rms(x) * gpre_ref[...] * (1.0 + sc_ref[0]) + sh_ref[0]).astype(BF16)
    z = _dot(h, win_ref[...])

    q_lat = (_rms(z[:, :Q_RANK]) * qn_ref[...]).astype(BF16)
    kv_lat = _rms(z[:, Q_RANK:Q_RANK + KV_RANK]) * kvn_ref[...]
    c0 = Q_RANK + KV_RANK
    k_pe = (z[:, c0:c0 + MLA_ROPE] * ck_ref[...]
            + z[:, c0 + MLA_ROPE:c0 + 2 * MLA_ROPE] * sk_ref[...])
    kv_out[0] = kv_lat
    kpe_out[0] = k_pe
    k_out[0] = jnp.concatenate([kv_lat, k_pe], axis=-1).astype(BF16)
    ones = jnp.ones((VT_ROWS - KV_RANK, kv_lat.shape[0]), F32)
    vt_out[0, 0] = jnp.concatenate([kv_lat.T, ones], axis=0).astype(BF16)

    n_abs = MLA_HEADS * KV_RANK
    n_pe = MLA_HEADS * MLA_ROPE
    qq = _dot(q_lat, wuq_ref[...])
    q_abs = qq[:, :n_abs]
    q_pe = qq[:, n_abs:n_abs + n_pe] * cq_ref[...] + qq[:, n_abs + n_pe:] * sq_ref[...]
    for hd in range(MLA_HEADS):
        qh = jnp.concatenate([q_abs[:, hd * KV_RANK:(hd + 1) * KV_RANK],
                              q_pe[:, hd * MLA_ROPE:(hd + 1) * MLA_ROPE]], axis=-1)
        q_out[0, hd] = (qh * Q_SCALE).astype(BF16)

    r0 = c0 + KPE_BLOCK
    rc, rsa, rsb = rc_ref[...], rsa_ref[...], rsb_ref[...]
    for off, out, scale in ((r0, rq_out, 1.0), (r0 + RET_W, rk_out, RET_SCALE)):
        for c in range(RET_W // LANES):
            xc = z[:, off + c * LANES:off + (c + 1) * LANES]
            y = (xc * rc + pltpu.roll(xc, LANES - RET_DK // 2, 1) * rsa
                 + pltpu.roll(xc, RET_DK // 2, 1) * rsb)
            out[0, :, c * LANES:(c + 1) * LANES] = (y * scale).astype(BF16)
    rv_out[0] = z[:, r0 + 2 * RET_W:r0 + 3 * RET_W].astype(BF16)
    rg_out[0] = z[:, r0 + 3 * RET_W:r0 + 4 * RET_W].astype(BF16)


def _proj(x, shift, scale, g_pre, w_in, q_norm, kv_norm, w_uq, tabs, l, tm):
    nseq, length, d = x.shape
    grid = (nseq, length // tm)
    vec = lambda n: pl.BlockSpec((1, n), lambda b, s: (0, 0))
    tile = lambda n: pl.BlockSpec((1, tm, n), lambda b, s: (b, s, 0))
    tab = lambda n: pl.BlockSpec((tm, n), lambda b, s: (s, 0))
    wl = lambda r, c: _resident((None, r, c), lambda b, s: (l, 0, 0))
    bf = lambda n: jax.ShapeDtypeStruct((nseq, length, n), BF16)
    return pl.pallas_call(
        _proj_kernel,
        grid=grid,
        in_specs=[tile(d), _mod_spec(shift, tm), _mod_spec(scale, tm), vec(d),
                  wl(d, W_IN_COLS), vec(Q_RANK), vec(KV_RANK),
                  wl(Q_RANK, w_uq.shape[-1]),
                  tab(MLA_HEADS * MLA_ROPE), tab(MLA_HEADS * MLA_ROPE), tab(MLA_ROPE), tab(MLA_ROPE),
                  tab(LANES), tab(LANES), tab(LANES)],
        out_specs=[pl.BlockSpec((1, MLA_HEADS, tm, QK_DIM), lambda b, s: (b, 0, s, 0)),
                   tile(QK_DIM),
                   pl.BlockSpec((1, 1, VT_ROWS, tm), lambda b, s: (b, s, 0, 0)),
                   tile(KV_RANK), tile(MLA_ROPE),
                   tile(RET_W), tile(RET_W), tile(RET_W), tile(RET_W)],
        out_shape=[jax.ShapeDtypeStruct((nseq, MLA_HEADS, length, QK_DIM), BF16),
                   bf(QK_DIM),
                   jax.ShapeDtypeStruct((nseq, length // tm, VT_ROWS, tm), BF16),
                   jax.ShapeDtypeStruct((nseq, length, KV_RANK), F32),
                   jax.ShapeDtypeStruct((nseq, length, MLA_ROPE), F32),
                   bf(RET_W), bf(RET_W), bf(RET_W), bf(RET_W)],
        compiler_params=_params("parallel", "parallel"),
        name="mixer_proj",
    )(x, shift, scale, g_pre, w_in, q_norm, kv_norm, w_uq, *tabs)


def _rope_tables(pos):
    pos = np.asarray(pos, np.float32)[:, None]

    def cs(half):
        inv = (ROPE_BASE ** (-np.arange(half, dtype=np.float32) / half)).astype(np.float32)
        ang = pos * inv[None, :]
        return np.cos(ang), np.sin(ang)

    c16, s16 = cs(MLA_ROPE // 2)
    ck = np.concatenate([c16, c16], axis=-1)
    sk = np.concatenate([s16, s16], axis=-1)
    cq = np.tile(ck, (1, MLA_HEADS))
    sq = np.tile(sk, (1, MLA_HEADS))
    c32, s32 = cs(RET_DK // 2)
    zero = np.zeros_like(s32)
    reps = LANES // RET_DK
    rc = np.tile(np.concatenate([c32, c32], axis=-1), (1, reps))
    rsa = np.tile(np.concatenate([-s32, zero], axis=-1), (1, reps))
    rsb = np.tile(np.concatenate([zero, s32], axis=-1), (1, reps))
    return tuple(jnp.asarray(t, F32) for t in (cq, sq, ck, sk, rc, rsa, rsb))


def _attn_kernel(q_ref, k_ref, vt_ref, o_ref, m_scr, alpha_scr, p_scr, acc_scr, *, tq, tk, cb):
    i = pl.program_id(1)
    rows = MLA_HEADS * tq
    q = q_ref[0].reshape(rows, QK_DIM)
    m_scr[...] = jnp.full(m_scr.shape, NEG_INF, F32)
    acc_scr[...] = jnp.zeros(acc_scr.shape, F32)
    n_full = (i * tq) // tk

    def scores(j):
        start = pl.multiple_of(j * tk, tk)
        return _dot_nt(k_ref[0, pl.ds(start, tk), :], q)

    def values(j):
        for c0 in range(0, rows, cb):
            cs = slice(c0, c0 + cb)
            acc_scr[:, cs] = acc_scr[:, cs] * alpha_scr[:, cs] + _dot(vt_ref[0, j], p_scr[:, cs])

    def softmax(j, s, masked):
        if masked:
            kpos = j * tk + lax.broadcasted_iota(jnp.int32, (tk, tq), 0)
            qpos = i * tq + lax.broadcasted_iota(jnp.int32, (tk, tq), 1)
            allowed = (kpos // CHUNK) <= (qpos // CHUNK)
            allowed = jnp.concatenate([allowed] * (cb // tq), axis=1)
        for c0 in range(0, rows, cb):
            cs = slice(c0, c0 + cb)
            sc = s[:, cs]
            if masked:
                sc = jnp.where(allowed, sc, NEG_INF)
            m_prev = m_scr[:, cs]
            m_part = jnp.max(sc.reshape(MAX_GROUPS, tk // MAX_GROUPS, cb), axis=0)
            m_new = jnp.maximum(m_prev, jnp.max(m_part, axis=0, keepdims=True))
            alpha_scr[:, cs] = jnp.exp2(m_prev - m_new)
            p_scr[:, cs] = jnp.exp2((sc - m_new).astype(BF16))
            m_scr[:, cs] = m_new

    def pipelined(j, masked):
        s = scores(j)
        values(j - 1)
        softmax(j, s, masked)

    @pl.when(n_full > 0)
    def _():
        softmax(0, scores(0), False)

        def body(j, carry):
            pipelined(j, False)
            return carry

        lax.fori_loop(1, n_full, body, 0)
        pipelined(n_full, True)

    @pl.when(n_full == 0)
    def _():
        softmax(0, scores(0), True)

    values(n_full)

    acc = acc_scr[...]
    o_t = acc[:KV_RANK] / acc[KV_RANK:KV_RANK + 1]
    for hd in range(MLA_HEADS):
        o_ref[0, :, hd * KV_RANK:(hd + 1) * KV_RANK] = o_t[:, hd * tq:(hd + 1) * tq].T.astype(BF16)


def _attention_prompt(q, k, vt, tq, tk):
    nseq, _, length, _ = q.shape
    rows = MLA_HEADS * tq
    assert vt.shape[-1] == tk and tk % tq == 0
    cb = max(tq, MXU_DIM)
    return pl.pallas_call(
        functools.partial(_attn_kernel, tq=tq, tk=tk, cb=cb),
        grid=(nseq, length // tq),
        in_specs=[pl.BlockSpec((1, MLA_HEADS, tq, QK_DIM), lambda b, i: (b, 0, i, 0)),
                  pl.BlockSpec((1, length, QK_DIM), lambda b, i: (b, 0, 0)),
                  pl.BlockSpec((1, length // tk, VT_ROWS, tk), lambda b, i: (b, 0, 0, 0))],
        out_specs=pl.BlockSpec((1, tq, MLA_HEADS * KV_RANK), lambda b, i: (b, i, 0)),
        out_shape=jax.ShapeDtypeStruct((nseq, length, MLA_HEADS * KV_RANK), BF16),
        scratch_shapes=[pltpu.VMEM((1, rows), F32), pltpu.VMEM((1, rows), F32),
                        pltpu.VMEM((tk, rows), BF16), pltpu.VMEM((VT_ROWS, rows), F32)],
        compiler_params=_params("parallel", "arbitrary"),
        name="mla_attention_prompt",
    )(q, k, vt)


def _chunk_mask(s, t, q_start, k_start):
    n = s.shape[1]
    qpos = q_start + lax.broadcasted_iota(jnp.int32, (t, n), 0)
    kpos = k_start + lax.broadcasted_iota(jnp.int32, (t, n), 1)
    allowed = (kpos // CHUNK) <= (qpos // CHUNK)
    return jnp.where(allowed[None], s.reshape(MLA_HEADS, t, n), NEG_INF).reshape(s.shape)


def _attn_sample_kernel(q_ref, ckv_ref, ckpe_ref, kn_ref, o_ref, *, t, past, mask_cache, mask_new):
    rows = MLA_HEADS * t
    q = q_ref[0].reshape(rows, QK_DIM)
    ckv = ckv_ref[0].astype(BF16)
    ckpe = ckpe_ref[0].astype(BF16)
    kn = kn_ref[0]
    s_c = _dot_nt(q[:, :KV_RANK], ckv) + _dot_nt(q[:, KV_RANK:], ckpe)
    s_n = _dot_nt(q, kn)
    if mask_cache:
        s_c = _chunk_mask(s_c, t, past, 0)
    if mask_new:
        s_n = _chunk_mask(s_n, t, past, past)
    m = jnp.maximum(jnp.max(s_c, axis=1, keepdims=True), jnp.max(s_n, axis=1, keepdims=True))
    p_c = jnp.exp2(s_c - m)
    p_n = jnp.exp2(s_n - m)
    denom = jnp.sum(p_c, axis=1, keepdims=True) + jnp.sum(p_n, axis=1, keepdims=True)
    o = _dot(p_c.astype(BF16), ckv) + _dot(p_n.astype(BF16), kn[:, :KV_RANK])
    o = (o / denom).astype(BF16)
    for hd in range(MLA_HEADS):
        o_ref[0, :, hd * KV_RANK:(hd + 1) * KV_RANK] = o[hd * t:(hd + 1) * t]


def _attention_sample(q, cache_kv, cache_kpe, k_new, l):
    nb, _, t, _ = q.shape
    past = cache_kv.shape[2]
    q_chunk = (past + np.arange(t)) // CHUNK
    k_chunk = np.arange(past + t) // CHUNK
    allowed = k_chunk[None, :] <= q_chunk[:, None]
    mask_cache = not allowed[:, :past].all()
    mask_new = not allowed[:, past:].all()
    return pl.pallas_call(
        functools.partial(_attn_sample_kernel, t=t, past=past, mask_cache=mask_cache, mask_new=mask_new),
        grid=(nb,),
        in_specs=[pl.BlockSpec((1, MLA_HEADS, t, QK_DIM), lambda b: (b, 0, 0, 0)),
                  pl.BlockSpec((None, 1, past, KV_RANK), lambda b: (l, b, 0, 0)),
                  pl.BlockSpec((None, 1, past, MLA_ROPE), lambda b: (l, b, 0, 0)),
                  pl.BlockSpec((1, t, QK_DIM), lambda b: (b, 0, 0))],
        out_specs=pl.BlockSpec((1, t, MLA_HEADS * KV_RANK), lambda b: (b, 0, 0)),
        out_shape=jax.ShapeDtypeStruct((nb, t, MLA_HEADS * KV_RANK), BF16),
        compiler_params=_params("parallel"),
        name="mla_attention_sample",
    )(q, cache_kv, cache_kpe, k_new)


def _ret_kernel(rq_ref, rk_ref, rv_ref, s0_ref, dec_ref, cross_ref, kdec_ref, gt_ref,
                o_ref, sout_ref, st_scr):
    c = pl.program_id(1)

    @pl.when(c == 0)
    def _():
        st_scr[...] = s0_ref[0]

    heads = range(RET_HEADS)
    ks = [slice(hd * RET_DK, (hd + 1) * RET_DK) for hd in heads]
    vs = [slice(hd * RET_DV, (hd + 1) * RET_DV) for hd in heads]
    qs = [rq_ref[0, :, ks[hd]] for hd in heads]
    kk = [rk_ref[0, :, ks[hd]] for hd in heads]
    vv = [rv_ref[0, :, vs[hd]] for hd in heads]
    st = [st_scr[hd] for hd in heads]
    scores = [_dot_nt(qs[hd], kk[hd]) for hd in heads]
    cross = [_dot(qs[hd], st[hd].astype(BF16)) for hd in heads]
    for hd in heads:
        k_dec = (kk[hd].astype(F32) * kdec_ref[hd]).astype(BF16)
        st_scr[hd] = gt_ref[hd] * st[hd] + _dot_tn(k_dec, vv[hd])
    for hd in heads:
        sc = (scores[hd] * dec_ref[hd]).astype(BF16)
        o_ref[0, :, vs[hd]] = (_dot(sc, vv[hd]) + cross[hd] * cross_ref[hd]).astype(BF16)

    @pl.when(c == pl.num_programs(1) - 1)
    def _():
        sout_ref[0] = st_scr[...]


def _ret_tables(t):
    f32 = np.float32
    log_g = np.log1p(-np.exp2(-5.0 - np.arange(RET_HEADS, dtype=f32))).astype(f32)
    n = np.arange(t, dtype=f32)
    diff = n[:, None] - n[None, :]
    dec = np.where(diff >= 0.0, np.exp(np.maximum(diff, 0.0)[None] * log_g[:, None, None]), 0.0)
    cross = np.exp((n + 1.0)[None, :] * log_g[:, None])
    kdec = np.exp((t - 1.0 - n)[None, :] * log_g[:, None])
    gt = np.exp(f32(t) * log_g)
    cross = np.broadcast_to(cross[:, :, None], (RET_HEADS, t, RET_DV))
    kdec = np.broadcast_to(kdec[:, :, None], (RET_HEADS, t, RET_DK))
    gt = np.broadcast_to(gt[:, None, None], (RET_HEADS, RET_DK, RET_DV))
    return tuple(jnp.asarray(a, F32) for a in (dec, cross, kdec, gt))


def _retention(rq, rk, rv, state0, tabs, t):
    nseq, length, _ = rq.shape
    tile = pl.BlockSpec((1, t, RET_W), lambda b, c: (b, c, 0))
    st = pl.BlockSpec((1, RET_HEADS, RET_DK, RET_DV), lambda b, c: (b, 0, 0, 0))
    const = lambda a: pl.BlockSpec(a.shape, lambda b, c: (0, 0, 0))
    return pl.pallas_call(
        _ret_kernel,
        grid=(nseq, length // t),
        in_specs=[tile, tile, tile, st] + [const(a) for a in tabs],
        out_specs=[tile, st],
        out_shape=[jax.ShapeDtypeStruct((nseq, length, RET_W), BF16),
                   jax.ShapeDtypeStruct(state0.shape, F32)],
        scratch_shapes=[pltpu.VMEM((RET_HEADS, RET_DK, RET_DV), F32)],
        compiler_params=_params("parallel", "arbitrary"),
        name="retention",
    )(rq, rk, rv, state0, *tabs)


def _out_kernel(x_ref, ol_ref, or_ref, rg_ref, gt_ref, gpost_ref, wuv_ref, wo_ref, grp_ref, o_ref):
    n_mla = MLA_HEADS * MLA_V
    o_mla = _dot(ol_ref[0], wuv_ref[...]).astype(BF16)
    mix = _dot(o_mla, wo_ref[:n_mla, :])
    o = or_ref[0].astype(F32)
    sq = o * o
    hi = sq.astype(BF16)
    lo = (sq - hi.astype(F32)).astype(BF16)
    ms = _dot(hi, grp_ref[...]) + _dot(lo, grp_ref[...])
    o_ret = (o * lax.rsqrt(ms + EPS) * _silu(rg_ref[0].astype(F32))).astype(BF16)
    mix = mix + _dot(o_ret, wo_ref[n_mla:, :])
    o_ref[0] = x_ref[0] + gt_ref[0] * (_rms(mix) * gpost_ref[...])


def _mixer_out(x, o_lat, o_ret, rg, gate, g_post, w_uv, w_out, l, tm):
    nseq, length, d = x.shape
    tile = lambda n: pl.BlockSpec((1, tm, n), lambda b, s: (b, s, 0))
    wl = lambda r, c: _resident((None, r, c), lambda b, s: (l, 0, 0))
    head = np.arange(RET_W) // RET_DV
    group_mean = jnp.asarray((head[:, None] == head[None, :]).astype(np.float32) / RET_DV, BF16)
    return pl.pallas_call(
        _out_kernel,
        grid=(nseq, length // tm),
        in_specs=[tile(d), tile(o_lat.shape[-1]), tile(RET_W), tile(RET_W), _mod_spec(gate, tm),
                  pl.BlockSpec((1, d), lambda b, s: (0, 0)),
                  wl(*w_uv.shape[1:]), wl(*w_out.shape[1:]),
                  _resident((RET_W, RET_W), lambda b, s: (0, 0))],
        out_specs=tile(d),
        out_shape=jax.ShapeDtypeStruct(x.shape, F32),
        compiler_params=_params("parallel", "parallel"),
        name="mixer_out",
    )(x, o_lat, o_ret, rg, gate, g_post, w_uv, w_out, group_mean)


def _prep_w_in(w_in):
    c0 = Q_RANK + KV_RANK
    half = MLA_ROPE // 2
    kpe = w_in[..., c0:c0 + MLA_ROPE]
    kpe_rot = jnp.concatenate([-kpe[..., half:], kpe[..., :half]], axis=-1)
    pad = jnp.zeros(w_in.shape[:-1] + (KPE_BLOCK - 2 * MLA_ROPE,), w_in.dtype)
    return jnp.concatenate([w_in[..., :c0], kpe, kpe_rot, pad, w_in[..., c0 + MLA_ROPE:]],
                           axis=-1).astype(BF16)


def _absorb_kernel(uq_ref, uk_ref, o_ref):
    o_ref[0, 0] = lax.dot_general(uq_ref[0, 0], uk_ref[0, 0], (((1,), (1,)), ((), ())),
                                  preferred_element_type=F32, precision=lax.Precision.HIGHEST)


def _prep_w_uq(w_uq, w_uk):
    depth = w_uq.shape[0]
    half = MLA_ROPE // 2
    w = w_uq.reshape(depth, Q_RANK, MLA_HEADS, MLA_NOPE + MLA_ROPE)
    nope = w[..., :MLA_NOPE].transpose(0, 2, 1, 3)
    uk = w_uk.reshape(depth, KV_RANK, MLA_HEADS, MLA_NOPE).transpose(0, 2, 1, 3)
    absorbed = pl.pallas_call(
        _absorb_kernel,
        grid=(depth, MLA_HEADS),
        in_specs=[pl.BlockSpec((1, 1, Q_RANK, MLA_NOPE), lambda l, h: (l, h, 0, 0)),
                  pl.BlockSpec((1, 1, KV_RANK, MLA_NOPE), lambda l, h: (l, h, 0, 0))],
        out_specs=pl.BlockSpec((1, 1, Q_RANK, KV_RANK), lambda l, h: (l, h, 0, 0)),
        out_shape=jax.ShapeDtypeStruct((depth, MLA_HEADS, Q_RANK, KV_RANK), F32),
        compiler_params=_params("parallel", "parallel"),
        name="absorb_w_uk",
    )(nope, uk)
    absorbed = absorbed.transpose(0, 2, 1, 3).reshape(depth, Q_RANK, MLA_HEADS * KV_RANK)
    pe = w[..., MLA_NOPE:]
    rot = jnp.concatenate([-pe[..., half:], pe[..., :half]], axis=-1)
    flat = lambda a: a.reshape(depth, Q_RANK, MLA_HEADS * MLA_ROPE)
    return jnp.concatenate([absorbed, flat(pe), flat(rot)], axis=-1).astype(BF16)


def _prep_w_uv(w_uv):
    depth = w_uv.shape[0]
    w = w_uv.reshape(depth, KV_RANK, MLA_HEADS, MLA_V)
    eye = jnp.eye(MLA_HEADS, dtype=w_uv.dtype)
    bd = jnp.einsum('lrhv,hg->lhrgv', w, eye)
    return bd.reshape(depth, MLA_HEADS * KV_RANK, MLA_HEADS * MLA_V).astype(BF16)


def _pick_tile(length, target):
    t = min(length, target)
    while length % t:
        t //= 2
    return t


def kernel(x_prompt, x_sample, cache_kv_latent, cache_k_rope, state_ret, c_prompt, c_sample,
           ada_w, ada_b, norm_pre, norm_post, ffn_w1, ffn_w3, ffn_w2,
           w_in, q_norm, kv_norm, w_uq, w_uk, w_uv, w_out):
    depth = ada_w.shape[0]
    nb_p, seq, d = x_prompt.shape
    nb_s, dec_seq, _ = x_sample.shape
    past = cache_kv_latent.shape[2]
    n_tok_s = nb_s * dec_seq

    w1, w3, w2 = ffn_w1.astype(BF16), ffn_w3.astype(BF16), ffn_w2.astype(BF16)
    w_in_b = _prep_w_in(w_in)
    w_uq_b = _prep_w_uq(w_uq, w_uk)
    w_uv_b = _prep_w_uv(w_uv)
    w_out_b = w_out.astype(BF16)

    mods = _ada_mods(jnp.concatenate([c_prompt, c_sample], axis=0), ada_w, ada_b)

    tm_p = _pick_tile(seq, 512)
    tm_ffn = _pick_tile(seq, 2 * FFN_SUB_ROWS)
    tq = _pick_tile(seq, 512)
    t_ret = _pick_tile(seq, 256)
    tabs_p = _rope_tables(np.arange(seq))
    tabs_s = _rope_tables(np.tile(past + np.arange(dec_seq), nb_s))
    ret_tabs_p = _ret_tables(t_ret)
    ret_tabs_s = _ret_tables(dec_seq)
    zero_state = jnp.zeros((nb_p, RET_HEADS, RET_DK, RET_DV), F32)

    def mod_p(l, r):
        return mods[l, :nb_p, r][:, None, :]

    def mod_s(l, r):
        return jnp.repeat(mods[l, nb_p:, r], dec_seq, axis=0)[None]

    y_p = x_prompt
    y_s = x_sample.reshape(1, n_tok_s, d)
    lat_p, kpe_p, ret_p, lat_s, kpe_s, ret_s = [], [], [], [], [], []

    for l in range(depth):
        gp = lambda j: norm_pre[l, j][None, :]
        go = lambda j: norm_post[l, j][None, :]
        qn, kvn = q_norm[l][None, :], kv_norm[l][None, :]

        y_p = _ffn(y_p, mod_p(l, 0), mod_p(l, 1), mod_p(l, 2), gp(0), go(0), w1, w3, w2, l, 0, tm_ffn)
        q, k, vt, kv, kpe, rq, rk, rv, rg = _proj(y_p, mod_p(l, 3), mod_p(l, 4), gp(1), w_in_b, qn, kvn,
                                                  w_uq_b, tabs_p, l, tm_p)
        o_lat = _attention_prompt(q, k, vt, tq, tm_p)
        o_ret, r_fin = _retention(rq, rk, rv, zero_state, ret_tabs_p, t_ret)
        y_p = _mixer_out(y_p, o_lat, o_ret, rg, mod_p(l, 5), go(1), w_uv_b, w_out_b, l, tm_p)
        y_p = _ffn(y_p, mod_p(l, 6), mod_p(l, 7), mod_p(l, 8), gp(2), go(2), w1, w3, w2, l, 1, tm_ffn)
        lat_p.append(kv); kpe_p.append(kpe); ret_p.append(r_fin)

        y_s = _ffn(y_s, mod_s(l, 0), mod_s(l, 1), mod_s(l, 2), gp(0), go(0), w1, w3, w2, l, 0, n_tok_s)
        q, k, _, kv, kpe, rq, rk, rv, rg = _proj(y_s, mod_s(l, 3), mod_s(l, 4), gp(1), w_in_b, qn, kvn,
                                                 w_uq_b, tabs_s, l, n_tok_s)
        per_seq = lambda a: a.reshape(nb_s, dec_seq, a.shape[-1])
        q = q.reshape(MLA_HEADS, nb_s, dec_seq, QK_DIM).swapaxes(0, 1)
        o_lat = _attention_sample(q, cache_kv_latent, cache_k_rope, per_seq(k), l)
        o_ret, r_new = _retention(per_seq(rq), per_seq(rk), per_seq(rv),
                                  state_ret[l], ret_tabs_s, dec_seq)
        y_s = _mixer_out(y_s, o_lat.reshape(1, n_tok_s, -1), o_ret.reshape(1, n_tok_s, -1), rg,
                         mod_s(l, 5), go(1), w_uv_b, w_out_b, l, n_tok_s)
        y_s = _ffn(y_s, mod_s(l, 6), mod_s(l, 7), mod_s(l, 8), gp(2), go(2), w1, w3, w2, l, 1, n_tok_s)
        lat_s.append(per_seq(kv)); kpe_s.append(per_seq(kpe)); ret_s.append(r_new)

    return (y_p, y_s.reshape(nb_s, dec_seq, d),
            jnp.stack(lat_p), jnp.stack(kpe_p), jnp.stack(ret_p),
            jnp.stack(lat_s), jnp.stack(kpe_s), jnp.stack(ret_s))
```

```python
import functools

import jax
import jax.numpy as jnp
import numpy as np
from jax import lax
from jax.experimental import pallas as pl
from jax.experimental.pallas import tpu as pltpu

CHUNK = 64
N_SUB = 3
MLA_HEADS = 8
MLA_NOPE = 64
MLA_ROPE = 32
MLA_V = 64
Q_RANK = 256
KV_RANK = 128
RET_HEADS = 8
RET_DK = 64
RET_DV = 64
ROPE_BASE = 10000.0
EPS = 1e-6
MLA_SCALE = (MLA_NOPE + MLA_ROPE) ** -0.5
Q_SCALE = MLA_SCALE * float(np.log2(np.e))
MAX_GROUPS = 8
MXU_DIM = 256
FFN_CHUNK = 3 * MXU_DIM
FFN_SUB_ROWS = 512
Q_SUB = 128
RET_SCALE = RET_DK ** -0.5
NEG_INF = -1e30

QK_DIM = KV_RANK + MLA_ROPE
RET_W = RET_HEADS * RET_DK
LANES = 128
KPE_BLOCK = LANES
BF16_SUBLANES = 16
VT_ROWS = KV_RANK + BF16_SUBLANES
W_IN_COLS = Q_RANK + KV_RANK + KPE_BLOCK + 4 * RET_W

F32 = jnp.float32
BF16 = jnp.bfloat16

VMEM_LIMIT = 56 * 1024 * 1024


def _params(*sem):
    return pltpu.CompilerParams(dimension_semantics=sem, vmem_limit_bytes=VMEM_LIMIT)


def _resident(block_shape, index_map):
    return pl.BlockSpec(block_shape, index_map, pipeline_mode=pl.Buffered(1))


def _rms(x):
    return x * lax.rsqrt(jnp.mean(x * x, axis=-1, keepdims=True) + EPS)


def _silu(x):
    return x * jax.nn.sigmoid(x)


def _dot(a, b):
    return jnp.dot(a, b, preferred_element_type=F32)


def _dot_nt(a, b):
    return lax.dot_general(a, b, (((1,), (1,)), ((), ())), preferred_element_type=F32)


def _dot_tn(a, b):
    return lax.dot_general(a, b, (((0,), (0,)), ((), ())), preferred_element_type=F32)


def _ada_kernel(c_ref, w_ref, b_ref, o_ref):
    sc = _silu(c_ref[...])
    o_ref[0] = jnp.dot(sc, w_ref[0], preferred_element_type=F32,
                       precision=lax.Precision.HIGHEST) + b_ref[0]


def _ada_mods(c_all, ada_w, ada_b):
    depth, d, cols = ada_w.shape
    nb = c_all.shape[0]
    n_col = cols // d
    out = pl.pallas_call(
        _ada_kernel,
        grid=(depth, n_col),
        in_specs=[
            pl.BlockSpec((nb, d), lambda l, j: (0, 0)),
            pl.BlockSpec((1, d, d), lambda l, j: (l, 0, j)),
            pl.BlockSpec((1, 1, d), lambda l, j: (l, 0, j)),
        ],
        out_specs=pl.BlockSpec((1, nb, d), lambda l, j: (l, 0, j)),
        out_shape=jax.ShapeDtypeStruct((depth, nb, cols), F32),
        compiler_params=_params("parallel", "parallel"),
        name="ada_mods",
    )(c_all, ada_w, ada_b.reshape(depth, 1, cols))
    return out.reshape(depth, nb, n_col, d)


def _ffn_kernel(x_ref, sh_ref, sc_ref, gt_ref, gpre_ref, gpost_ref, w1_ref, w3_ref, w2_ref, o_ref,
                *, coef, f_chunk, sub):
    d_ff = w1_ref.shape[1]
    starts = list(range(0, x_ref.shape[1], sub))
    rows = lambda ref, r0: ref[0] if ref.shape[1] == 1 else ref[0, r0:r0 + sub]
    hs = [(_rms(x_ref[0, r0:r0 + sub]) * gpre_ref[...] * (1.0 + rows(sc_ref, r0))
           + rows(sh_ref, r0)).astype(BF16) for r0 in starts]
    outs = [None] * len(starts)
    pending = [None] * len(starts)

    def down(i, c0, c1):
        part = _dot(pending[i], w2_ref[c0:c1, :])
        outs[i] = part if outs[i] is None else outs[i] + part

    prev = None
    for c0 in range(0, d_ff, f_chunk):
        c1 = min(c0 + f_chunk, d_ff)
        gated = []
        for i, h in enumerate(hs):
            a = _dot(h, w1_ref[:, c0:c1])
            b = _dot(h, w3_ref[:, c0:c1])
            gated.append((_silu(a) * b).astype(BF16))
        if prev is not None:
            for i in range(len(hs)):
                down(i, *prev)
        pending, prev = gated, (c0, c1)
    for i, r0 in enumerate(starts):
        down(i, *prev)
        o_ref[0, r0:r0 + sub] = (x_ref[0, r0:r0 + sub]
                                 + (coef * rows(gt_ref, r0)) * (_rms(outs[i]) * gpost_ref[...]))


def _mod_spec(mod, tm):
    if mod.shape[1] == 1:
        return pl.BlockSpec((1, 1, mod.shape[2]), lambda b, s: (b, 0, 0))
    return pl.BlockSpec((1, tm, mod.shape[2]), lambda b, s: (b, s, 0))


def _ffn(x, shift, scale, gate, g_pre, g_post, w1, w3, w2, l, k, tm):
    nseq, length, d = x.shape
    d_ff = w1.shape[-1]
    f_chunk = min(d_ff, FFN_CHUNK)
    w_up = _resident((None, None, d, d_ff), lambda b, s: (l, k, 0, 0))
    w_dn = _resident((None, None, d_ff, d), lambda b, s: (l, k, 0, 0))
    vec = pl.BlockSpec((1, d), lambda b, s: (0, 0))
    tile = pl.BlockSpec((1, tm, d), lambda b, s: (b, s, 0))
    return pl.pallas_call(
        functools.partial(_ffn_kernel, coef=0.5, f_chunk=f_chunk, sub=min(tm, FFN_SUB_ROWS)),
        grid=(nseq, length // tm),
        in_specs=[tile, _mod_spec(shift, tm), _mod_spec(scale, tm), _mod_spec(gate, tm),
                  vec, vec, w_up, w_up, w_dn],
        out_specs=tile,
        out_shape=jax.ShapeDtypeStruct(x.shape, F32),
        compiler_params=_params("parallel", "parallel"),
        name="ffn",
    )(x, shift, scale, gate, g_pre, g_post, w1, w3, w2)


def _proj_kernel(x_ref, sh_ref, sc_ref, gpre_ref, win_ref, qn_ref, kvn_ref, wuq_ref,
                 cq_ref, sq_ref, ck_ref, sk_ref, rc_ref, rsa_ref, rsb_ref,
                 q_out, k_out, vt_out, kv_out, kpe_out, rq_out, rk_out, rv_out, rg_out):
    x = x_ref[0]
    h = (_rms(x) * gpre_ref[...] * (1.0 + sc_ref[0]) + sh_ref[0]).astype(BF16)
    z = _dot(h, win_ref[...])

    q_lat = (_rms(z[:, :Q_RANK]) * qn_ref[...]).astype(BF16)
    kv_lat = _rms(z[:, Q_RANK:Q_RANK + KV_RANK]) * kvn_ref[...]
    c0 = Q_RANK + KV_RANK
    k_pe = (z[:, c0:c0 + MLA_ROPE] * ck_ref[...]
            + z[:, c0 + MLA_ROPE:c0 + 2 * MLA_ROPE] * sk_ref[...])
    kv_out[0] = kv_lat
    kpe_out[0] = k_pe
    k_out[0] = jnp.concatenate([kv_lat, k_pe], axis=-1).astype(BF16)
    ones = jnp.ones((VT_ROWS - KV_RANK, kv_lat.shape[0]), F32)
    vt_out[0, 0] = jnp.concatenate([kv_lat.T, ones], axis=0).astype(BF16)

    n_abs = MLA_HEADS * KV_RANK
    n_pe = MLA_HEADS * MLA_ROPE
    qq = _dot(q_lat, wuq_ref[...])
    q_abs = qq[:, :n_abs]
    q_pe = qq[:, n_abs:n_abs + n_pe] * cq_ref[...] + qq[:, n_abs + n_pe:] * sq_ref[...]
    for hd in range(MLA_HEADS):
        qh = jnp.concatenate([q_abs[:, hd * KV_RANK:(hd + 1) * KV_RANK],
                              q_pe[:, hd * MLA_ROPE:(hd + 1) * MLA_ROPE]], axis=-1)
        qh = (qh * Q_SCALE).astype(BF16)
        qsub = q_out.shape[3]
        for r in range(q_out.shape[1]):
            q_out[0, r, hd] = qh[r * qsub:(r + 1) * qsub]

    r0 = c0 + KPE_BLOCK
    rc, rsa, rsb = rc_ref[...], rsa_ref[...], rsb_ref[...]
    for off, out, scale in ((r0, rq_out, 1.0), (r0 + RET_W, rk_out, RET_SCALE)):
        for c in range(RET_W // LANES):
            xc = z[:, off + c * LANES:off + (c + 1) * LANES]
            y = (xc * rc + pltpu.roll(xc, LANES - RET_DK // 2, 1) * rsa
                 + pltpu.roll(xc, RET_DK // 2, 1) * rsb)
            out[0, :, c * LANES:(c + 1) * LANES] = (y * scale).astype(BF16)
    rv_out[0] = z[:, r0 + 2 * RET_W:r0 + 3 * RET_W].astype(BF16)
    rg_out[0] = z[:, r0 + 3 * RET_W:r0 + 4 * RET_W].astype(BF16)


def _proj(x, shift, scale, g_pre, w_in, q_norm, kv_norm, w_uq, tabs, l, tm, qsub):
    nseq, length, d = x.shape
    grid = (nseq, length // tm)
    vec = lambda n: pl.BlockSpec((1, n), lambda b, s: (0, 0))
    tile = lambda n: pl.BlockSpec((1, tm, n), lambda b, s: (b, s, 0))
    tab = lambda n: pl.BlockSpec((tm, n), lambda b, s: (s, 0))
    wl = lambda r, c: _resident((None, r, c), lambda b, s: (l, 0, 0))
    bf = lambda n: jax.ShapeDtypeStruct((nseq, length, n), BF16)
    return pl.pallas_call(
        _proj_kernel,
        grid=grid,
        in_specs=[tile(d), _mod_spec(shift, tm), _mod_spec(scale, tm), vec(d),
                  wl(d, W_IN_COLS), vec(Q_RANK), vec(KV_RANK),
                  wl(Q_RANK, w_uq.shape[-1]),
                  tab(MLA_HEADS * MLA_ROPE), tab(MLA_HEADS * MLA_ROPE), tab(MLA_ROPE), tab(MLA_ROPE),
                  tab(LANES), tab(LANES), tab(LANES)],
        out_specs=[pl.BlockSpec((1, tm // qsub, MLA_HEADS, qsub, QK_DIM), lambda b, s: (b, s, 0, 0, 0)),
                   tile(QK_DIM),
                   pl.BlockSpec((1, 1, VT_ROWS, tm), lambda b, s: (b, s, 0, 0)),
                   tile(KV_RANK), tile(MLA_ROPE),
                   tile(RET_W), tile(RET_W), tile(RET_W), tile(RET_W)],
        out_shape=[jax.ShapeDtypeStruct((nseq, length // qsub, MLA_HEADS, qsub, QK_DIM), BF16),
                   bf(QK_DIM),
                   jax.ShapeDtypeStruct((nseq, length // tm, VT_ROWS, tm), BF16),
                   jax.ShapeDtypeStruct((nseq, length, KV_RANK), F32),
                   jax.ShapeDtypeStruct((nseq, length, MLA_ROPE), F32),
                   bf(RET_W), bf(RET_W), bf(RET_W), bf(RET_W)],
        compiler_params=_params("parallel", "parallel"),
        name="mixer_proj",
    )(x, shift, scale, g_pre, w_in, q_norm, kv_norm, w_uq, *tabs)


def _rope_tables(pos):
    pos = np.asarray(pos, np.float32)[:, None]

    def cs(half):
        inv = (ROPE_BASE ** (-np.arange(half, dtype=np.float32) / half)).astype(np.float32)
        ang = pos * inv[None, :]
        return np.cos(ang), np.sin(ang)

    c16, s16 = cs(MLA_ROPE // 2)
    ck = np.concatenate([c16, c16], axis=-1)
    sk = np.concatenate([s16, s16], axis=-1)
    cq = np.tile(ck, (1, MLA_HEADS))
    sq = np.tile(sk, (1, MLA_HEADS))
    c32, s32 = cs(RET_DK // 2)
    zero = np.zeros_like(s32)
    reps = LANES // RET_DK
    rc = np.tile(np.concatenate([c32, c32], axis=-1), (1, reps))
    rsa = np.tile(np.concatenate([-s32, zero], axis=-1), (1, reps))
    rsb = np.tile(np.concatenate([zero, s32], axis=-1), (1, reps))
    return tuple(jnp.asarray(t, F32) for t in (cq, sq, ck, sk, rc, rsa, rsb))


def _attn_kernel(q_ref, k_ref, vt_ref, o_ref, m_scr, alpha_scr, p_scr, acc_scr, *, t, qs):
    i = pl.program_id(1)
    nq = pl.num_programs(1) - 1
    n_sub = t // qs
    cb = MLA_HEADS * qs
    rows = n_sub * cb
    subs = [(r, slice(r * cb, (r + 1) * cb)) for r in range(n_sub)]
    slot = i % 2
    q = q_ref[0].reshape(rows, QK_DIM)

    def k_tile(j):
        return k_ref[0, pl.ds(pl.multiple_of(j * t, t), t), :]

    def scores(j):
        return _dot_nt(k_tile(j), q)

    def scores_diag(j):
        kt = k_tile(j)
        return [_dot_nt(kt[:(r + 1) * qs], q[cs]) for r, cs in subs]

    def update(cs, sc):
        n = sc.shape[0]
        m_prev = m_scr[:, cs]
        m_part = jnp.max(sc.reshape(MAX_GROUPS, n // MAX_GROUPS, cb), axis=0)
        m_new = jnp.maximum(m_prev, jnp.max(m_part, axis=0, keepdims=True))
        alpha_scr[:, cs] = jnp.exp2(m_prev - m_new)
        p_scr[:n, cs] = jnp.exp2((sc - m_new).astype(BF16))
        m_scr[:, cs] = m_new

    def softmax(s):
        for _, cs in subs:
            update(cs, s[:, cs])

    def softmax_diag(parts):
        kc = lax.broadcasted_iota(jnp.int32, (qs, qs), 0) // CHUNK
        qc = lax.broadcasted_iota(jnp.int32, (qs, qs), 1) // CHUNK
        allowed = jnp.concatenate([kc <= qc] * MLA_HEADS, axis=1)
        for (r, cs), sc in zip(subs, parts):
            last = jnp.where(allowed, sc[r * qs:], NEG_INF)
            update(cs, last if r == 0 else jnp.concatenate([sc[:r * qs], last], axis=0))

    def values(j, acc_slot):
        for _, cs in subs:
            acc_scr[acc_slot, :, cs] = (acc_scr[acc_slot, :, cs] * alpha_scr[:, cs]
                                        + _dot(vt_ref[0, j], p_scr[:, cs]))

    def values_diag(j, acc_slot):
        for r, cs in subs:
            n = (r + 1) * qs
            acc_scr[acc_slot, :, cs] = (acc_scr[acc_slot, :, cs] * alpha_scr[:, cs]
                                        + _dot(vt_ref[0, j, :, :n], p_scr[:n, cs]))

    def begin():
        m_scr[...] = jnp.full(m_scr.shape, NEG_INF, F32)
        acc_scr[slot] = jnp.zeros(acc_scr.shape[1:], F32)

    def drain_previous():
        prev = 1 - slot
        values_diag(i - 1, prev)
        acc = acc_scr[prev]
        o_t = acc[:KV_RANK] / acc[KV_RANK:KV_RANK + 1]
        for r in range(n_sub):
            for hd in range(MLA_HEADS):
                c0 = (r * MLA_HEADS + hd) * qs
                o_ref[0, r * qs:(r + 1) * qs, hd * KV_RANK:(hd + 1) * KV_RANK] = (
                    o_t[:, c0:c0 + qs].T.astype(BF16))

    @pl.when(i == 0)
    def _():
        begin()
        softmax_diag(scores_diag(0))

    @pl.when(jnp.logical_and(i > 0, i < nq))
    def _():
        begin()
        s = scores(0)
        drain_previous()
        softmax(s)

        def body(j, carry):
            s = scores(j)
            values(j - 1, slot)
            softmax(s)
            return carry

        lax.fori_loop(1, i, body, 0)
        parts = scores_diag(i)
        values(i - 1, slot)
        softmax_diag(parts)

    @pl.when(i == nq)
    def _():
        drain_previous()


def _attention_prompt(q, k, vt, t):
    nseq, _, _, qs, _ = q.shape
    length = k.shape[1]
    rows = MLA_HEADS * t
    nq = length // t
    assert vt.shape[-1] == t and t % qs == 0 and qs % CHUNK == 0
    return pl.pallas_call(
        functools.partial(_attn_kernel, t=t, qs=qs),
        grid=(nseq, nq + 1),
        in_specs=[pl.BlockSpec((1, t // qs, MLA_HEADS, qs, QK_DIM),
                               lambda b, i: (b, jnp.minimum(i, nq - 1), 0, 0, 0)),
                  pl.BlockSpec((1, length, QK_DIM), lambda b, i: (b, 0, 0)),
                  pl.BlockSpec((1, nq, VT_ROWS, t), lambda b, i: (b, 0, 0, 0))],
        out_specs=pl.BlockSpec((1, t, MLA_HEADS * KV_RANK), lambda b, i: (b, jnp.maximum(i - 1, 0), 0)),
        out_shape=jax.ShapeDtypeStruct((nseq, length, MLA_HEADS * KV_RANK), BF16),
        scratch_shapes=[pltpu.VMEM((1, rows), F32), pltpu.VMEM((1, rows), F32),
                        pltpu.VMEM((t, rows), BF16), pltpu.VMEM((2, VT_ROWS, rows), F32)],
        compiler_params=_params("parallel", "arbitrary"),
        name="mla_attention_prompt",
    )(q, k, vt)


def _chunk_mask(s, t, q_start, k_start):
    n = s.shape[1]
    qpos = q_start + lax.broadcasted_iota(jnp.int32, (t, n), 0)
    kpos = k_start + lax.broadcasted_iota(jnp.int32, (t, n), 1)
    allowed = (kpos // CHUNK) <= (qpos // CHUNK)
    return jnp.where(allowed[None], s.reshape(MLA_HEADS, t, n), NEG_INF).reshape(s.shape)


def _attn_sample_kernel(q_ref, ckv_ref, ckpe_ref, kn_ref, o_ref, *, t, past, mask_cache, mask_new):
    rows = MLA_HEADS * t
    q = q_ref[0].reshape(rows, QK_DIM)
    ckv = ckv_ref[0].astype(BF16)
    ckpe = ckpe_ref[0].astype(BF16)
    kn = kn_ref[0]
    s_c = _dot_nt(q[:, :KV_RANK], ckv) + _dot_nt(q[:, KV_RANK:], ckpe)
    s_n = _dot_nt(q, kn)
    if mask_cache:
        s_c = _chunk_mask(s_c, t, past, 0)
    if mask_new:
        s_n = _chunk_mask(s_n, t, past, past)
    m = jnp.maximum(jnp.max(s_c, axis=1, keepdims=True), jnp.max(s_n, axis=1, keepdims=True))
    p_c = jnp.exp2(s_c - m)
    p_n = jnp.exp2(s_n - m)
    denom = jnp.sum(p_c, axis=1, keepdims=True) + jnp.sum(p_n, axis=1, keepdims=True)
    o = _dot(p_c.astype(BF16), ckv) + _dot(p_n.astype(BF16), kn[:, :KV_RANK])
    o = (o / denom).astype(BF16)
    for hd in range(MLA_HEADS):
        o_ref[0, :, hd * KV_RANK:(hd + 1) * KV_RANK] = o[hd * t:(hd + 1) * t]


def _attention_sample(q, cache_kv, cache_kpe, k_new, l):
    nb, _, t, _ = q.shape
    past = cache_kv.shape[2]
    q_chunk = (past + np.arange(t)) // CHUNK
    k_chunk = np.arange(past + t) // CHUNK
    allowed = k_chunk[None, :] <= q_chunk[:, None]
    mask_cache = not allowed[:, :past].all()
    mask_new = not allowed[:, past:].all()
    return pl.pallas_call(
        functools.partial(_attn_sample_kernel, t=t, past=past, mask_cache=mask_cache, mask_new=mask_new),
        grid=(nb,),
        in_specs=[pl.BlockSpec((1, MLA_HEADS, t, QK_DIM), lambda b: (b, 0, 0, 0)),
                  pl.BlockSpec((None, 1, past, KV_RANK), lambda b: (l, b, 0, 0)),
                  pl.BlockSpec((None, 1, past, MLA_ROPE), lambda b: (l, b, 0, 0)),
                  pl.BlockSpec((1, t, QK_DIM), lambda b: (b, 0, 0))],
        out_specs=pl.BlockSpec((1, t, MLA_HEADS * KV_RANK), lambda b: (b, 0, 0)),
        out_shape=jax.ShapeDtypeStruct((nb, t, MLA_HEADS * KV_RANK), BF16),
        compiler_params=_params("parallel"),
        name="mla_attention_sample",
    )(q, cache_kv, cache_kpe, k_new)


def _ret_kernel(rq_ref, rk_ref, rv_ref, s0_ref, dec_ref, cross_ref, kdec_ref, gt_ref,
                o_ref, sout_ref, st_scr):
    c = pl.program_id(1)

    @pl.when(c == 0)
    def _():
        st_scr[...] = s0_ref[0]

    heads = range(RET_HEADS)
    ks = [slice(hd * RET_DK, (hd + 1) * RET_DK) for hd in heads]
    vs = [slice(hd * RET_DV, (hd + 1) * RET_DV) for hd in heads]
    qs = [rq_ref[0, :, ks[hd]] for hd in heads]
    kk = [rk_ref[0, :, ks[hd]] for hd in heads]
    vv = [rv_ref[0, :, vs[hd]] for hd in heads]
    st = [st_scr[hd] for hd in heads]
    scores = [_dot_nt(qs[hd], kk[hd]) for hd in heads]
    cross = [_dot(qs[hd], st[hd].astype(BF16)) for hd in heads]
    for hd in heads:
        k_dec = (kk[hd].astype(F32) * kdec_ref[hd]).astype(BF16)
        st_scr[hd] = gt_ref[hd] * st[hd] + _dot_tn(k_dec, vv[hd])
    for hd in heads:
        sc = (scores[hd] * dec_ref[hd]).astype(BF16)
        o_ref[0, :, vs[hd]] = (_dot(sc, vv[hd]) + cross[hd] * cross_ref[hd]).astype(BF16)

    @pl.when(c == pl.num_programs(1) - 1)
    def _():
        sout_ref[0] = st_scr[...]


def _ret_tables(t):
    f32 = np.float32
    log_g = np.log1p(-np.exp2(-5.0 - np.arange(RET_HEADS, dtype=f32))).astype(f32)
    n = np.arange(t, dtype=f32)
    diff = n[:, None] - n[None, :]
    dec = np.where(diff >= 0.0, np.exp(np.maximum(diff, 0.0)[None] * log_g[:, None, None]), 0.0)
    cross = np.exp((n + 1.0)[None, :] * log_g[:, None])
    kdec = np.exp((t - 1.0 - n)[None, :] * log_g[:, None])
    gt = np.exp(f32(t) * log_g)
    cross = np.broadcast_to(cross[:, :, None], (RET_HEADS, t, RET_DV))
    kdec = np.broadcast_to(kdec[:, :, None], (RET_HEADS, t, RET_DK))
    gt = np.broadcast_to(gt[:, None, None], (RET_HEADS, RET_DK, RET_DV))
    return tuple(jnp.asarray(a, F32) for a in (dec, cross, kdec, gt))


def _retention(rq, rk, rv, state0, tabs, t):
    nseq, length, _ = rq.shape
    tile = pl.BlockSpec((1, t, RET_W), lambda b, c: (b, c, 0))
    st = pl.BlockSpec((1, RET_HEADS, RET_DK, RET_DV), lambda b, c: (b, 0, 0, 0))
    const = lambda a: pl.BlockSpec(a.shape, lambda b, c: (0, 0, 0))
    return pl.pallas_call(
        _ret_kernel,
        grid=(nseq, length // t),
        in_specs=[tile, tile, tile, st] + [const(a) for a in tabs],
        out_specs=[tile, st],
        out_shape=[jax.ShapeDtypeStruct((nseq, length, RET_W), BF16),
                   jax.ShapeDtypeStruct(state0.shape, F32)],
        scratch_shapes=[pltpu.VMEM((RET_HEADS, RET_DK, RET_DV), F32)],
        compiler_params=_params("parallel", "arbitrary"),
        name="retention",
    )(rq, rk, rv, state0, *tabs)


def _out_kernel(x_ref, ol_ref, or_ref, rg_ref, gt_ref, gpost_ref, wuv_ref, wo_ref, grp_ref, o_ref):
    n_mla = MLA_HEADS * MLA_V
    o_mla = _dot(ol_ref[0], wuv_ref[...]).astype(BF16)
    mix = _dot(o_mla, wo_ref[:n_mla, :])
    o = or_ref[0].astype(F32)
    sq = o * o
    hi = sq.astype(BF16)
    lo = (sq - hi.astype(F32)).astype(BF16)
    ms = _dot(hi, grp_ref[...]) + _dot(lo, grp_ref[...])
    o_ret = (o * lax.rsqrt(ms + EPS) * _silu(rg_ref[0].astype(F32))).astype(BF16)
    mix = mix + _dot(o_ret, wo_ref[n_mla:, :])
    o_ref[0] = x_ref[0] + gt_ref[0] * (_rms(mix) * gpost_ref[...])


def _mixer_out(x, o_lat, o_ret, rg, gate, g_post, w_uv, w_out, l, tm):
    nseq, length, d = x.shape
    tile = lambda n: pl.BlockSpec((1, tm, n), lambda b, s: (b, s, 0))
    wl = lambda r, c: _resident((None, r, c), lambda b, s: (l, 0, 0))
    head = np.arange(RET_W) // RET_DV
    group_mean = jnp.asarray((head[:, None] == head[None, :]).astype(np.float32) / RET_DV, BF16)
    return pl.pallas_call(
        _out_kernel,
        grid=(nseq, length // tm),
        in_specs=[tile(d), tile(o_lat.shape[-1]), tile(RET_W), tile(RET_W), _mod_spec(gate, tm),
                  pl.BlockSpec((1, d), lambda b, s: (0, 0)),
                  wl(*w_uv.shape[1:]), wl(*w_out.shape[1:]),
                  _resident((RET_W, RET_W), lambda b, s: (0, 0))],
        out_specs=tile(d),
        out_shape=jax.ShapeDtypeStruct(x.shape, F32),
        compiler_params=_params("parallel", "parallel"),
        name="mixer_out",
    )(x, o_lat, o_ret, rg, gate, g_post, w_uv, w_out, group_mean)


def _prep_w_in(w_in):
    c0 = Q_RANK + KV_RANK
    half = MLA_ROPE // 2
    kpe = w_in[..., c0:c0 + MLA_ROPE]
    kpe_rot = jnp.concatenate([-kpe[..., half:], kpe[..., :half]], axis=-1)
    pad = jnp.zeros(w_in.shape[:-1] + (KPE_BLOCK - 2 * MLA_ROPE,), w_in.dtype)
    return jnp.concatenate([w_in[..., :c0], kpe, kpe_rot, pad, w_in[..., c0 + MLA_ROPE:]],
                           axis=-1).astype(BF16)


def _absorb_kernel(uq_ref, uk_ref, o_ref):
    o_ref[0, 0] = lax.dot_general(uq_ref[0, 0], uk_ref[0, 0], (((1,), (1,)), ((), ())),
                                  preferred_element_type=F32, precision=lax.Precision.HIGHEST)


def _prep_w_uq(w_uq, w_uk):
    depth = w_uq.shape[0]
    half = MLA_ROPE // 2
    w = w_uq.reshape(depth, Q_RANK, MLA_HEADS, MLA_NOPE + MLA_ROPE)
    nope = w[..., :MLA_NOPE].transpose(0, 2, 1, 3)
    uk = w_uk.reshape(depth, KV_RANK, MLA_HEADS, MLA_NOPE).transpose(0, 2, 1, 3)
    absorbed = pl.pallas_call(
        _absorb_kernel,
        grid=(depth, MLA_HEADS),
        in_specs=[pl.BlockSpec((1, 1, Q_RANK, MLA_NOPE), lambda l, h: (l, h, 0, 0)),
                  pl.BlockSpec((1, 1, KV_RANK, MLA_NOPE), lambda l, h: (l, h, 0, 0))],
        out_specs=pl.BlockSpec((1, 1, Q_RANK, KV_RANK), lambda l, h: (l, h, 0, 0)),
        out_shape=jax.ShapeDtypeStruct((depth, MLA_HEADS, Q_RANK, KV_RANK), F32),
        compiler_params=_params("parallel", "parallel"),
        name="absorb_w_uk",
    )(nope, uk)
    absorbed = absorbed.transpose(0, 2, 1, 3).reshape(depth, Q_RANK, MLA_HEADS * KV_RANK)
    pe = w[..., MLA_NOPE:]
    rot = jnp.concatenate([-pe[..., half:], pe[..., :half]], axis=-1)
    flat = lambda a: a.reshape(depth, Q_RANK, MLA_HEADS * MLA_ROPE)
    return jnp.concatenate([absorbed, flat(pe), flat(rot)], axis=-1).astype(BF16)


def _prep_w_uv(w_uv):
    depth = w_uv.shape[0]
    w = w_uv.reshape(depth, KV_RANK, MLA_HEADS, MLA_V)
    eye = jnp.eye(MLA_HEADS, dtype=w_uv.dtype)
    bd = jnp.einsum('lrhv,hg->lhrgv', w, eye)
    return bd.reshape(depth, MLA_HEADS * KV_RANK, MLA_HEADS * MLA_V).astype(BF16)


def _pick_tile(length, target):
    t = min(length, target)
    while length % t:
        t //= 2
    return t


def kernel(x_prompt, x_sample, cache_kv_latent, cache_k_rope, state_ret, c_prompt, c_sample,
           ada_w, ada_b, norm_pre, norm_post, ffn_w1, ffn_w3, ffn_w2,
           w_in, q_norm, kv_norm, w_uq, w_uk, w_uv, w_out):
    depth = ada_w.shape[0]
    nb_p, seq, d = x_prompt.shape
    nb_s, dec_seq, _ = x_sample.shape
    past = cache_kv_latent.shape[2]
    n_tok_s = nb_s * dec_seq

    w1, w3, w2 = ffn_w1.astype(BF16), ffn_w3.astype(BF16), ffn_w2.astype(BF16)
    w_in_b = _prep_w_in(w_in)
    w_uq_b = _prep_w_uq(w_uq, w_uk)
    w_uv_b = _prep_w_uv(w_uv)
    w_out_b = w_out.astype(BF16)

    mods = _ada_mods(jnp.concatenate([c_prompt, c_sample], axis=0), ada_w, ada_b)

    tm_p = _pick_tile(seq, 512)
    tm_ffn = _pick_tile(seq, 2 * FFN_SUB_ROWS)
    t_ret = _pick_tile(seq, 256)
    tabs_p = _rope_tables(np.arange(seq))
    tabs_s = _rope_tables(np.tile(past + np.arange(dec_seq), nb_s))
    ret_tabs_p = _ret_tables(t_ret)
    ret_tabs_s = _ret_tables(dec_seq)
    zero_state = jnp.zeros((nb_p, RET_HEADS, RET_DK, RET_DV), F32)

    def mod_p(l, r):
        return mods[l, :nb_p, r][:, None, :]

    def mod_s(l, r):
        return jnp.repeat(mods[l, nb_p:, r], dec_seq, axis=0)[None]

    y_p = x_prompt
    y_s = x_sample.reshape(1, n_tok_s, d)
    lat_p, kpe_p, ret_p, lat_s, kpe_s, ret_s = [], [], [], [], [], []

    for l in range(depth):
        gp = lambda j: norm_pre[l, j][None, :]
        go = lambda j: norm_post[l, j][None, :]
        qn, kvn = q_norm[l][None, :], kv_norm[l][None, :]

        y_p = _ffn(y_p, mod_p(l, 0), mod_p(l, 1), mod_p(l, 2), gp(0), go(0), w1, w3, w2, l, 0, tm_ffn)
        q, k, vt, kv, kpe, rq, rk, rv, rg = _proj(y_p, mod_p(l, 3), mod_p(l, 4), gp(1), w_in_b, qn, kvn,
                                                  w_uq_b, tabs_p, l, tm_p, Q_SUB)
        o_lat = _attention_prompt(q, k, vt, tm_p)
        o_ret, r_fin = _retention(rq, rk, rv, zero_state, ret_tabs_p, t_ret)
        y_p = _mixer_out(y_p, o_lat, o_ret, rg, mod_p(l, 5), go(1), w_uv_b, w_out_b, l, tm_p)
        y_p = _ffn(y_p, mod_p(l, 6), mod_p(l, 7), mod_p(l, 8), gp(2), go(2), w1, w3, w2, l, 1, tm_ffn)
        lat_p.append(kv); kpe_p.append(kpe); ret_p.append(r_fin)

        y_s = _ffn(y_s, mod_s(l, 0), mod_s(l, 1), mod_s(l, 2), gp(0), go(0), w1, w3, w2, l, 0, n_tok_s)
        q, k, _, kv, kpe, rq, rk, rv, rg = _proj(y_s, mod_s(l, 3), mod_s(l, 4), gp(1), w_in_b, qn, kvn,
                                                 w_uq_b, tabs_s, l, n_tok_s, dec_seq)
        per_seq = lambda a: a.reshape(nb_s, dec_seq, a.shape[-1])
        q = q.reshape(nb_s, MLA_HEADS, dec_seq, QK_DIM)
        o_lat = _attention_sample(q, cache_kv_latent, cache_k_rope, per_seq(k), l)
        o_ret, r_new = _retention(per_seq(rq), per_seq(rk), per_seq(rv),
                                  state_ret[l], ret_tabs_s, dec_seq)
        y_s = _mixer_out(y_s, o_lat.reshape(1, n_tok_s, -1), o_ret.reshape(1, n_tok_s, -1), rg,
                         mod_s(l, 5), go(1), w_uv_b, w_out_b, l, n_tok_s)
        y_s = _ffn(y_s, mod_s(l, 6), mod_s(l, 7), mod_s(l, 8), gp(2), go(2), w1, w3, w2, l, 1, n_tok_s)
        lat_s.append(per_seq(kv)); kpe_s.append(per_seq(kpe)); ret_s.append(r_new)

    return (y_p, y_s.reshape(nb_s, dec_seq, d),
            jnp.stack(lat_p), jnp.stack(kpe_p), jnp.stack(ret_p),
            jnp.stack(lat_s), jnp.stack(kpe_s), jnp.stack(ret_s))
```

```python
import functools

import jax
import jax.numpy as jnp
import numpy as np
from jax import lax
from jax.experimental import pallas as pl
from jax.experimental.pallas import tpu as pltpu

CHUNK = 64
N_SUB = 3
MLA_HEADS = 8
MLA_NOPE = 64
MLA_ROPE = 32
MLA_V = 64
Q_RANK = 256
KV_RANK = 128
RET_HEADS = 8
RET_DK = 64
RET_DV = 64
ROPE_BASE = 10000.0
EPS = 1e-6
MLA_SCALE = (MLA_NOPE + MLA_ROPE) ** -0.5
Q_SCALE = MLA_SCALE * float(np.log2(np.e))
MAX_GROUPS = 8
MXU_DIM = 256
FFN_CHUNK = 3 * MXU_DIM
FFN_SUB_ROWS = 512
Q_SUB = 128
RET_SCALE = RET_DK ** -0.5
NEG_INF = -1e30

QK_DIM = KV_RANK + MLA_ROPE
RET_W = RET_HEADS * RET_DK
LANES = 128
KPE_BLOCK = LANES
BF16_SUBLANES = 16
VT_ROWS = KV_RANK + BF16_SUBLANES
W_IN_COLS = Q_RANK + KV_RANK + KPE_BLOCK + 4 * RET_W

F32 = jnp.float32
BF16 = jnp.bfloat16

VMEM_LIMIT = 56 * 1024 * 1024


def _params(*sem):
    return pltpu.CompilerParams(dimension_semantics=sem, vmem_limit_bytes=VMEM_LIMIT)


def _resident(block_shape, index_map):
    return pl.BlockSpec(block_shape, index_map, pipeline_mode=pl.Buffered(1))


def _rms(x):
    return x * lax.rsqrt(jnp.mean(x * x, axis=-1, keepdims=True) + EPS)


def _silu(x):
    return x * jax.nn.sigmoid(x)


def _dot(a, b):
    return jnp.dot(a, b, preferred_element_type=F32)


def _dot_nt(a, b):
    return lax.dot_general(a, b, (((1,), (1,)), ((), ())), preferred_element_type=F32)


def _dot_tn(a, b):
    return lax.dot_general(a, b, (((0,), (0,)), ((), ())), preferred_element_type=F32)


def _ada_kernel(c_ref, w_ref, b_ref, o_ref):
    sc = _silu(c_ref[...])
    o_ref[0] = jnp.dot(sc, w_ref[0], preferred_element_type=F32,
                       precision=lax.Precision.HIGHEST) + b_ref[0]


def _ada_mods(c_all, ada_w, ada_b):
    depth, d, cols = ada_w.shape
    nb = c_all.shape[0]
    n_col = cols // d
    out = pl.pallas_call(
        _ada_kernel,
        grid=(depth, n_col),
        in_specs=[
            pl.BlockSpec((nb, d), lambda l, j: (0, 0)),
            pl.BlockSpec((1, d, d), lambda l, j: (l, 0, j)),
            pl.BlockSpec((1, 1, d), lambda l, j: (l, 0, j)),
        ],
        out_specs=pl.BlockSpec((1, nb, d), lambda l, j: (l, 0, j)),
        out_shape=jax.ShapeDtypeStruct((depth, nb, cols), F32),
        compiler_params=_params("parallel", "parallel"),
        name="ada_mods",
    )(c_all, ada_w, ada_b.reshape(depth, 1, cols))
    return out.reshape(depth, nb, n_col, d)


def _ffn_kernel(x_ref, sh_ref, sc_ref, gt_ref, gpre_ref, gpost_ref, w1_ref, w3_ref, w2_ref, o_ref,
                *, coef, f_chunk, sub):
    d_ff = w1_ref.shape[1]
    starts = list(range(0, x_ref.shape[1], sub))
    rows = lambda ref, r0: ref[0] if ref.shape[1] == 1 else ref[0, r0:r0 + sub]
    hs = [(_rms(x_ref[0, r0:r0 + sub]) * gpre_ref[...] * (1.0 + rows(sc_ref, r0))
           + rows(sh_ref, r0)).astype(BF16) for r0 in starts]
    outs = [None] * len(starts)
    pending = [None] * len(starts)

    def down(i, c0, c1):
        part = _dot(pending[i], w2_ref[c0:c1, :])
        outs[i] = part if outs[i] is None else outs[i] + part

    prev = None
    for c0 in range(0, d_ff, f_chunk):
        c1 = min(c0 + f_chunk, d_ff)
        gated = []
        for i, h in enumerate(hs):
            a = _dot(h, w1_ref[:, c0:c1])
            b = _dot(h, w3_ref[:, c0:c1])
            gated.append((_silu(a) * b).astype(BF16))
        if prev is not None:
            for i in range(len(hs)):
                down(i, *prev)
        pending, prev = gated, (c0, c1)
    for i, r0 in enumerate(starts):
        down(i, *prev)
        o_ref[0, r0:r0 + sub] = (x_ref[0, r0:r0 + sub]
                                 + (coef * rows(gt_ref, r0)) * (_rms(outs[i]) * gpost_ref[...]))


def _mod_spec(mod, tm):
    if mod.shape[1] == 1:
        return pl.BlockSpec((1, 1, mod.shape[2]), lambda b, s: (b, 0, 0))
    return pl.BlockSpec((1, tm, mod.shape[2]), lambda b, s: (b, s, 0))


def _ffn(x, shift, scale, gate, g_pre, g_post, w1, w3, w2, l, k, tm):
    nseq, length, d = x.shape
    d_ff = w1.shape[-1]
    f_chunk = min(d_ff, FFN_CHUNK)
    w_up = _resident((None, None, d, d_ff), lambda b, s: (l, k, 0, 0))
    w_dn = _resident((None, None, d_ff, d), lambda b, s: (l, k, 0, 0))
    vec = pl.BlockSpec((1, d), lambda b, s: (0, 0))
    tile = pl.BlockSpec((1, tm, d), lambda b, s: (b, s, 0))
    return pl.pallas_call(
        functools.partial(_ffn_kernel, coef=0.5, f_chunk=f_chunk, sub=min(tm, FFN_SUB_ROWS)),
        grid=(nseq, length // tm),
        in_specs=[tile, _mod_spec(shift, tm), _mod_spec(scale, tm), _mod_spec(gate, tm),
                  vec, vec, w_up, w_up, w_dn],
        out_specs=tile,
        out_shape=jax.ShapeDtypeStruct(x.shape, F32),
        compiler_params=_params("parallel", "parallel"),
        name="ffn",
    )(x, shift, scale, gate, g_pre, g_post, w1, w3, w2)


def _proj_kernel(x_ref, sh_ref, sc_ref, gpre_ref, win_ref, qn_ref, kvn_ref, wuq_ref,
                 cq_ref, sq_ref, ck_ref, sk_ref, rc_ref, rsa_ref, rsb_ref, *rest):
    q_out, k_out, vt_out, kv_out, kpe_out, rq_out, rk_out, rv_out, rg_out = rest[-9:]
    x = x_ref[0]
    h = (_rms(x) * gpre_ref[...] * (1.0 + sc_ref[0]) + sh_ref[0]).astype(BF16)
    z = _dot(h, win_ref[...])

    q_lat = (_rms(z[:, :Q_RANK]) * qn_ref[...]).astype(BF16)
    kv_lat = _rms(z[:, Q_RANK:Q_RANK + KV_RANK]) * kvn_ref[...]
    c0 = Q_RANK + KV_RANK
    k_pe = (z[:, c0:c0 + MLA_ROPE] * ck_ref[...]
            + z[:, c0 + MLA_ROPE:c0 + 2 * MLA_ROPE] * sk_ref[...])
    kv_out[0] = kv_lat
    kpe_out[0] = k_pe
    k_out[0] = jnp.concatenate([kv_lat, k_pe], axis=-1).astype(BF16)
    ones = jnp.ones((VT_ROWS - KV_RANK, kv_lat.shape[0]), F32)
    vt_out[0, 0] = jnp.concatenate([kv_lat.T, ones], axis=0).astype(BF16)

    n_abs = MLA_HEADS * KV_RANK
    n_pe = MLA_HEADS * MLA_ROPE
    qq = _dot(q_lat, wuq_ref[...])
    q_abs = qq[:, :n_abs]
    q_pe = qq[:, n_abs:n_abs + n_pe] * cq_ref[...] + qq[:, n_abs + n_pe:] * sq_ref[...]
    for hd in range(MLA_HEADS):
        qh = jnp.concatenate([q_abs[:, hd * KV_RANK:(hd + 1) * KV_RANK],
                              q_pe[:, hd * MLA_ROPE:(hd + 1) * MLA_ROPE]], axis=-1)
        qh = (qh * Q_SCALE).astype(BF16)
        qsub = q_out.shape[3]
        for r in range(q_out.shape[1]):
            q_out[0, r, hd] = qh[r * qsub:(r + 1) * qsub]

    r0 = c0 + KPE_BLOCK
    rc, rsa, rsb = rc_ref[...], rsa_ref[...], rsb_ref[...]
    for off, out, scale in ((r0, rq_out, 1.0), (r0 + RET_W, rk_out, RET_SCALE)):
        for c in range(RET_W // LANES):
            xc = z[:, off + c * LANES:off + (c + 1) * LANES]
            y = (xc * rc + pltpu.roll(xc, LANES - RET_DK // 2, 1) * rsa
                 + pltpu.roll(xc, RET_DK // 2, 1) * rsb)
            out[0, :, c * LANES:(c + 1) * LANES] = (y * scale).astype(BF16)
    rv_out[0] = z[:, r0 + 2 * RET_W:r0 + 3 * RET_W].astype(BF16)
    rg_out[0] = z[:, r0 + 3 * RET_W:r0 + 4 * RET_W].astype(BF16)


def _proj(x, shift, scale, g_pre, w_in, q_norm, kv_norm, w_uq, tabs, l, tm, qsub, caches):
    nseq, length, d = x.shape
    depth = w_in.shape[0]
    grid = (nseq, length // tm)
    vec = lambda n: pl.BlockSpec((1, n), lambda b, s: (0, 0))
    tile = lambda n: pl.BlockSpec((1, tm, n), lambda b, s: (b, s, 0))
    tab = lambda n: pl.BlockSpec((tm, n), lambda b, s: (s, 0))
    wl = lambda r, c: _resident((None, r, c), lambda b, s: (l, 0, 0))
    layer_tile = lambda n: pl.BlockSpec((None, 1, tm, n), lambda b, s: (l, b, s, 0))
    bf = lambda n: jax.ShapeDtypeStruct((nseq, length, n), BF16)
    in_specs = [tile(d), _mod_spec(shift, tm), _mod_spec(scale, tm), vec(d),
                wl(d, W_IN_COLS), vec(Q_RANK), vec(KV_RANK),
                wl(Q_RANK, w_uq.shape[-1]),
                tab(MLA_HEADS * MLA_ROPE), tab(MLA_HEADS * MLA_ROPE), tab(MLA_ROPE), tab(MLA_ROPE),
                tab(LANES), tab(LANES), tab(LANES)]
    args = [x, shift, scale, g_pre, w_in, q_norm, kv_norm, w_uq, *tabs]
    aliases = {}
    if caches is not None:
        aliases = {len(args): 3, len(args) + 1: 4}
        in_specs += [pl.BlockSpec(memory_space=pl.ANY)] * 2
        args += list(caches)
    return pl.pallas_call(
        _proj_kernel,
        grid=grid,
        in_specs=in_specs,
        out_specs=[pl.BlockSpec((1, tm // qsub, MLA_HEADS, qsub, QK_DIM), lambda b, s: (b, s, 0, 0, 0)),
                   tile(QK_DIM),
                   pl.BlockSpec((1, 1, VT_ROWS, tm), lambda b, s: (b, s, 0, 0)),
                   layer_tile(KV_RANK), layer_tile(MLA_ROPE),
                   tile(RET_W), tile(RET_W), tile(RET_W), tile(RET_W)],
        out_shape=[jax.ShapeDtypeStruct((nseq, length // qsub, MLA_HEADS, qsub, QK_DIM), BF16),
                   bf(QK_DIM),
                   jax.ShapeDtypeStruct((nseq, length // tm, VT_ROWS, tm), BF16),
                   jax.ShapeDtypeStruct((depth, nseq, length, KV_RANK), F32),
                   jax.ShapeDtypeStruct((depth, nseq, length, MLA_ROPE), F32),
                   bf(RET_W), bf(RET_W), bf(RET_W), bf(RET_W)],
        input_output_aliases=aliases,
        compiler_params=_params("parallel", "parallel"),
        name="mixer_proj",
    )(*args)


def _rope_tables(pos):
    pos = np.asarray(pos, np.float32)[:, None]

    def cs(half):
        inv = (ROPE_BASE ** (-np.arange(half, dtype=np.float32) / half)).astype(np.float32)
        ang = pos * inv[None, :]
        return np.cos(ang), np.sin(ang)

    c16, s16 = cs(MLA_ROPE // 2)
    ck = np.concatenate([c16, c16], axis=-1)
    sk = np.concatenate([s16, s16], axis=-1)
    cq = np.tile(ck, (1, MLA_HEADS))
    sq = np.tile(sk, (1, MLA_HEADS))
    c32, s32 = cs(RET_DK // 2)
    zero = np.zeros_like(s32)
    reps = LANES // RET_DK
    rc = np.tile(np.concatenate([c32, c32], axis=-1), (1, reps))
    rsa = np.tile(np.concatenate([-s32, zero], axis=-1), (1, reps))
    rsb = np.tile(np.concatenate([zero, s32], axis=-1), (1, reps))
    return tuple(jnp.asarray(t, F32) for t in (cq, sq, ck, sk, rc, rsa, rsb))


def _attn_kernel(q_ref, k_ref, vt_ref, o_ref, m_scr, alpha_scr, p_scr, acc_scr, *, t, qs):
    i = pl.program_id(1)
    nq = pl.num_programs(1) - 1
    n_sub = t // qs
    cb = MLA_HEADS * qs
    rows = n_sub * cb
    subs = [(r, slice(r * cb, (r + 1) * cb)) for r in range(n_sub)]
    slot = i % 2
    q = q_ref[0].reshape(rows, QK_DIM)

    def k_tile(j):
        return k_ref[0, pl.ds(pl.multiple_of(j * t, t), t), :]

    def scores(j):
        return _dot_nt(k_tile(j), q)

    def scores_diag(j):
        kt = k_tile(j)
        return [_dot_nt(kt[:(r + 1) * qs], q[cs]) for r, cs in subs]

    def update(cs, sc):
        n = sc.shape[0]
        m_prev = m_scr[:, cs]
        m_part = jnp.max(sc.reshape(MAX_GROUPS, n // MAX_GROUPS, cb), axis=0)
        m_new = jnp.maximum(m_prev, jnp.max(m_part, axis=0, keepdims=True))
        alpha_scr[:, cs] = jnp.exp2(m_prev - m_new)
        p_scr[:n, cs] = jnp.exp2((sc - m_new).astype(BF16))
        m_scr[:, cs] = m_new

    def softmax(s):
        for _, cs in subs:
            update(cs, s[:, cs])

    def softmax_diag(parts):
        kc = lax.broadcasted_iota(jnp.int32, (qs, qs), 0) // CHUNK
        qc = lax.broadcasted_iota(jnp.int32, (qs, qs), 1) // CHUNK
        allowed = jnp.concatenate([kc <= qc] * MLA_HEADS, axis=1)
        for (r, cs), sc in zip(subs, parts):
            last = jnp.where(allowed, sc[r * qs:], NEG_INF)
            update(cs, last if r == 0 else jnp.concatenate([sc[:r * qs], last], axis=0))

    def values(j, acc_slot):
        for _, cs in subs:
            acc_scr[acc_slot, :, cs] = (acc_scr[acc_slot, :, cs] * alpha_scr[:, cs]
                                        + _dot(vt_ref[0, j], p_scr[:, cs]))

    def values_diag(j, acc_slot):
        for r, cs in subs:
            n = (r + 1) * qs
            acc_scr[acc_slot, :, cs] = (acc_scr[acc_slot, :, cs] * alpha_scr[:, cs]
                                        + _dot(vt_ref[0, j, :, :n], p_scr[:n, cs]))

    def begin():
        m_scr[...] = jnp.full(m_scr.shape, NEG_INF, F32)
        acc_scr[slot] = jnp.zeros(acc_scr.shape[1:], F32)

    def drain_previous():
        prev = 1 - slot
        values_diag(i - 1, prev)
        acc = acc_scr[prev]
        o_t = acc[:KV_RANK] * (1.0 / acc[KV_RANK:KV_RANK + 1])
        for r in range(n_sub):
            for hd in range(MLA_HEADS):
                c0 = (r * MLA_HEADS + hd) * qs
                o_ref[0, r * qs:(r + 1) * qs, hd * KV_RANK:(hd + 1) * KV_RANK] = (
                    o_t[:, c0:c0 + qs].T.astype(BF16))

    @pl.when(i == 0)
    def _():
        begin()
        softmax_diag(scores_diag(0))

    @pl.when(jnp.logical_and(i > 0, i < nq))
    def _():
        begin()
        s = scores(0)
        drain_previous()
        softmax(s)

        def body(j, carry):
            s = scores(j)
            values(j - 1, slot)
            softmax(s)
            return carry

        lax.fori_loop(1, i, body, 0)
        parts = scores_diag(i)
        values(i - 1, slot)
        softmax_diag(parts)

    @pl.when(i == nq)
    def _():
        drain_previous()


def _attention_prompt(q, k, vt, t):
    nseq, _, _, qs, _ = q.shape
    length = k.shape[1]
    rows = MLA_HEADS * t
    nq = length // t
    assert vt.shape[-1] == t and t % qs == 0 and qs % CHUNK == 0
    return pl.pallas_call(
        functools.partial(_attn_kernel, t=t, qs=qs),
        grid=(nseq, nq + 1),
        in_specs=[pl.BlockSpec((1, t // qs, MLA_HEADS, qs, QK_DIM),
                               lambda b, i: (b, jnp.minimum(i, nq - 1), 0, 0, 0)),
                  pl.BlockSpec((1, length, QK_DIM), lambda b, i: (b, 0, 0)),
                  pl.BlockSpec((1, nq, VT_ROWS, t), lambda b, i: (b, 0, 0, 0))],
        out_specs=pl.BlockSpec((1, t, MLA_HEADS * KV_RANK), lambda b, i: (b, jnp.maximum(i - 1, 0), 0)),
        out_shape=jax.ShapeDtypeStruct((nseq, length, MLA_HEADS * KV_RANK), BF16),
        scratch_shapes=[pltpu.VMEM((1, rows), F32), pltpu.VMEM((1, rows), F32),
                        pltpu.VMEM((t, rows), BF16), pltpu.VMEM((2, VT_ROWS, rows), F32)],
        compiler_params=_params("parallel", "arbitrary"),
        name="mla_attention_prompt",
    )(q, k, vt)


def _chunk_mask(s, t, q_start, k_start):
    n = s.shape[1]
    qpos = q_start + lax.broadcasted_iota(jnp.int32, (t, n), 0)
    kpos = k_start + lax.broadcasted_iota(jnp.int32, (t, n), 1)
    allowed = (kpos // CHUNK) <= (qpos // CHUNK)
    return jnp.where(allowed[None], s.reshape(MLA_HEADS, t, n), NEG_INF).reshape(s.shape)


def _attn_sample_kernel(q_ref, ckv_ref, ckpe_ref, kn_ref, o_ref, *, t, past, mask_cache, mask_new):
    rows = MLA_HEADS * t
    q = q_ref[0].reshape(rows, QK_DIM)
    ckv = ckv_ref[0].astype(BF16)
    ckpe = ckpe_ref[0].astype(BF16)
    kn = kn_ref[0]
    s_c = _dot_nt(q[:, :KV_RANK], ckv) + _dot_nt(q[:, KV_RANK:], ckpe)
    s_n = _dot_nt(q, kn)
    if mask_cache:
        s_c = _chunk_mask(s_c, t, past, 0)
    if mask_new:
        s_n = _chunk_mask(s_n, t, past, past)
    m = jnp.maximum(jnp.max(s_c, axis=1, keepdims=True), jnp.max(s_n, axis=1, keepdims=True))
    p_c = jnp.exp2(s_c - m)
    p_n = jnp.exp2(s_n - m)
    denom = jnp.sum(p_c, axis=1, keepdims=True) + jnp.sum(p_n, axis=1, keepdims=True)
    o = _dot(p_c.astype(BF16), ckv) + _dot(p_n.astype(BF16), kn[:, :KV_RANK])
    o = (o / denom).astype(BF16)
    for hd in range(MLA_HEADS):
        o_ref[0, :, hd * KV_RANK:(hd + 1) * KV_RANK] = o[hd * t:(hd + 1) * t]


def _attention_sample(q, cache_kv, cache_kpe, k_new, l):
    nb, _, t, _ = q.shape
    past = cache_kv.shape[2]
    q_chunk = (past + np.arange(t)) // CHUNK
    k_chunk = np.arange(past + t) // CHUNK
    allowed = k_chunk[None, :] <= q_chunk[:, None]
    mask_cache = not allowed[:, :past].all()
    mask_new = not allowed[:, past:].all()
    return pl.pallas_call(
        functools.partial(_attn_sample_kernel, t=t, past=past, mask_cache=mask_cache, mask_new=mask_new),
        grid=(nb,),
        in_specs=[pl.BlockSpec((1, MLA_HEADS, t, QK_DIM), lambda b: (b, 0, 0, 0)),
                  pl.BlockSpec((None, 1, past, KV_RANK), lambda b: (l, b, 0, 0)),
                  pl.BlockSpec((None, 1, past, MLA_ROPE), lambda b: (l, b, 0, 0)),
                  pl.BlockSpec((1, t, QK_DIM), lambda b: (b, 0, 0))],
        out_specs=pl.BlockSpec((1, t, MLA_HEADS * KV_RANK), lambda b: (b, 0, 0)),
        out_shape=jax.ShapeDtypeStruct((nb, t, MLA_HEADS * KV_RANK), BF16),
        compiler_params=_params("parallel"),
        name="mla_attention_sample",
    )(q, cache_kv, cache_kpe, k_new)


def _ret_kernel(rq_ref, rk_ref, rv_ref, s0_ref, dec_ref, cross_ref, kdec_ref, gt_ref,
                o_ref, sout_ref, st_scr):
    c = pl.program_id(1)

    @pl.when(c == 0)
    def _():
        st_scr[...] = s0_ref[0]

    heads = range(RET_HEADS)
    ks = [slice(hd * RET_DK, (hd + 1) * RET_DK) for hd in heads]
    vs = [slice(hd * RET_DV, (hd + 1) * RET_DV) for hd in heads]
    qs = [rq_ref[0, :, ks[hd]] for hd in heads]
    kk = [rk_ref[0, :, ks[hd]] for hd in heads]
    vv = [rv_ref[0, :, vs[hd]] for hd in heads]
    st = [st_scr[hd] for hd in heads]
    scores = [_dot_nt(qs[hd], kk[hd]) for hd in heads]
    cross = [_dot(qs[hd], st[hd].astype(BF16)) for hd in heads]
    for hd in heads:
        k_dec = (kk[hd].astype(F32) * kdec_ref[hd]).astype(BF16)
        st_scr[hd] = gt_ref[hd] * st[hd] + _dot_tn(k_dec, vv[hd])
    for hd in heads:
        sc = (scores[hd] * dec_ref[hd]).astype(BF16)
        o_ref[0, :, vs[hd]] = (_dot(sc, vv[hd]) + cross[hd] * cross_ref[hd]).astype(BF16)

    @pl.when(c == pl.num_programs(1) - 1)
    def _():
        sout_ref[0] = st_scr[...]


def _ret_tables(t):
    f32 = np.float32
    log_g = np.log1p(-np.exp2(-5.0 - np.arange(RET_HEADS, dtype=f32))).astype(f32)
    n = np.arange(t, dtype=f32)
    diff = n[:, None] - n[None, :]
    dec = np.where(diff >= 0.0, np.exp(np.maximum(diff, 0.0)[None] * log_g[:, None, None]), 0.0)
    cross = np.exp((n + 1.0)[None, :] * log_g[:, None])
    kdec = np.exp((t - 1.0 - n)[None, :] * log_g[:, None])
    gt = np.exp(f32(t) * log_g)
    cross = np.broadcast_to(cross[:, :, None], (RET_HEADS, t, RET_DV))
    kdec = np.broadcast_to(kdec[:, :, None], (RET_HEADS, t, RET_DK))
    gt = np.broadcast_to(gt[:, None, None], (RET_HEADS, RET_DK, RET_DV))
    return tuple(jnp.asarray(a, F32) for a in (dec, cross, kdec, gt))


def _retention(rq, rk, rv, state0, tabs, t):
    nseq, length, _ = rq.shape
    tile = pl.BlockSpec((1, t, RET_W), lambda b, c: (b, c, 0))
    st = pl.BlockSpec((1, RET_HEADS, RET_DK, RET_DV), lambda b, c: (b, 0, 0, 0))
    const = lambda a: pl.BlockSpec(a.shape, lambda b, c: (0, 0, 0))
    return pl.pallas_call(
        _ret_kernel,
        grid=(nseq, length // t),
        in_specs=[tile, tile, tile, st] + [const(a) for a in tabs],
        out_specs=[tile, st],
        out_shape=[jax.ShapeDtypeStruct((nseq, length, RET_W), BF16),
                   jax.ShapeDtypeStruct(state0.shape, F32)],
        scratch_shapes=[pltpu.VMEM((RET_HEADS, RET_DK, RET_DV), F32)],
        compiler_params=_params("parallel", "arbitrary"),
        name="retention",
    )(rq, rk, rv, state0, *tabs)


def _out_kernel(x_ref, ol_ref, or_ref, rg_ref, gt_ref, gpost_ref, wuv_ref, wo_ref, grp_ref, o_ref):
    n_mla = MLA_HEADS * MLA_V
    o_mla = _dot(ol_ref[0], wuv_ref[...]).astype(BF16)
    mix = _dot(o_mla, wo_ref[:n_mla, :])
    o = or_ref[0].astype(F32)
    ms = _dot((o * o).astype(BF16), grp_ref[...])
    o_ret = (o * lax.rsqrt(ms + EPS) * _silu(rg_ref[0].astype(F32))).astype(BF16)
    mix = mix + _dot(o_ret, wo_ref[n_mla:, :])
    o_ref[0] = x_ref[0] + gt_ref[0] * (_rms(mix) * gpost_ref[...])


def _mixer_out(x, o_lat, o_ret, rg, gate, g_post, w_uv, w_out, l, tm):
    nseq, length, d = x.shape
    tile = lambda n: pl.BlockSpec((1, tm, n), lambda b, s: (b, s, 0))
    wl = lambda r, c: _resident((None, r, c), lambda b, s: (l, 0, 0))
    head = np.arange(RET_W) // RET_DV
    group_mean = jnp.asarray((head[:, None] == head[None, :]).astype(np.float32) / RET_DV, BF16)
    return pl.pallas_call(
        _out_kernel,
        grid=(nseq, length // tm),
        in_specs=[tile(d), tile(o_lat.shape[-1]), tile(RET_W), tile(RET_W), _mod_spec(gate, tm),
                  pl.BlockSpec((1, d), lambda b, s: (0, 0)),
                  wl(*w_uv.shape[1:]), wl(*w_out.shape[1:]),
                  _resident((RET_W, RET_W), lambda b, s: (0, 0))],
        out_specs=tile(d),
        out_shape=jax.ShapeDtypeStruct(x.shape, F32),
        compiler_params=_params("parallel", "parallel"),
        name="mixer_out",
    )(x, o_lat, o_ret, rg, gate, g_post, w_uv, w_out, group_mean)


def _prep_w_in(w_in):
    c0 = Q_RANK + KV_RANK
    half = MLA_ROPE // 2
    kpe = w_in[..., c0:c0 + MLA_ROPE]
    kpe_rot = jnp.concatenate([-kpe[..., half:], kpe[..., :half]], axis=-1)
    pad = jnp.zeros(w_in.shape[:-1] + (KPE_BLOCK - 2 * MLA_ROPE,), w_in.dtype)
    return jnp.concatenate([w_in[..., :c0], kpe, kpe_rot, pad, w_in[..., c0 + MLA_ROPE:]],
                           axis=-1).astype(BF16)


def _absorb_kernel(uq_ref, uk_ref, o_ref):
    o_ref[0, 0] = lax.dot_general(uq_ref[0, 0], uk_ref[0, 0], (((1,), (1,)), ((), ())),
                                  preferred_element_type=F32, precision=lax.Precision.HIGHEST)


def _prep_w_uq(w_uq, w_uk):
    depth = w_uq.shape[0]
    half = MLA_ROPE // 2
    w = w_uq.reshape(depth, Q_RANK, MLA_HEADS, MLA_NOPE + MLA_ROPE)
    nope = w[..., :MLA_NOPE].transpose(0, 2, 1, 3)
    uk = w_uk.reshape(depth, KV_RANK, MLA_HEADS, MLA_NOPE).transpose(0, 2, 1, 3)
    absorbed = pl.pallas_call(
        _absorb_kernel,
        grid=(depth, MLA_HEADS),
        in_specs=[pl.BlockSpec((1, 1, Q_RANK, MLA_NOPE), lambda l, h: (l, h, 0, 0)),
                  pl.BlockSpec((1, 1, KV_RANK, MLA_NOPE), lambda l, h: (l, h, 0, 0))],
        out_specs=pl.BlockSpec((1, 1, Q_RANK, KV_RANK), lambda l, h: (l, h, 0, 0)),
        out_shape=jax.ShapeDtypeStruct((depth, MLA_HEADS, Q_RANK, KV_RANK), F32),
        compiler_params=_params("parallel", "parallel"),
        name="absorb_w_uk",
    )(nope, uk)
    absorbed = absorbed.transpose(0, 2, 1, 3).reshape(depth, Q_RANK, MLA_HEADS * KV_RANK)
    pe = w[..., MLA_NOPE:]
    rot = jnp.concatenate([-pe[..., half:], pe[..., :half]], axis=-1)
    flat = lambda a: a.reshape(depth, Q_RANK, MLA_HEADS * MLA_ROPE)
    return jnp.concatenate([absorbed, flat(pe), flat(rot)], axis=-1).astype(BF16)


def _prep_w_uv(w_uv):
    depth = w_uv.shape[0]
    w = w_uv.reshape(depth, KV_RANK, MLA_HEADS, MLA_V)
    eye = jnp.eye(MLA_HEADS, dtype=w_uv.dtype)
    bd = jnp.einsum('lrhv,hg->lhrgv', w, eye)
    return bd.reshape(depth, MLA_HEADS * KV_RANK, MLA_HEADS * MLA_V).astype(BF16)


def _pick_tile(length, target):
    t = min(length, target)
    while length % t:
        t //= 2
    return t


def kernel(x_prompt, x_sample, cache_kv_latent, cache_k_rope, state_ret, c_prompt, c_sample,
           ada_w, ada_b, norm_pre, norm_post, ffn_w1, ffn_w3, ffn_w2,
           w_in, q_norm, kv_norm, w_uq, w_uk, w_uv, w_out):
    depth = ada_w.shape[0]
    nb_p, seq, d = x_prompt.shape
    nb_s, dec_seq, _ = x_sample.shape
    past = cache_kv_latent.shape[2]
    n_tok_s = nb_s * dec_seq

    w1, w3, w2 = ffn_w1.astype(BF16), ffn_w3.astype(BF16), ffn_w2.astype(BF16)
    w_in_b = _prep_w_in(w_in)
    w_uq_b = _prep_w_uq(w_uq, w_uk)
    w_uv_b = _prep_w_uv(w_uv)
    w_out_b = w_out.astype(BF16)

    mods = _ada_mods(jnp.concatenate([c_prompt, c_sample], axis=0), ada_w, ada_b)

    tm_p = _pick_tile(seq, 512)
    tm_ffn = _pick_tile(seq, 2 * FFN_SUB_ROWS)
    t_ret = _pick_tile(seq, 256)
    tabs_p = _rope_tables(np.arange(seq))
    tabs_s = _rope_tables(np.tile(past + np.arange(dec_seq), nb_s))
    ret_tabs_p = _ret_tables(t_ret)
    ret_tabs_s = _ret_tables(dec_seq)
    zero_state = jnp.zeros((nb_p, RET_HEADS, RET_DK, RET_DV), F32)

    def mod_p(l, r):
        return mods[l, :nb_p, r][:, None, :]

    def mod_s(l, r):
        return jnp.repeat(mods[l, nb_p:, r], dec_seq, axis=0)[None]

    y_p = x_prompt
    y_s = x_sample.reshape(1, n_tok_s, d)
    caches_p, caches_s = None, None
    ret_p, ret_s = [], []

    for l in range(depth):
        gp = lambda j: norm_pre[l, j][None, :]
        go = lambda j: norm_post[l, j][None, :]
        qn, kvn = q_norm[l][None, :], kv_norm[l][None, :]

        y_p = _ffn(y_p, mod_p(l, 0), mod_p(l, 1), mod_p(l, 2), gp(0), go(0), w1, w3, w2, l, 0, tm_ffn)
        q, k, vt, kv, kpe, rq, rk, rv, rg = _proj(y_p, mod_p(l, 3), mod_p(l, 4), gp(1), w_in_b, qn, kvn,
                                                  w_uq_b, tabs_p, l, tm_p, Q_SUB, caches_p)
        caches_p = (kv, kpe)
        o_lat = _attention_prompt(q, k, vt, tm_p)
        o_ret, r_fin = _retention(rq, rk, rv, zero_state, ret_tabs_p, t_ret)
        y_p = _mixer_out(y_p, o_lat, o_ret, rg, mod_p(l, 5), go(1), w_uv_b, w_out_b, l, tm_p)
        y_p = _ffn(y_p, mod_p(l, 6), mod_p(l, 7), mod_p(l, 8), gp(2), go(2), w1, w3, w2, l, 1, tm_ffn)
        ret_p.append(r_fin)

        y_s = _ffn(y_s, mod_s(l, 0), mod_s(l, 1), mod_s(l, 2), gp(0), go(0), w1, w3, w2, l, 0, n_tok_s)
        q, k, _, kv, kpe, rq, rk, rv, rg = _proj(y_s, mod_s(l, 3), mod_s(l, 4), gp(1), w_in_b, qn, kvn,
                                                 w_uq_b, tabs_s, l, n_tok_s, dec_seq, caches_s)
        caches_s = (kv, kpe)
        per_seq = lambda a: a.reshape(nb_s, dec_seq, a.shape[-1])
        q = q.reshape(nb_s, MLA_HEADS, dec_seq, QK_DIM)
        o_lat = _attention_sample(q, cache_kv_latent, cache_k_rope, per_seq(k), l)
        o_ret, r_new = _retention(per_seq(rq), per_seq(rk), per_seq(rv),
                                  state_ret[l], ret_tabs_s, dec_seq)
        y_s = _mixer_out(y_s, o_lat.reshape(1, n_tok_s, -1), o_ret.reshape(1, n_tok_s, -1), rg,
                         mod_s(l, 5), go(1), w_uv_b, w_out_b, l, n_tok_s)
        y_s = _ffn(y_s, mod_s(l, 6), mod_s(l, 7), mod_s(l, 8), gp(2), go(2), w1, w3, w2, l, 1, n_tok_s)
        ret_s.append(r_new)

    per_layer_seq = lambda a: a.reshape(depth, nb_s, dec_seq, a.shape[-1])
    return (y_p, y_s.reshape(nb_s, dec_seq, d),
            caches_p[0], caches_p[1], jnp.stack(ret_p),
            per_layer_seq(caches_s[0]), per_layer_seq(caches_s[1]), jnp.stack(ret_s))
```

```python
import functools

import jax
import jax.numpy as jnp
import numpy as np
from jax import lax
from jax.experimental import pallas as pl
from jax.experimental.pallas import tpu as pltpu

CHUNK = 64
N_SUB = 3
MLA_HEADS = 8
MLA_NOPE = 64
MLA_ROPE = 32
MLA_V = 64
Q_RANK = 256
KV_RANK = 128
RET_HEADS = 8
RET_DK = 64
RET_DV = 64
ROPE_BASE = 10000.0
EPS = 1e-6
MLA_SCALE = (MLA_NOPE + MLA_ROPE) ** -0.5
Q_SCALE = MLA_SCALE * float(np.log2(np.e))
MAX_GROUPS = 8
MXU_DIM = 256
FFN_CHUNK = 3 * MXU_DIM
FFN_SUB_ROWS = 512
Q_SUB = 128
MIX_SUB_ROWS = 256
RET_SCALE = RET_DK ** -0.5
NEG_INF = -1e30

QK_DIM = KV_RANK + MLA_ROPE
RET_W = RET_HEADS * RET_DK
LANES = 128
KPE_BLOCK = LANES
BF16_SUBLANES = 16
VT_ROWS = KV_RANK + BF16_SUBLANES
W_IN_COLS = Q_RANK + KV_RANK + KPE_BLOCK + 4 * RET_W

F32 = jnp.float32
BF16 = jnp.bfloat16

VMEM_LIMIT = 56 * 1024 * 1024


def _params(*sem):
    return pltpu.CompilerParams(dimension_semantics=sem, vmem_limit_bytes=VMEM_LIMIT)


def _resident(block_shape, index_map):
    return pl.BlockSpec(block_shape, index_map, pipeline_mode=pl.Buffered(1))


def _rms(x):
    return x * lax.rsqrt(jnp.mean(x * x, axis=-1, keepdims=True) + EPS)


def _silu(x):
    return x * jax.nn.sigmoid(x)


def _dot(a, b):
    return jnp.dot(a, b, preferred_element_type=F32)


def _dot_nt(a, b):
    return lax.dot_general(a, b, (((1,), (1,)), ((), ())), preferred_element_type=F32)


def _dot_tn(a, b):
    return lax.dot_general(a, b, (((0,), (0,)), ((), ())), preferred_element_type=F32)


def _ada_kernel(c_ref, w_ref, b_ref, o_ref):
    sc = _silu(c_ref[...])
    o_ref[0] = jnp.dot(sc, w_ref[0], preferred_element_type=F32,
                       precision=lax.Precision.HIGHEST) + b_ref[0]


def _ada_mods(c_all, ada_w, ada_b):
    depth, d, cols = ada_w.shape
    nb = c_all.shape[0]
    n_col = cols // d
    out = pl.pallas_call(
        _ada_kernel,
        grid=(depth, n_col),
        in_specs=[
            pl.BlockSpec((nb, d), lambda l, j: (0, 0)),
            pl.BlockSpec((1, d, d), lambda l, j: (l, 0, j)),
            pl.BlockSpec((1, 1, d), lambda l, j: (l, 0, j)),
        ],
        out_specs=pl.BlockSpec((1, nb, d), lambda l, j: (l, 0, j)),
        out_shape=jax.ShapeDtypeStruct((depth, nb, cols), F32),
        compiler_params=_params("parallel", "parallel"),
        name="ada_mods",
    )(c_all, ada_w, ada_b.reshape(depth, 1, cols))
    return out.reshape(depth, nb, n_col, d)


def _ffn_kernel(x_ref, sh_ref, sc_ref, gt_ref, gpre_ref, gpost_ref, w1_ref, w3_ref, w2_ref, o_ref,
                *, coef, f_chunk, sub):
    d_ff = w1_ref.shape[1]
    starts = list(range(0, x_ref.shape[1], sub))
    rows = lambda ref, r0: ref[0] if ref.shape[1] == 1 else ref[0, r0:r0 + sub]
    hs = [(_rms(x_ref[0, r0:r0 + sub]) * gpre_ref[...] * (1.0 + rows(sc_ref, r0))
           + rows(sh_ref, r0)).astype(BF16) for r0 in starts]
    outs = [None] * len(starts)
    pending = [None] * len(starts)

    def down(i, c0, c1):
        part = _dot(pending[i], w2_ref[c0:c1, :])
        outs[i] = part if outs[i] is None else outs[i] + part

    prev = None
    for c0 in range(0, d_ff, f_chunk):
        c1 = min(c0 + f_chunk, d_ff)
        gated = []
        for i, h in enumerate(hs):
            a = _dot(h, w1_ref[:, c0:c1])
            b = _dot(h, w3_ref[:, c0:c1])
            gated.append((_silu(a) * b).astype(BF16))
        if prev is not None:
            for i in range(len(hs)):
                down(i, *prev)
        pending, prev = gated, (c0, c1)
    for i, r0 in enumerate(starts):
        down(i, *prev)
        o_ref[0, r0:r0 + sub] = (x_ref[0, r0:r0 + sub]
                                 + (coef * rows(gt_ref, r0)) * (_rms(outs[i]) * gpost_ref[...]))


def _mod_spec(mod, tm):
    if mod.shape[1] == 1:
        return pl.BlockSpec((1, 1, mod.shape[2]), lambda b, s: (b, 0, 0))
    return pl.BlockSpec((1, tm, mod.shape[2]), lambda b, s: (b, s, 0))


def _ffn(x, shift, scale, gate, g_pre, g_post, w1, w3, w2, l, k, tm):
    nseq, length, d = x.shape
    d_ff = w1.shape[-1]
    f_chunk = min(d_ff, FFN_CHUNK)
    w_up = _resident((None, None, d, d_ff), lambda b, s: (l, k, 0, 0))
    w_dn = _resident((None, None, d_ff, d), lambda b, s: (l, k, 0, 0))
    vec = pl.BlockSpec((1, d), lambda b, s: (0, 0))
    tile = pl.BlockSpec((1, tm, d), lambda b, s: (b, s, 0))
    return pl.pallas_call(
        functools.partial(_ffn_kernel, coef=0.5, f_chunk=f_chunk, sub=min(tm, FFN_SUB_ROWS)),
        grid=(nseq, length // tm),
        in_specs=[tile, _mod_spec(shift, tm), _mod_spec(scale, tm), _mod_spec(gate, tm),
                  vec, vec, w_up, w_up, w_dn],
        out_specs=tile,
        out_shape=jax.ShapeDtypeStruct(x.shape, F32),
        compiler_params=_params("parallel", "parallel"),
        name="ffn",
    )(x, shift, scale, gate, g_pre, g_post, w1, w3, w2)


def _proj_kernel(x_ref, sh_ref, sc_ref, gpre_ref, win_ref, qn_ref, kvn_ref, wuq_ref,
                 cq_ref, sq_ref, ck_ref, sk_ref, rc_ref, rsa_ref, rsb_ref, *rest, sub):
    q_out, k_out, vt_out, kv_out, kpe_out, rq_out, rk_out, rv_out, rg_out = rest[-9:]
    qsub = q_out.shape[3]
    c0 = Q_RANK + KV_RANK
    r0 = c0 + KPE_BLOCK
    n_abs = MLA_HEADS * KV_RANK
    n_pe = MLA_HEADS * MLA_ROPE

    starts = list(range(0, x_ref.shape[1], sub))
    rows = lambda ref, s0: ref[0] if ref.shape[1] == 1 else ref[0, s0:s0 + sub]

    def project(s0):
        h = (_rms(x_ref[0, s0:s0 + sub]) * gpre_ref[...] * (1.0 + rows(sc_ref, s0))
             + rows(sh_ref, s0)).astype(BF16)
        return _dot(h, win_ref[...])

    z_next = project(starts[0])
    for k, s0 in enumerate(starts):
        z = z_next
        if k + 1 < len(starts):
            z_next = project(starts[k + 1])
        rs = slice(s0, s0 + sub)
        q_lat = (_rms(z[:, :Q_RANK]) * qn_ref[...]).astype(BF16)
        kv_lat = _rms(z[:, Q_RANK:c0]) * kvn_ref[...]
        k_pe = z[:, c0:c0 + MLA_ROPE] * ck_ref[rs] + z[:, c0 + MLA_ROPE:c0 + 2 * MLA_ROPE] * sk_ref[rs]
        kv_out[0, rs] = kv_lat
        kpe_out[0, rs] = k_pe
        k_out[0, rs] = jnp.concatenate([kv_lat, k_pe], axis=-1).astype(BF16)
        ones = jnp.ones((VT_ROWS - KV_RANK, sub), F32)
        vt_w = vt_out.shape[3]
        vt_out[0, s0 // vt_w, :, s0 % vt_w:s0 % vt_w + sub] = (
            jnp.concatenate([kv_lat.T, ones], axis=0).astype(BF16))

        qq = _dot(q_lat, wuq_ref[...])
        q_abs = qq[:, :n_abs]
        q_pe = qq[:, n_abs:n_abs + n_pe] * cq_ref[rs] + qq[:, n_abs + n_pe:] * sq_ref[rs]
        for hd in range(MLA_HEADS):
            qh = jnp.concatenate([q_abs[:, hd * KV_RANK:(hd + 1) * KV_RANK],
                                  q_pe[:, hd * MLA_ROPE:(hd + 1) * MLA_ROPE]], axis=-1)
            qh = (qh * Q_SCALE).astype(BF16)
            for r in range(sub // qsub):
                q_out[0, s0 // qsub + r, hd] = qh[r * qsub:(r + 1) * qsub]

        rc, rsa, rsb = rc_ref[rs], rsa_ref[rs], rsb_ref[rs]
        for off, out, scale in ((r0, rq_out, 1.0), (r0 + RET_W, rk_out, RET_SCALE)):
            for c in range(RET_W // LANES):
                xc = z[:, off + c * LANES:off + (c + 1) * LANES]
                y = (xc * rc + pltpu.roll(xc, LANES - RET_DK // 2, 1) * rsa
                     + pltpu.roll(xc, RET_DK // 2, 1) * rsb)
                out[0, rs, c * LANES:(c + 1) * LANES] = (y * scale).astype(BF16)
        rv_out[0, rs] = z[:, r0 + 2 * RET_W:r0 + 3 * RET_W].astype(BF16)
        rg_out[0, rs] = z[:, r0 + 3 * RET_W:r0 + 4 * RET_W].astype(BF16)


def _proj(x, shift, scale, g_pre, w_in, q_norm, kv_norm, w_uq, tabs, l, tm, qsub, vt_w, caches):
    nseq, length, d = x.shape
    depth = w_in.shape[0]
    grid = (nseq, length // tm)
    vec = lambda n: pl.BlockSpec((1, n), lambda b, s: (0, 0))
    tile = lambda n: pl.BlockSpec((1, tm, n), lambda b, s: (b, s, 0))
    tab = lambda n: pl.BlockSpec((tm, n), lambda b, s: (s, 0))
    wl = lambda r, c: _resident((None, r, c), lambda b, s: (l, 0, 0))
    layer_tile = lambda n: pl.BlockSpec((None, 1, tm, n), lambda b, s: (l, b, s, 0))
    bf = lambda n: jax.ShapeDtypeStruct((nseq, length, n), BF16)
    in_specs = [tile(d), _mod_spec(shift, tm), _mod_spec(scale, tm), vec(d),
                wl(d, W_IN_COLS), vec(Q_RANK), vec(KV_RANK),
                wl(Q_RANK, w_uq.shape[-1]),
                tab(MLA_HEADS * MLA_ROPE), tab(MLA_HEADS * MLA_ROPE), tab(MLA_ROPE), tab(MLA_ROPE),
                tab(LANES), tab(LANES), tab(LANES)]
    args = [x, shift, scale, g_pre, w_in, q_norm, kv_norm, w_uq, *tabs]
    aliases = {}
    if caches is not None:
        aliases = {len(args): 3, len(args) + 1: 4}
        in_specs += [pl.BlockSpec(memory_space=pl.ANY)] * 2
        args += list(caches)
    return pl.pallas_call(
        functools.partial(_proj_kernel, sub=max(qsub, min(tm, MIX_SUB_ROWS))),
        grid=grid,
        in_specs=in_specs,
        out_specs=[pl.BlockSpec((1, tm // qsub, MLA_HEADS, qsub, QK_DIM), lambda b, s: (b, s, 0, 0, 0)),
                   tile(QK_DIM),
                   pl.BlockSpec((1, tm // vt_w, VT_ROWS, vt_w), lambda b, s: (b, s, 0, 0)),
                   layer_tile(KV_RANK), layer_tile(MLA_ROPE),
                   tile(RET_W), tile(RET_W), tile(RET_W), tile(RET_W)],
        out_shape=[jax.ShapeDtypeStruct((nseq, length // qsub, MLA_HEADS, qsub, QK_DIM), BF16),
                   bf(QK_DIM),
                   jax.ShapeDtypeStruct((nseq, length // vt_w, VT_ROWS, vt_w), BF16),
                   jax.ShapeDtypeStruct((depth, nseq, length, KV_RANK), F32),
                   jax.ShapeDtypeStruct((depth, nseq, length, MLA_ROPE), F32),
                   bf(RET_W), bf(RET_W), bf(RET_W), bf(RET_W)],
        input_output_aliases=aliases,
        compiler_params=_params("parallel", "parallel"),
        name="mixer_proj",
    )(*args)


def _rope_tables(pos):
    pos = np.asarray(pos, np.float32)[:, None]

    def cs(half):
        inv = (ROPE_BASE ** (-np.arange(half, dtype=np.float32) / half)).astype(np.float32)
        ang = pos * inv[None, :]
        return np.cos(ang), np.sin(ang)

    c16, s16 = cs(MLA_ROPE // 2)
    ck = np.concatenate([c16, c16], axis=-1)
    sk = np.concatenate([s16, s16], axis=-1)
    cq = np.tile(ck, (1, MLA_HEADS))
    sq = np.tile(sk, (1, MLA_HEADS))
    c32, s32 = cs(RET_DK // 2)
    zero = np.zeros_like(s32)
    reps = LANES // RET_DK
    rc = np.tile(np.concatenate([c32, c32], axis=-1), (1, reps))
    rsa = np.tile(np.concatenate([-s32, zero], axis=-1), (1, reps))
    rsb = np.tile(np.concatenate([zero, s32], axis=-1), (1, reps))
    return tuple(jnp.asarray(t, F32) for t in (cq, sq, ck, sk, rc, rsa, rsb))


def _attn_kernel(q_ref, k_ref, vt_ref, o_ref, m_scr, alpha_scr, p_scr, acc_scr, *, t, qs):
    i = pl.program_id(1)
    nq = pl.num_programs(1) - 1
    n_sub = t // qs
    cb = MLA_HEADS * qs
    rows = n_sub * cb
    subs = [(r, slice(r * cb, (r + 1) * cb)) for r in range(n_sub)]
    slot = i % 2
    q = q_ref[0].reshape(rows, QK_DIM)

    def k_tile(j):
        return k_ref[0, pl.ds(pl.multiple_of(j * t, t), t), :]

    def scores(j):
        return _dot_nt(k_tile(j), q)

    def scores_diag(j):
        kt = k_tile(j)
        return [_dot_nt(kt[:(r + 1) * qs], q[cs]) for r, cs in subs]

    def update(cs, sc):
        n = sc.shape[0]
        m_prev = m_scr[:, cs]
        m_part = jnp.max(sc.reshape(MAX_GROUPS, n // MAX_GROUPS, cb), axis=0)
        m_new = jnp.maximum(m_prev, jnp.max(m_part, axis=0, keepdims=True))
        alpha_scr[:, cs] = jnp.exp2(m_prev - m_new)
        p_scr[:n, cs] = jnp.exp2((sc - m_new).astype(BF16))
        m_scr[:, cs] = m_new

    def softmax(s):
        for _, cs in subs:
            update(cs, s[:, cs])

    def softmax_diag(parts):
        kc = lax.broadcasted_iota(jnp.int32, (qs, qs), 0) // CHUNK
        qc = lax.broadcasted_iota(jnp.int32, (qs, qs), 1) // CHUNK
        allowed = jnp.concatenate([kc <= qc] * MLA_HEADS, axis=1)
        for (r, cs), sc in zip(subs, parts):
            last = jnp.where(allowed, sc[r * qs:], NEG_INF)
            update(cs, last if r == 0 else jnp.concatenate([sc[:r * qs], last], axis=0))

    def values(j, acc_slot):
        for _, cs in subs:
            acc_scr[acc_slot, :, cs] = (acc_scr[acc_slot, :, cs] * alpha_scr[:, cs]
                                        + _dot(vt_ref[0, j], p_scr[:, cs]))

    def values_diag(j, acc_slot):
        for r, cs in subs:
            n = (r + 1) * qs
            acc_scr[acc_slot, :, cs] = (acc_scr[acc_slot, :, cs] * alpha_scr[:, cs]
                                        + _dot(vt_ref[0, j, :, :n], p_scr[:n, cs]))

    def begin():
        m_scr[...] = jnp.full(m_scr.shape, NEG_INF, F32)
        acc_scr[slot] = jnp.zeros(acc_scr.shape[1:], F32)

    def drain_previous():
        prev = 1 - slot
        values_diag(i - 1, prev)
        acc = acc_scr[prev]
        o_t = acc[:KV_RANK] * (1.0 / acc[KV_RANK:KV_RANK + 1])
        for r in range(n_sub):
            for hd in range(MLA_HEADS):
                c0 = (r * MLA_HEADS + hd) * qs
                o_ref[0, r * qs:(r + 1) * qs, hd * KV_RANK:(hd + 1) * KV_RANK] = (
                    o_t[:, c0:c0 + qs].T.astype(BF16))

    @pl.when(i == 0)
    def _():
        begin()
        softmax_diag(scores_diag(0))

    @pl.when(jnp.logical_and(i > 0, i < nq))
    def _():
        begin()
        s = scores(0)
        drain_previous()
        softmax(s)

        def body(j, carry):
            s = scores(j)
            values(j - 1, slot)
            softmax(s)
            return carry

        lax.fori_loop(1, i, body, 0)
        parts = scores_diag(i)
        values(i - 1, slot)
        softmax_diag(parts)

    @pl.when(i == nq)
    def _():
        drain_previous()


def _attention_prompt(q, k, vt, t):
    nseq, _, _, qs, _ = q.shape
    length = k.shape[1]
    rows = MLA_HEADS * t
    nq = length // t
    assert vt.shape[-1] == t and t % qs == 0 and qs % CHUNK == 0
    return pl.pallas_call(
        functools.partial(_attn_kernel, t=t, qs=qs),
        grid=(nseq, nq + 1),
        in_specs=[pl.BlockSpec((1, t // qs, MLA_HEADS, qs, QK_DIM),
                               lambda b, i: (b, jnp.minimum(i, nq - 1), 0, 0, 0)),
                  pl.BlockSpec((1, length, QK_DIM), lambda b, i: (b, 0, 0)),
                  pl.BlockSpec((1, nq, VT_ROWS, t), lambda b, i: (b, 0, 0, 0))],
        out_specs=pl.BlockSpec((1, t, MLA_HEADS * KV_RANK), lambda b, i: (b, jnp.maximum(i - 1, 0), 0)),
        out_shape=jax.ShapeDtypeStruct((nseq, length, MLA_HEADS * KV_RANK), BF16),
        scratch_shapes=[pltpu.VMEM((1, rows), F32), pltpu.VMEM((1, rows), F32),
                        pltpu.VMEM((t, rows), BF16), pltpu.VMEM((2, VT_ROWS, rows), F32)],
        compiler_params=_params("parallel", "arbitrary"),
        name="mla_attention_prompt",
    )(q, k, vt)


def _chunk_mask(s, t, q_start, k_start):
    n = s.shape[1]
    qpos = q_start + lax.broadcasted_iota(jnp.int32, (t, n), 0)
    kpos = k_start + lax.broadcasted_iota(jnp.int32, (t, n), 1)
    allowed = (kpos // CHUNK) <= (qpos // CHUNK)
    return jnp.where(allowed[None], s.reshape(MLA_HEADS, t, n), NEG_INF).reshape(s.shape)


def _attn_sample_kernel(q_ref, ckv_ref, ckpe_ref, kn_ref, o_ref, *, t, past, mask_cache, mask_new):
    rows = MLA_HEADS * t
    q = q_ref[0].reshape(rows, QK_DIM)
    ckv = ckv_ref[0].astype(BF16)
    ckpe = ckpe_ref[0].astype(BF16)
    kn = kn_ref[0]
    s_c = _dot_nt(q[:, :KV_RANK], ckv) + _dot_nt(q[:, KV_RANK:], ckpe)
    s_n = _dot_nt(q, kn)
    if mask_cache:
        s_c = _chunk_mask(s_c, t, past, 0)
    if mask_new:
        s_n = _chunk_mask(s_n, t, past, past)
    m = jnp.maximum(jnp.max(s_c, axis=1, keepdims=True), jnp.max(s_n, axis=1, keepdims=True))
    p_c = jnp.exp2(s_c - m)
    p_n = jnp.exp2(s_n - m)
    denom = jnp.sum(p_c, axis=1, keepdims=True) + jnp.sum(p_n, axis=1, keepdims=True)
    o = _dot(p_c.astype(BF16), ckv) + _dot(p_n.astype(BF16), kn[:, :KV_RANK])
    o = (o / denom).astype(BF16)
    for hd in range(MLA_HEADS):
        o_ref[0, :, hd * KV_RANK:(hd + 1) * KV_RANK] = o[hd * t:(hd + 1) * t]


def _attention_sample(q, cache_kv, cache_kpe, k_new, l):
    nb, _, t, _ = q.shape
    past = cache_kv.shape[2]
    q_chunk = (past + np.arange(t)) // CHUNK
    k_chunk = np.arange(past + t) // CHUNK
    allowed = k_chunk[None, :] <= q_chunk[:, None]
    mask_cache = not allowed[:, :past].all()
    mask_new = not allowed[:, past:].all()
    return pl.pallas_call(
        functools.partial(_attn_sample_kernel, t=t, past=past, mask_cache=mask_cache, mask_new=mask_new),
        grid=(nb,),
        in_specs=[pl.BlockSpec((1, MLA_HEADS, t, QK_DIM), lambda b: (b, 0, 0, 0)),
                  pl.BlockSpec((None, 1, past, KV_RANK), lambda b: (l, b, 0, 0)),
                  pl.BlockSpec((None, 1, past, MLA_ROPE), lambda b: (l, b, 0, 0)),
                  pl.BlockSpec((1, t, QK_DIM), lambda b: (b, 0, 0))],
        out_specs=pl.BlockSpec((1, t, MLA_HEADS * KV_RANK), lambda b: (b, 0, 0)),
        out_shape=jax.ShapeDtypeStruct((nb, t, MLA_HEADS * KV_RANK), BF16),
        compiler_params=_params("parallel"),
        name="mla_attention_sample",
    )(q, cache_kv, cache_kpe, k_new)


def _ret_kernel(rq_ref, rk_ref, rv_ref, s0_ref, dec_ref, cross_ref, kdec_ref, gt_ref,
                o_ref, sout_ref, st_scr):
    c = pl.program_id(1)

    @pl.when(c == 0)
    def _():
        st_scr[...] = s0_ref[0]

    heads = range(RET_HEADS)
    ks = [slice(hd * RET_DK, (hd + 1) * RET_DK) for hd in heads]
    vs = [slice(hd * RET_DV, (hd + 1) * RET_DV) for hd in heads]
    qs = [rq_ref[0, :, ks[hd]] for hd in heads]
    kk = [rk_ref[0, :, ks[hd]] for hd in heads]
    vv = [rv_ref[0, :, vs[hd]] for hd in heads]
    st = [st_scr[hd] for hd in heads]
    scores = [_dot_nt(qs[hd], kk[hd]) for hd in heads]
    cross = [_dot(qs[hd], st[hd].astype(BF16)) for hd in heads]
    for hd in heads:
        k_dec = (kk[hd].astype(F32) * kdec_ref[hd]).astype(BF16)
        st_scr[hd] = gt_ref[hd] * st[hd] + _dot_tn(k_dec, vv[hd])
    for hd in heads:
        sc = (scores[hd] * dec_ref[hd]).astype(BF16)
        o_ref[0, :, vs[hd]] = (_dot(sc, vv[hd]) + cross[hd] * cross_ref[hd]).astype(BF16)

    @pl.when(c == pl.num_programs(1) - 1)
    def _():
        sout_ref[0] = st_scr[...]


def _ret_tables(t):
    f32 = np.float32
    log_g = np.log1p(-np.exp2(-5.0 - np.arange(RET_HEADS, dtype=f32))).astype(f32)
    n = np.arange(t, dtype=f32)
    diff = n[:, None] - n[None, :]
    dec = np.where(diff >= 0.0, np.exp(np.maximum(diff, 0.0)[None] * log_g[:, None, None]), 0.0)
    cross = np.exp((n + 1.0)[None, :] * log_g[:, None])
    kdec = np.exp((t - 1.0 - n)[None, :] * log_g[:, None])
    gt = np.exp(f32(t) * log_g)
    cross = np.broadcast_to(cross[:, :, None], (RET_HEADS, t, RET_DV))
    kdec = np.broadcast_to(kdec[:, :, None], (RET_HEADS, t, RET_DK))
    gt = np.broadcast_to(gt[:, None, None], (RET_HEADS, RET_DK, RET_DV))
    return tuple(jnp.asarray(a, F32) for a in (dec, cross, kdec, gt))


def _retention(rq, rk, rv, state0, tabs, t):
    nseq, length, _ = rq.shape
    tile = pl.BlockSpec((1, t, RET_W), lambda b, c: (b, c, 0))
    st = pl.BlockSpec((1, RET_HEADS, RET_DK, RET_DV), lambda b, c: (b, 0, 0, 0))
    const = lambda a: pl.BlockSpec(a.shape, lambda b, c: (0, 0, 0))
    return pl.pallas_call(
        _ret_kernel,
        grid=(nseq, length // t),
        in_specs=[tile, tile, tile, st] + [const(a) for a in tabs],
        out_specs=[tile, st],
        out_shape=[jax.ShapeDtypeStruct((nseq, length, RET_W), BF16),
                   jax.ShapeDtypeStruct(state0.shape, F32)],
        scratch_shapes=[pltpu.VMEM((RET_HEADS, RET_DK, RET_DV), F32)],
        compiler_params=_params("parallel", "arbitrary"),
        name="retention",
    )(rq, rk, rv, state0, *tabs)


def _out_kernel(x_ref, ol_ref, or_ref, rg_ref, gt_ref, gpost_ref, wuv_ref, wo_ref, grp_ref, o_ref, *, sub):
    n_mla = MLA_HEADS * MLA_V
    starts = list(range(0, x_ref.shape[1], sub))
    tiles = [slice(s0, s0 + sub) for s0 in starts]
    o_mla = [_dot(ol_ref[0, rs], wuv_ref[...]).astype(BF16) for rs in tiles]
    mix = [_dot(om, wo_ref[:n_mla, :]) for om in o_mla]
    o = [or_ref[0, rs].astype(F32) for rs in tiles]
    ms = [_dot((v * v).astype(BF16), grp_ref[...]) for v in o]
    o_ret = [(v * lax.rsqrt(m + EPS) * _silu(rg_ref[0, rs].astype(F32))).astype(BF16)
             for v, m, rs in zip(o, ms, tiles)]
    for i, rs in enumerate(tiles):
        gate = gt_ref[0] if gt_ref.shape[1] == 1 else gt_ref[0, rs]
        total = mix[i] + _dot(o_ret[i], wo_ref[n_mla:, :])
        o_ref[0, rs] = x_ref[0, rs] + gate * (_rms(total) * gpost_ref[...])


def _mixer_out(x, o_lat, o_ret, rg, gate, g_post, w_uv, w_out, l, tm):
    nseq, length, d = x.shape
    tile = lambda n: pl.BlockSpec((1, tm, n), lambda b, s: (b, s, 0))
    wl = lambda r, c: _resident((None, r, c), lambda b, s: (l, 0, 0))
    head = np.arange(RET_W) // RET_DV
    group_mean = jnp.asarray((head[:, None] == head[None, :]).astype(np.float32) / RET_DV, BF16)
    return pl.pallas_call(
        functools.partial(_out_kernel, sub=min(tm, MIX_SUB_ROWS)),
        grid=(nseq, length // tm),
        in_specs=[tile(d), tile(o_lat.shape[-1]), tile(RET_W), tile(RET_W), _mod_spec(gate, tm),
                  pl.BlockSpec((1, d), lambda b, s: (0, 0)),
                  wl(*w_uv.shape[1:]), wl(*w_out.shape[1:]),
                  _resident((RET_W, RET_W), lambda b, s: (0, 0))],
        out_specs=tile(d),
        out_shape=jax.ShapeDtypeStruct(x.shape, F32),
        compiler_params=_params("parallel", "parallel"),
        name="mixer_out",
    )(x, o_lat, o_ret, rg, gate, g_post, w_uv, w_out, group_mean)


def _prep_w_in(w_in):
    c0 = Q_RANK + KV_RANK
    half = MLA_ROPE // 2
    kpe = w_in[..., c0:c0 + MLA_ROPE]
    kpe_rot = jnp.concatenate([-kpe[..., half:], kpe[..., :half]], axis=-1)
    pad = jnp.zeros(w_in.shape[:-1] + (KPE_BLOCK - 2 * MLA_ROPE,), w_in.dtype)
    return jnp.concatenate([w_in[..., :c0], kpe, kpe_rot, pad, w_in[..., c0 + MLA_ROPE:]],
                           axis=-1).astype(BF16)


def _absorb_kernel(uq_ref, uk_ref, o_ref):
    o_ref[0, 0] = lax.dot_general(uq_ref[0, 0], uk_ref[0, 0], (((1,), (1,)), ((), ())),
                                  preferred_element_type=F32, precision=lax.Precision.HIGHEST)


def _prep_w_uq(w_uq, w_uk):
    depth = w_uq.shape[0]
    half = MLA_ROPE // 2
    w = w_uq.reshape(depth, Q_RANK, MLA_HEADS, MLA_NOPE + MLA_ROPE)
    nope = w[..., :MLA_NOPE].transpose(0, 2, 1, 3)
    uk = w_uk.reshape(depth, KV_RANK, MLA_HEADS, MLA_NOPE).transpose(0, 2, 1, 3)
    absorbed = pl.pallas_call(
        _absorb_kernel,
        grid=(depth, MLA_HEADS),
        in_specs=[pl.BlockSpec((1, 1, Q_RANK, MLA_NOPE), lambda l, h: (l, h, 0, 0)),
                  pl.BlockSpec((1, 1, KV_RANK, MLA_NOPE), lambda l, h: (l, h, 0, 0))],
        out_specs=pl.BlockSpec((1, 1, Q_RANK, KV_RANK), lambda l, h: (l, h, 0, 0)),
        out_shape=jax.ShapeDtypeStruct((depth, MLA_HEADS, Q_RANK, KV_RANK), F32),
        compiler_params=_params("parallel", "parallel"),
        name="absorb_w_uk",
    )(nope, uk)
    absorbed = absorbed.transpose(0, 2, 1, 3).reshape(depth, Q_RANK, MLA_HEADS * KV_RANK)
    pe = w[..., MLA_NOPE:]
    rot = jnp.concatenate([-pe[..., half:], pe[..., :half]], axis=-1)
    flat = lambda a: a.reshape(depth, Q_RANK, MLA_HEADS * MLA_ROPE)
    return jnp.concatenate([absorbed, flat(pe), flat(rot)], axis=-1).astype(BF16)


def _prep_w_uv(w_uv):
    depth = w_uv.shape[0]
    w = w_uv.reshape(depth, KV_RANK, MLA_HEADS, MLA_V)
    eye = jnp.eye(MLA_HEADS, dtype=w_uv.dtype)
    bd = jnp.einsum('lrhv,hg->lhrgv', w, eye)
    return bd.reshape(depth, MLA_HEADS * KV_RANK, MLA_HEADS * MLA_V).astype(BF16)


def _pick_tile(length, target):
    t = min(length, target)
    while length % t:
        t //= 2
    return t


def kernel(x_prompt, x_sample, cache_kv_latent, cache_k_rope, state_ret, c_prompt, c_sample,
           ada_w, ada_b, norm_pre, norm_post, ffn_w1, ffn_w3, ffn_w2,
           w_in, q_norm, kv_norm, w_uq, w_uk, w_uv, w_out):
    depth = ada_w.shape[0]
    nb_p, seq, d = x_prompt.shape
    nb_s, dec_seq, _ = x_sample.shape
    past = cache_kv_latent.shape[2]
    n_tok_s = nb_s * dec_seq

    w1, w3, w2 = ffn_w1.astype(BF16), ffn_w3.astype(BF16), ffn_w2.astype(BF16)
    w_in_b = _prep_w_in(w_in)
    w_uq_b = _prep_w_uq(w_uq, w_uk)
    w_uv_b = _prep_w_uv(w_uv)
    w_out_b = w_out.astype(BF16)

    mods = _ada_mods(jnp.concatenate([c_prompt, c_sample], axis=0), ada_w, ada_b)

    t_att = _pick_tile(seq, 512)
    tm_p = _pick_tile(seq, 4 * MIX_SUB_ROWS)
    tm_ffn = _pick_tile(seq, 2 * FFN_SUB_ROWS)
    t_ret = _pick_tile(seq, 256)
    tabs_p = _rope_tables(np.arange(seq))
    tabs_s = _rope_tables(np.tile(past + np.arange(dec_seq), nb_s))
    ret_tabs_p = _ret_tables(t_ret)
    ret_tabs_s = _ret_tables(dec_seq)
    zero_state = jnp.zeros((nb_p, RET_HEADS, RET_DK, RET_DV), F32)

    def mod_p(l, r):
        return mods[l, :nb_p, r][:, None, :]

    def mod_s(l, r):
        return jnp.repeat(mods[l, nb_p:, r], dec_seq, axis=0)[None]

    y_p = x_prompt
    y_s = x_sample.reshape(1, n_tok_s, d)
    caches_p, caches_s = None, None
    ret_p, ret_s = [], []

    for l in range(depth):
        gp = lambda j: norm_pre[l, j][None, :]
        go = lambda j: norm_post[l, j][None, :]
        qn, kvn = q_norm[l][None, :], kv_norm[l][None, :]

        y_p = _ffn(y_p, mod_p(l, 0), mod_p(l, 1), mod_p(l, 2), gp(0), go(0), w1, w3, w2, l, 0, tm_ffn)
        q, k, vt, kv, kpe, rq, rk, rv, rg = _proj(y_p, mod_p(l, 3), mod_p(l, 4), gp(1), w_in_b, qn, kvn,
                                                  w_uq_b, tabs_p, l, tm_p, Q_SUB, t_att, caches_p)
        caches_p = (kv, kpe)
        o_lat = _attention_prompt(q, k, vt, t_att)
        o_ret, r_fin = _retention(rq, rk, rv, zero_state, ret_tabs_p, t_ret)
        y_p = _mixer_out(y_p, o_lat, o_ret, rg, mod_p(l, 5), go(1), w_uv_b, w_out_b, l, tm_p)
        y_p = _ffn(y_p, mod_p(l, 6), mod_p(l, 7), mod_p(l, 8), gp(2), go(2), w1, w3, w2, l, 1, tm_ffn)
        ret_p.append(r_fin)

        y_s = _ffn(y_s, mod_s(l, 0), mod_s(l, 1), mod_s(l, 2), gp(0), go(0), w1, w3, w2, l, 0, n_tok_s)
        q, k, _, kv, kpe, rq, rk, rv, rg = _proj(y_s, mod_s(l, 3), mod_s(l, 4), gp(1), w_in_b, qn, kvn,
                                                 w_uq_b, tabs_s, l, n_tok_s, dec_seq, n_tok_s, caches_s)
        caches_s = (kv, kpe)
        per_seq = lambda a: a.reshape(nb_s, dec_seq, a.shape[-1])
        q = q.reshape(nb_s, MLA_HEADS, dec_seq, QK_DIM)
        o_lat = _attention_sample(q, cache_kv_latent, cache_k_rope, per_seq(k), l)
        o_ret, r_new = _retention(per_seq(rq), per_seq(rk), per_seq(rv),
                                  state_ret[l], ret_tabs_s, dec_seq)
        y_s = _mixer_out(y_s, o_lat.reshape(1, n_tok_s, -1), o_ret.reshape(1, n_tok_s, -1), rg,
                         mod_s(l, 5), go(1), w_uv_b, w_out_b, l, n_tok_s)
        y_s = _ffn(y_s, mod_s(l, 6), mod_s(l, 7), mod_s(l, 8), gp(2), go(2), w1, w3, w2, l, 1, n_tok_s)
        ret_s.append(r_new)

    per_layer_seq = lambda a: a.reshape(depth, nb_s, dec_seq, a.shape[-1])
    return (y_p, y_s.reshape(nb_s, dec_seq, d),
            caches_p[0], caches_p[1], jnp.stack(ret_p),
            per_layer_seq(caches_s[0]), per_layer_seq(caches_s[1]), jnp.stack(ret_s))
```

```python
import functools

import jax
import jax.numpy as jnp
import numpy as np
from jax import lax
from jax.experimental import pallas as pl
from jax.experimental.pallas import tpu as pltpu

CHUNK = 64
N_SUB = 3
MLA_HEADS = 8
MLA_NOPE = 64
MLA_ROPE = 32
MLA_V = 64
Q_RANK = 256
KV_RANK = 128
RET_HEADS = 8
RET_DK = 64
RET_DV = 64
ROPE_BASE = 10000.0
EPS = 1e-6
MLA_SCALE = (MLA_NOPE + MLA_ROPE) ** -0.5
Q_SCALE = MLA_SCALE * float(np.log2(np.e))
MAX_GROUPS = 8
MXU_DIM = 256
FFN_CHUNK = 3 * MXU_DIM
FFN_SUB_ROWS = 512
Q_SUB = 128
MIX_SUB_ROWS = 256
FULL_TILE_COLS = 256
RET_SCALE = RET_DK ** -0.5
NEG_INF = -1e30

QK_DIM = KV_RANK + MLA_ROPE
RET_W = RET_HEADS * RET_DK
LANES = 128
KPE_BLOCK = LANES
BF16_SUBLANES = 16
VT_ROWS = KV_RANK + BF16_SUBLANES
W_IN_COLS = Q_RANK + KV_RANK + KPE_BLOCK + 4 * RET_W

F32 = jnp.float32
BF16 = jnp.bfloat16

VMEM_LIMIT = 56 * 1024 * 1024


def _params(*sem):
    return pltpu.CompilerParams(dimension_semantics=sem, vmem_limit_bytes=VMEM_LIMIT)


def _resident(block_shape, index_map):
    return pl.BlockSpec(block_shape, index_map, pipeline_mode=pl.Buffered(1))


def _rms(x):
    return x * lax.rsqrt(jnp.mean(x * x, axis=-1, keepdims=True) + EPS)


def _silu(x):
    return x * jax.nn.sigmoid(x)


def _dot(a, b):
    return jnp.dot(a, b, preferred_element_type=F32)


def _dot_nt(a, b):
    return lax.dot_general(a, b, (((1,), (1,)), ((), ())), preferred_element_type=F32)


def _dot_tn(a, b):
    return lax.dot_general(a, b, (((0,), (0,)), ((), ())), preferred_element_type=F32)


def _ada_kernel(c_ref, w_ref, b_ref, o_ref):
    sc = _silu(c_ref[...])
    o_ref[0] = jnp.dot(sc, w_ref[0], preferred_element_type=F32,
                       precision=lax.Precision.HIGHEST) + b_ref[0]


def _ada_mods(c_all, ada_w, ada_b):
    depth, d, cols = ada_w.shape
    nb = c_all.shape[0]
    n_col = cols // d
    out = pl.pallas_call(
        _ada_kernel,
        grid=(depth, n_col),
        in_specs=[
            pl.BlockSpec((nb, d), lambda l, j: (0, 0)),
            pl.BlockSpec((1, d, d), lambda l, j: (l, 0, j)),
            pl.BlockSpec((1, 1, d), lambda l, j: (l, 0, j)),
        ],
        out_specs=pl.BlockSpec((1, nb, d), lambda l, j: (l, 0, j)),
        out_shape=jax.ShapeDtypeStruct((depth, nb, cols), F32),
        compiler_params=_params("parallel", "parallel"),
        name="ada_mods",
    )(c_all, ada_w, ada_b.reshape(depth, 1, cols))
    return out.reshape(depth, nb, n_col, d)


def _ffn_kernel(x_ref, sh_ref, sc_ref, gt_ref, gpre_ref, gpost_ref, w1_ref, w3_ref, w2_ref, o_ref,
                *, coef, f_chunk, sub):
    d_ff = w1_ref.shape[1]
    starts = list(range(0, x_ref.shape[1], sub))
    rows = lambda ref, r0: ref[0] if ref.shape[1] == 1 else ref[0, r0:r0 + sub]
    hs = [(_rms(x_ref[0, r0:r0 + sub]) * gpre_ref[...] * (1.0 + rows(sc_ref, r0))
           + rows(sh_ref, r0)).astype(BF16) for r0 in starts]
    outs = [None] * len(starts)
    pending = [None] * len(starts)

    def down(i, c0, c1):
        part = _dot(pending[i], w2_ref[c0:c1, :])
        outs[i] = part if outs[i] is None else outs[i] + part

    prev = None
    for c0 in range(0, d_ff, f_chunk):
        c1 = min(c0 + f_chunk, d_ff)
        gated = []
        for i, h in enumerate(hs):
            a = _dot(h, w1_ref[:, c0:c1])
            b = _dot(h, w3_ref[:, c0:c1])
            gated.append((_silu(a) * b).astype(BF16))
        if prev is not None:
            for i in range(len(hs)):
                down(i, *prev)
        pending, prev = gated, (c0, c1)
    for i, r0 in enumerate(starts):
        down(i, *prev)
        o_ref[0, r0:r0 + sub] = (x_ref[0, r0:r0 + sub]
                                 + (coef * rows(gt_ref, r0)) * (_rms(outs[i]) * gpost_ref[...]))


def _mod_spec(mod, tm):
    if mod.shape[1] == 1:
        return pl.BlockSpec((1, 1, mod.shape[2]), lambda b, s: (b, 0, 0))
    return pl.BlockSpec((1, tm, mod.shape[2]), lambda b, s: (b, s, 0))


def _ffn(x, shift, scale, gate, g_pre, g_post, w1, w3, w2, l, k, tm):
    nseq, length, d = x.shape
    d_ff = w1.shape[-1]
    f_chunk = min(d_ff, FFN_CHUNK)
    w_up = _resident((None, None, d, d_ff), lambda b, s: (l, k, 0, 0))
    w_dn = _resident((None, None, d_ff, d), lambda b, s: (l, k, 0, 0))
    vec = pl.BlockSpec((1, d), lambda b, s: (0, 0))
    tile = pl.BlockSpec((1, tm, d), lambda b, s: (b, s, 0))
    return pl.pallas_call(
        functools.partial(_ffn_kernel, coef=0.5, f_chunk=f_chunk, sub=min(tm, FFN_SUB_ROWS)),
        grid=(nseq, length // tm),
        in_specs=[tile, _mod_spec(shift, tm), _mod_spec(scale, tm), _mod_spec(gate, tm),
                  vec, vec, w_up, w_up, w_dn],
        out_specs=tile,
        out_shape=jax.ShapeDtypeStruct(x.shape, F32),
        compiler_params=_params("parallel", "parallel"),
        name="ffn",
    )(x, shift, scale, gate, g_pre, g_post, w1, w3, w2)


def _proj_kernel(x_ref, sh_ref, sc_ref, gpre_ref, win_ref, qn_ref, kvn_ref, wuq_ref,
                 cq_ref, sq_ref, ck_ref, sk_ref, rc_ref, rsa_ref, rsb_ref, *rest, sub):
    q_out, k_out, vt_out, kv_out, kpe_out, rq_out, rk_out, rv_out, rg_out = rest[-9:]
    qsub = q_out.shape[3]
    c0 = Q_RANK + KV_RANK
    r0 = c0 + KPE_BLOCK
    n_abs = MLA_HEADS * KV_RANK
    n_pe = MLA_HEADS * MLA_ROPE

    starts = list(range(0, x_ref.shape[1], sub))
    rows = lambda ref, s0: ref[0] if ref.shape[1] == 1 else ref[0, s0:s0 + sub]

    def project(s0):
        h = (_rms(x_ref[0, s0:s0 + sub]) * gpre_ref[...] * (1.0 + rows(sc_ref, s0))
             + rows(sh_ref, s0)).astype(BF16)
        return _dot(h, win_ref[...])

    z_next = project(starts[0])
    for k, s0 in enumerate(starts):
        z = z_next
        if k + 1 < len(starts):
            z_next = project(starts[k + 1])
        rs = slice(s0, s0 + sub)
        q_lat = (_rms(z[:, :Q_RANK]) * qn_ref[...]).astype(BF16)
        kv_lat = _rms(z[:, Q_RANK:c0]) * kvn_ref[...]
        k_pe = z[:, c0:c0 + MLA_ROPE] * ck_ref[rs] + z[:, c0 + MLA_ROPE:c0 + 2 * MLA_ROPE] * sk_ref[rs]
        kv_out[0, rs] = kv_lat
        kpe_out[0, rs] = k_pe
        k_out[0, rs] = jnp.concatenate([kv_lat, k_pe], axis=-1).astype(BF16)
        ones = jnp.ones((VT_ROWS - KV_RANK, sub), F32)
        vt_w = vt_out.shape[3]
        vt_out[0, s0 // vt_w, :, s0 % vt_w:s0 % vt_w + sub] = (
            jnp.concatenate([kv_lat.T, ones], axis=0).astype(BF16))

        qq = _dot(q_lat, wuq_ref[...])
        q_abs = qq[:, :n_abs]
        q_pe = qq[:, n_abs:n_abs + n_pe] * cq_ref[rs] + qq[:, n_abs + n_pe:] * sq_ref[rs]
        for hd in range(MLA_HEADS):
            qh = jnp.concatenate([q_abs[:, hd * KV_RANK:(hd + 1) * KV_RANK],
                                  q_pe[:, hd * MLA_ROPE:(hd + 1) * MLA_ROPE]], axis=-1)
            qh = (qh * Q_SCALE).astype(BF16)
            for r in range(sub // qsub):
                q_out[0, s0 // qsub + r, hd] = qh[r * qsub:(r + 1) * qsub]

        rc, rsa, rsb = rc_ref[rs], rsa_ref[rs], rsb_ref[rs]
        for off, out, scale in ((r0, rq_out, 1.0), (r0 + RET_W, rk_out, RET_SCALE)):
            for c in range(RET_W // LANES):
                xc = z[:, off + c * LANES:off + (c + 1) * LANES]
                y = (xc * rc + pltpu.roll(xc, LANES - RET_DK // 2, 1) * rsa
                     + pltpu.roll(xc, RET_DK // 2, 1) * rsb)
                out[0, rs, c * LANES:(c + 1) * LANES] = (y * scale).astype(BF16)
        rv_out[0, rs] = z[:, r0 + 2 * RET_W:r0 + 3 * RET_W].astype(BF16)
        rg_out[0, rs] = z[:, r0 + 3 * RET_W:r0 + 4 * RET_W].astype(BF16)


def _proj(x, shift, scale, g_pre, w_in, q_norm, kv_norm, w_uq, tabs, l, tm, qsub, vt_w, caches):
    nseq, length, d = x.shape
    depth = w_in.shape[0]
    grid = (nseq, length // tm)
    vec = lambda n: pl.BlockSpec((1, n), lambda b, s: (0, 0))
    tile = lambda n: pl.BlockSpec((1, tm, n), lambda b, s: (b, s, 0))
    tab = lambda n: pl.BlockSpec((tm, n), lambda b, s: (s, 0))
    wl = lambda r, c: _resident((None, r, c), lambda b, s: (l, 0, 0))
    layer_tile = lambda n: pl.BlockSpec((None, 1, tm, n), lambda b, s: (l, b, s, 0))
    bf = lambda n: jax.ShapeDtypeStruct((nseq, length, n), BF16)
    in_specs = [tile(d), _mod_spec(shift, tm), _mod_spec(scale, tm), vec(d),
                wl(d, W_IN_COLS), vec(Q_RANK), vec(KV_RANK),
                wl(Q_RANK, w_uq.shape[-1]),
                tab(MLA_HEADS * MLA_ROPE), tab(MLA_HEADS * MLA_ROPE), tab(MLA_ROPE), tab(MLA_ROPE),
                tab(LANES), tab(LANES), tab(LANES)]
    args = [x, shift, scale, g_pre, w_in, q_norm, kv_norm, w_uq, *tabs]
    aliases = {}
    if caches is not None:
        aliases = {len(args): 3, len(args) + 1: 4}
        in_specs += [pl.BlockSpec(memory_space=pl.ANY)] * 2
        args += list(caches)
    return pl.pallas_call(
        functools.partial(_proj_kernel, sub=max(qsub, min(tm, MIX_SUB_ROWS))),
        grid=grid,
        in_specs=in_specs,
        out_specs=[pl.BlockSpec((1, tm // qsub, MLA_HEADS, qsub, QK_DIM), lambda b, s: (b, s, 0, 0, 0)),
                   tile(QK_DIM),
                   pl.BlockSpec((1, tm // vt_w, VT_ROWS, vt_w), lambda b, s: (b, s, 0, 0)),
                   layer_tile(KV_RANK), layer_tile(MLA_ROPE),
                   tile(RET_W), tile(RET_W), tile(RET_W), tile(RET_W)],
        out_shape=[jax.ShapeDtypeStruct((nseq, length // qsub, MLA_HEADS, qsub, QK_DIM), BF16),
                   bf(QK_DIM),
                   jax.ShapeDtypeStruct((nseq, length // vt_w, VT_ROWS, vt_w), BF16),
                   jax.ShapeDtypeStruct((depth, nseq, length, KV_RANK), F32),
                   jax.ShapeDtypeStruct((depth, nseq, length, MLA_ROPE), F32),
                   bf(RET_W), bf(RET_W), bf(RET_W), bf(RET_W)],
        input_output_aliases=aliases,
        compiler_params=_params("parallel", "parallel"),
        name="mixer_proj",
    )(*args)


def _rope_tables(pos):
    pos = np.asarray(pos, np.float32)[:, None]

    def cs(half):
        inv = (ROPE_BASE ** (-np.arange(half, dtype=np.float32) / half)).astype(np.float32)
        ang = pos * inv[None, :]
        return np.cos(ang), np.sin(ang)

    c16, s16 = cs(MLA_ROPE // 2)
    ck = np.concatenate([c16, c16], axis=-1)
    sk = np.concatenate([s16, s16], axis=-1)
    cq = np.tile(ck, (1, MLA_HEADS))
    sq = np.tile(sk, (1, MLA_HEADS))
    c32, s32 = cs(RET_DK // 2)
    zero = np.zeros_like(s32)
    reps = LANES // RET_DK
    rc = np.tile(np.concatenate([c32, c32], axis=-1), (1, reps))
    rsa = np.tile(np.concatenate([-s32, zero], axis=-1), (1, reps))
    rsb = np.tile(np.concatenate([zero, s32], axis=-1), (1, reps))
    return tuple(jnp.asarray(t, F32) for t in (cq, sq, ck, sk, rc, rsa, rsb))


def _attn_kernel(q_ref, k_ref, vt_ref, o_ref, m_scr, alpha_scr, p_scr, acc_scr, *, t, qs):
    i = pl.program_id(1)
    nq = pl.num_programs(1) - 1
    n_sub = t // qs
    cb = MLA_HEADS * qs
    rows = n_sub * cb
    subs = [(r, slice(r * cb, (r + 1) * cb)) for r in range(n_sub)]
    fine = [slice(c, c + FULL_TILE_COLS) for c in range(0, rows, FULL_TILE_COLS)]
    slot = i % 2
    q = q_ref[0].reshape(rows, QK_DIM)

    def k_tile(j):
        return k_ref[0, pl.ds(pl.multiple_of(j * t, t), t), :]

    def scores(j):
        return _dot_nt(k_tile(j), q)

    def scores_diag(j):
        kt = k_tile(j)
        return [_dot_nt(kt[:(r + 1) * qs], q[cs]) for r, cs in subs]

    def update(cs, sc):
        n, width = sc.shape
        m_prev = m_scr[:, cs]
        m_part = jnp.max(sc.reshape(MAX_GROUPS, n // MAX_GROUPS, width), axis=0)
        m_new = jnp.maximum(m_prev, jnp.max(m_part, axis=0, keepdims=True))
        alpha_scr[:, cs] = jnp.exp2(m_prev - m_new)
        p_scr[:n, cs] = jnp.exp2((sc - m_new).astype(BF16))
        m_scr[:, cs] = m_new

    def softmax(s):
        for cs in fine:
            update(cs, s[:, cs])

    def softmax_diag(parts):
        kc = lax.broadcasted_iota(jnp.int32, (qs, qs), 0) // CHUNK
        qc = lax.broadcasted_iota(jnp.int32, (qs, qs), 1) // CHUNK
        allowed = jnp.concatenate([kc <= qc] * MLA_HEADS, axis=1)
        for (r, cs), sc in zip(subs, parts):
            last = jnp.where(allowed, sc[r * qs:], NEG_INF)
            update(cs, last if r == 0 else jnp.concatenate([sc[:r * qs], last], axis=0))

    def values(j, acc_slot):
        for cs in fine:
            acc_scr[acc_slot, :, cs] = (acc_scr[acc_slot, :, cs] * alpha_scr[:, cs]
                                        + _dot(vt_ref[0, j], p_scr[:, cs]))

    def values_diag(j, acc_slot):
        for r, cs in subs:
            n = (r + 1) * qs
            acc_scr[acc_slot, :, cs] = (acc_scr[acc_slot, :, cs] * alpha_scr[:, cs]
                                        + _dot(vt_ref[0, j, :, :n], p_scr[:n, cs]))

    def begin():
        m_scr[...] = jnp.full(m_scr.shape, NEG_INF, F32)
        acc_scr[slot] = jnp.zeros(acc_scr.shape[1:], F32)

    def drain_previous():
        prev = 1 - slot
        values_diag(i - 1, prev)
        acc = acc_scr[prev]
        o_t = acc[:KV_RANK] * (1.0 / acc[KV_RANK:KV_RANK + 1])
        for r in range(n_sub):
            for hd in range(MLA_HEADS):
                c0 = (r * MLA_HEADS + hd) * qs
                o_ref[0, r * qs:(r + 1) * qs, hd * KV_RANK:(hd + 1) * KV_RANK] = (
                    o_t[:, c0:c0 + qs].T.astype(BF16))

    @pl.when(i == 0)
    def _():
        begin()
        softmax_diag(scores_diag(0))

    @pl.when(jnp.logical_and(i > 0, i < nq))
    def _():
        begin()
        s = scores(0)
        drain_previous()
        softmax(s)

        def body(j, carry):
            s = scores(j)
            values(j - 1, slot)
            softmax(s)
            return carry

        lax.fori_loop(1, i, body, 0)
        parts = scores_diag(i)
        values(i - 1, slot)
        softmax_diag(parts)

    @pl.when(i == nq)
    def _():
        drain_previous()


def _attention_prompt(q, k, vt, t):
    nseq, _, _, qs, _ = q.shape
    length = k.shape[1]
    rows = MLA_HEADS * t
    nq = length // t
    assert vt.shape[-1] == t and t % qs == 0 and qs % CHUNK == 0
    return pl.pallas_call(
        functools.partial(_attn_kernel, t=t, qs=qs),
        grid=(nseq, nq + 1),
        in_specs=[pl.BlockSpec((1, t // qs, MLA_HEADS, qs, QK_DIM),
                               lambda b, i: (b, jnp.minimum(i, nq - 1), 0, 0, 0)),
                  pl.BlockSpec((1, length, QK_DIM), lambda b, i: (b, 0, 0)),
                  pl.BlockSpec((1, nq, VT_ROWS, t), lambda b, i: (b, 0, 0, 0))],
        out_specs=pl.BlockSpec((1, t, MLA_HEADS * KV_RANK), lambda b, i: (b, jnp.maximum(i - 1, 0), 0)),
        out_shape=jax.ShapeDtypeStruct((nseq, length, MLA_HEADS * KV_RANK), BF16),
        scratch_shapes=[pltpu.VMEM((1, rows), F32), pltpu.VMEM((1, rows), F32),
                        pltpu.VMEM((t, rows), BF16), pltpu.VMEM((2, VT_ROWS, rows), F32)],
        compiler_params=_params("parallel", "arbitrary"),
        name="mla_attention_prompt",
    )(q, k, vt)


def _chunk_mask(s, t, q_start, k_start):
    n = s.shape[1]
    qpos = q_start + lax.broadcasted_iota(jnp.int32, (t, n), 0)
    kpos = k_start + lax.broadcasted_iota(jnp.int32, (t, n), 1)
    allowed = (kpos // CHUNK) <= (qpos // CHUNK)
    return jnp.where(allowed[None], s.reshape(MLA_HEADS, t, n), NEG_INF).reshape(s.shape)


def _attn_sample_kernel(q_ref, ckv_ref, ckpe_ref, kn_ref, o_ref, *, t, past, mask_cache, mask_new):
    rows = MLA_HEADS * t
    q = q_ref[0].reshape(rows, QK_DIM)
    ckv = ckv_ref[0].astype(BF16)
    ckpe = ckpe_ref[0].astype(BF16)
    kn = kn_ref[0]
    s_c = _dot_nt(q[:, :KV_RANK], ckv) + _dot_nt(q[:, KV_RANK:], ckpe)
    s_n = _dot_nt(q, kn)
    if mask_cache:
        s_c = _chunk_mask(s_c, t, past, 0)
    if mask_new:
        s_n = _chunk_mask(s_n, t, past, past)
    m = jnp.maximum(jnp.max(s_c, axis=1, keepdims=True), jnp.max(s_n, axis=1, keepdims=True))
    p_c = jnp.exp2(s_c - m)
    p_n = jnp.exp2(s_n - m)
    denom = jnp.sum(p_c, axis=1, keepdims=True) + jnp.sum(p_n, axis=1, keepdims=True)
    o = _dot(p_c.astype(BF16), ckv) + _dot(p_n.astype(BF16), kn[:, :KV_RANK])
    o = (o / denom).astype(BF16)
    for hd in range(MLA_HEADS):
        o_ref[0, :, hd * KV_RANK:(hd + 1) * KV_RANK] = o[hd * t:(hd + 1) * t]


def _attention_sample(q, cache_kv, cache_kpe, k_new, l):
    nb, _, t, _ = q.shape
    past = cache_kv.shape[2]
    q_chunk = (past + np.arange(t)) // CHUNK
    k_chunk = np.arange(past + t) // CHUNK
    allowed = k_chunk[None, :] <= q_chunk[:, None]
    mask_cache = not allowed[:, :past].all()
    mask_new = not allowed[:, past:].all()
    return pl.pallas_call(
        functools.partial(_attn_sample_kernel, t=t, past=past, mask_cache=mask_cache, mask_new=mask_new),
        grid=(nb,),
        in_specs=[pl.BlockSpec((1, MLA_HEADS, t, QK_DIM), lambda b: (b, 0, 0, 0)),
                  pl.BlockSpec((None, 1, past, KV_RANK), lambda b: (l, b, 0, 0)),
                  pl.BlockSpec((None, 1, past, MLA_ROPE), lambda b: (l, b, 0, 0)),
                  pl.BlockSpec((1, t, QK_DIM), lambda b: (b, 0, 0))],
        out_specs=pl.BlockSpec((1, t, MLA_HEADS * KV_RANK), lambda b: (b, 0, 0)),
        out_shape=jax.ShapeDtypeStruct((nb, t, MLA_HEADS * KV_RANK), BF16),
        compiler_params=_params("parallel"),
        name="mla_attention_sample",
    )(q, cache_kv, cache_kpe, k_new)


def _ret_kernel(rq_ref, rk_ref, rv_ref, s0_ref, dec_ref, cross_ref, kdec_ref, gt_ref,
                o_ref, sout_ref, st_scr):
    c = pl.program_id(1)

    @pl.when(c == 0)
    def _():
        st_scr[...] = s0_ref[0]

    heads = range(RET_HEADS)
    t = dec_ref.shape[1]
    chunks = [slice(r, r + t) for r in range(0, rq_ref.shape[1], t)]
    ks = [slice(hd * RET_DK, (hd + 1) * RET_DK) for hd in heads]
    vs = [slice(hd * RET_DV, (hd + 1) * RET_DV) for hd in heads]
    qs = [[rq_ref[0, rows, ks[hd]] for hd in heads] for rows in chunks]
    kk = [[rk_ref[0, rows, ks[hd]] for hd in heads] for rows in chunks]
    vv = [[rv_ref[0, rows, vs[hd]] for hd in heads] for rows in chunks]
    scores = [[_dot_nt(q, k) for q, k in zip(qc, kc)] for qc, kc in zip(qs, kk)]
    st = [st_scr[hd] for hd in heads]
    for n, rows in enumerate(chunks):
        cross = [_dot(qs[n][hd], st[hd].astype(BF16)) for hd in heads]
        st_next = []
        for hd in heads:
            k_dec = (kk[n][hd].astype(F32) * kdec_ref[hd]).astype(BF16)
            st_next.append(gt_ref[hd] * st[hd] + _dot_tn(k_dec, vv[n][hd]))
        for hd in heads:
            sc = (scores[n][hd] * dec_ref[hd]).astype(BF16)
            o_ref[0, rows, vs[hd]] = (_dot(sc, vv[n][hd]) + cross[hd] * cross_ref[hd]).astype(BF16)
        st = st_next
    for hd in heads:
        st_scr[hd] = st[hd]

    @pl.when(c == pl.num_programs(1) - 1)
    def _():
        sout_ref[0] = st_scr[...]


def _ret_tables(t):
    f32 = np.float32
    log_g = np.log1p(-np.exp2(-5.0 - np.arange(RET_HEADS, dtype=f32))).astype(f32)
    n = np.arange(t, dtype=f32)
    diff = n[:, None] - n[None, :]
    dec = np.where(diff >= 0.0, np.exp(np.maximum(diff, 0.0)[None] * log_g[:, None, None]), 0.0)
    cross = np.exp((n + 1.0)[None, :] * log_g[:, None])
    kdec = np.exp((t - 1.0 - n)[None, :] * log_g[:, None])
    gt = np.exp(f32(t) * log_g)
    cross = np.broadcast_to(cross[:, :, None], (RET_HEADS, t, RET_DV))
    kdec = np.broadcast_to(kdec[:, :, None], (RET_HEADS, t, RET_DK))
    gt = np.broadcast_to(gt[:, None, None], (RET_HEADS, RET_DK, RET_DV))
    return tuple(jnp.asarray(a, F32) for a in (dec, cross, kdec, gt))


def _retention(rq, rk, rv, state0, tabs, t):
    nseq, length, _ = rq.shape
    assert t % tabs[0].shape[1] == 0
    tile = pl.BlockSpec((1, t, RET_W), lambda b, c: (b, c, 0))
    st = pl.BlockSpec((1, RET_HEADS, RET_DK, RET_DV), lambda b, c: (b, 0, 0, 0))
    const = lambda a: pl.BlockSpec(a.shape, lambda b, c: (0, 0, 0))
    return pl.pallas_call(
        _ret_kernel,
        grid=(nseq, length // t),
        in_specs=[tile, tile, tile, st] + [const(a) for a in tabs],
        out_specs=[tile, st],
        out_shape=[jax.ShapeDtypeStruct((nseq, length, RET_W), BF16),
                   jax.ShapeDtypeStruct(state0.shape, F32)],
        scratch_shapes=[pltpu.VMEM((RET_HEADS, RET_DK, RET_DV), F32)],
        compiler_params=_params("parallel", "arbitrary"),
        name="retention",
    )(rq, rk, rv, state0, *tabs)


def _out_kernel(x_ref, ol_ref, or_ref, rg_ref, gt_ref, gpost_ref, wuv_ref, wo_ref, grp_ref, o_ref, *, sub):
    n_mla = MLA_HEADS * MLA_V
    starts = list(range(0, x_ref.shape[1], sub))
    tiles = [slice(s0, s0 + sub) for s0 in starts]
    o_mla = [_dot(ol_ref[0, rs], wuv_ref[...]).astype(BF16) for rs in tiles]
    mix = [_dot(om, wo_ref[:n_mla, :]) for om in o_mla]
    o = [or_ref[0, rs].astype(F32) for rs in tiles]
    ms = [_dot((v * v).astype(BF16), grp_ref[...]) for v in o]
    o_ret = [(v * lax.rsqrt(m + EPS) * _silu(rg_ref[0, rs].astype(F32))).astype(BF16)
             for v, m, rs in zip(o, ms, tiles)]
    for i, rs in enumerate(tiles):
        gate = gt_ref[0] if gt_ref.shape[1] == 1 else gt_ref[0, rs]
        total = mix[i] + _dot(o_ret[i], wo_ref[n_mla:, :])
        o_ref[0, rs] = x_ref[0, rs] + gate * (_rms(total) * gpost_ref[...])


def _mixer_out(x, o_lat, o_ret, rg, gate, g_post, w_uv, w_out, l, tm):
    nseq, length, d = x.shape
    tile = lambda n: pl.BlockSpec((1, tm, n), lambda b, s: (b, s, 0))
    wl = lambda r, c: _resident((None, r, c), lambda b, s: (l, 0, 0))
    head = np.arange(RET_W) // RET_DV
    group_mean = jnp.asarray((head[:, None] == head[None, :]).astype(np.float32) / RET_DV, BF16)
    return pl.pallas_call(
        functools.partial(_out_kernel, sub=min(tm, MIX_SUB_ROWS)),
        grid=(nseq, length // tm),
        in_specs=[tile(d), tile(o_lat.shape[-1]), tile(RET_W), tile(RET_W), _mod_spec(gate, tm),
                  pl.BlockSpec((1, d), lambda b, s: (0, 0)),
                  wl(*w_uv.shape[1:]), wl(*w_out.shape[1:]),
                  _resident((RET_W, RET_W), lambda b, s: (0, 0))],
        out_specs=tile(d),
        out_shape=jax.ShapeDtypeStruct(x.shape, F32),
        compiler_params=_params("parallel", "parallel"),
        name="mixer_out",
    )(x, o_lat, o_ret, rg, gate, g_post, w_uv, w_out, group_mean)


def _prep_w_in(w_in):
    c0 = Q_RANK + KV_RANK
    half = MLA_ROPE // 2
    kpe = w_in[..., c0:c0 + MLA_ROPE]
    kpe_rot = jnp.concatenate([-kpe[..., half:], kpe[..., :half]], axis=-1)
    pad = jnp.zeros(w_in.shape[:-1] + (KPE_BLOCK - 2 * MLA_ROPE,), w_in.dtype)
    return jnp.concatenate([w_in[..., :c0], kpe, kpe_rot, pad, w_in[..., c0 + MLA_ROPE:]],
                           axis=-1).astype(BF16)


def _absorb_kernel(uq_ref, uk_ref, o_ref):
    o_ref[0, 0] = lax.dot_general(uq_ref[0, 0], uk_ref[0, 0], (((1,), (1,)), ((), ())),
                                  preferred_element_type=F32, precision=lax.Precision.HIGHEST)


def _prep_w_uq(w_uq, w_uk):
    depth = w_uq.shape[0]
    half = MLA_ROPE // 2
    w = w_uq.reshape(depth, Q_RANK, MLA_HEADS, MLA_NOPE + MLA_ROPE)
    nope = w[..., :MLA_NOPE].transpose(0, 2, 1, 3)
    uk = w_uk.reshape(depth, KV_RANK, MLA_HEADS, MLA_NOPE).transpose(0, 2, 1, 3)
    absorbed = pl.pallas_call(
        _absorb_kernel,
        grid=(depth, MLA_HEADS),
        in_specs=[pl.BlockSpec((1, 1, Q_RANK, MLA_NOPE), lambda l, h: (l, h, 0, 0)),
                  pl.BlockSpec((1, 1, KV_RANK, MLA_NOPE), lambda l, h: (l, h, 0, 0))],
        out_specs=pl.BlockSpec((1, 1, Q_RANK, KV_RANK), lambda l, h: (l, h, 0, 0)),
        out_shape=jax.ShapeDtypeStruct((depth, MLA_HEADS, Q_RANK, KV_RANK), F32),
        compiler_params=_params("parallel", "parallel"),
        name="absorb_w_uk",
    )(nope, uk)
    absorbed = absorbed.transpose(0, 2, 1, 3).reshape(depth, Q_RANK, MLA_HEADS * KV_RANK)
    pe = w[..., MLA_NOPE:]
    rot = jnp.concatenate([-pe[..., half:], pe[..., :half]], axis=-1)
    flat = lambda a: a.reshape(depth, Q_RANK, MLA_HEADS * MLA_ROPE)
    return jnp.concatenate([absorbed, flat(pe), flat(rot)], axis=-1).astype(BF16)


def _prep_w_uv(w_uv):
    depth = w_uv.shape[0]
    w = w_uv.reshape(depth, KV_RANK, MLA_HEADS, MLA_V)
    eye = jnp.eye(MLA_HEADS, dtype=w_uv.dtype)
    bd = jnp.einsum('lrhv,hg->lhrgv', w, eye)
    return bd.reshape(depth, MLA_HEADS * KV_RANK, MLA_HEADS * MLA_V).astype(BF16)


def _pick_tile(length, target):
    t = min(length, target)
    while length % t:
        t //= 2
    return t


def kernel(x_prompt, x_sample, cache_kv_latent, cache_k_rope, state_ret, c_prompt, c_sample,
           ada_w, ada_b, norm_pre, norm_post, ffn_w1, ffn_w3, ffn_w2,
           w_in, q_norm, kv_norm, w_uq, w_uk, w_uv, w_out):
    depth = ada_w.shape[0]
    nb_p, seq, d = x_prompt.shape
    nb_s, dec_seq, _ = x_sample.shape
    past = cache_kv_latent.shape[2]
    n_tok_s = nb_s * dec_seq

    w1, w3, w2 = ffn_w1.astype(BF16), ffn_w3.astype(BF16), ffn_w2.astype(BF16)
    w_in_b = _prep_w_in(w_in)
    w_uq_b = _prep_w_uq(w_uq, w_uk)
    w_uv_b = _prep_w_uv(w_uv)
    w_out_b = w_out.astype(BF16)

    mods = _ada_mods(jnp.concatenate([c_prompt, c_sample], axis=0), ada_w, ada_b)

    t_att = _pick_tile(seq, 512)
    tm_p = _pick_tile(seq, 4 * MIX_SUB_ROWS)
    tm_ffn = _pick_tile(seq, 2 * FFN_SUB_ROWS)
    t_ret = _pick_tile(seq, 256)
    ret_step = _pick_tile(seq, 2 * t_ret)
    tabs_p = _rope_tables(np.arange(seq))
    tabs_s = _rope_tables(np.tile(past + np.arange(dec_seq), nb_s))
    ret_tabs_p = _ret_tables(t_ret)
    ret_tabs_s = _ret_tables(dec_seq)
    zero_state = jnp.zeros((nb_p, RET_HEADS, RET_DK, RET_DV), F32)

    def mod_p(l, r):
        return mods[l, :nb_p, r][:, None, :]

    def mod_s(l, r):
        return jnp.repeat(mods[l, nb_p:, r], dec_seq, axis=0)[None]

    y_p = x_prompt
    y_s = x_sample.reshape(1, n_tok_s, d)
    caches_p, caches_s = None, None
    ret_p, ret_s = [], []

    for l in range(depth):
        gp = lambda j: norm_pre[l, j][None, :]
        go = lambda j: norm_post[l, j][None, :]
        qn, kvn = q_norm[l][None, :], kv_norm[l][None, :]

        y_p = _ffn(y_p, mod_p(l, 0), mod_p(l, 1), mod_p(l, 2), gp(0), go(0), w1, w3, w2, l, 0, tm_ffn)
        q, k, vt, kv, kpe, rq, rk, rv, rg = _proj(y_p, mod_p(l, 3), mod_p(l, 4), gp(1), w_in_b, qn, kvn,
                                                  w_uq_b, tabs_p, l, tm_p, Q_SUB, t_att, caches_p)
        caches_p = (kv, kpe)
        o_lat = _attention_prompt(q, k, vt, t_att)
        o_ret, r_fin = _retention(rq, rk, rv, zero_state, ret_tabs_p, ret_step)
        y_p = _mixer_out(y_p, o_lat, o_ret, rg, mod_p(l, 5), go(1), w_uv_b, w_out_b, l, tm_p)
        y_p = _ffn(y_p, mod_p(l, 6), mod_p(l, 7), mod_p(l, 8), gp(2), go(2), w1, w3, w2, l, 1, tm_ffn)
        ret_p.append(r_fin)

        y_s = _ffn(y_s, mod_s(l, 0), mod_s(l, 1), mod_s(l, 2), gp(0), go(0), w1, w3, w2, l, 0, n_tok_s)
        q, k, _, kv, kpe, rq, rk, rv, rg = _proj(y_s, mod_s(l, 3), mod_s(l, 4), gp(1), w_in_b, qn, kvn,
                                                 w_uq_b, tabs_s, l, n_tok_s, dec_seq, n_tok_s, caches_s)
        caches_s = (kv, kpe)
        per_seq = lambda a: a.reshape(nb_s, dec_seq, a.shape[-1])
        q = q.reshape(nb_s, MLA_HEADS, dec_seq, QK_DIM)
        o_lat = _attention_sample(q, cache_kv_latent, cache_k_rope, per_seq(k), l)
        o_ret, r_new = _retention(per_seq(rq), per_seq(rk), per_seq(rv),
                                  state_ret[l], ret_tabs_s, dec_seq)
        y_s = _mixer_out(y_s, o_lat.reshape(1, n_tok_s, -1), o_ret.reshape(1, n_tok_s, -1), rg,
                         mod_s(l, 5), go(1), w_uv_b, w_out_b, l, n_tok_s)
        y_s = _ffn(y_s, mod_s(l, 6), mod_s(l, 7), mod_s(l, 8), gp(2), go(2), w1, w3, w2, l, 1, n_tok_s)
        ret_s.append(r_new)

    per_layer_seq = lambda a: a.reshape(depth, nb_s, dec_seq, a.shape[-1])
    return (y_p, y_s.reshape(nb_s, dec_seq, d),
            caches_p[0], caches_p[1], jnp.stack(ret_p),
            per_layer_seq(caches_s[0]), per_layer_seq(caches_s[1]), jnp.stack(ret_s))
```

```python
import functools

import jax
import jax.numpy as jnp
import numpy as np
from jax import lax
from jax.experimental import pallas as pl
from jax.experimental.pallas import tpu as pltpu

CHUNK = 64
N_SUB = 3
MLA_HEADS = 8
MLA_NOPE = 64
MLA_ROPE = 32
MLA_V = 64
Q_RANK = 256
KV_RANK = 128
RET_HEADS = 8
RET_DK = 64
RET_DV = 64
ROPE_BASE = 10000.0
EPS = 1e-6
MLA_SCALE = (MLA_NOPE + MLA_ROPE) ** -0.5
Q_SCALE = MLA_SCALE * float(np.log2(np.e))
MAX_GROUPS = 8
MXU_DIM = 256
FFN_CHUNK = 3 * MXU_DIM
FFN_SUB_ROWS = 512
Q_SUB = 128
MIX_SUB_ROWS = 256
FULL_TILE_COLS = 256
RET_SCALE = RET_DK ** -0.5
NEG_INF = -1e30

QK_DIM = KV_RANK + MLA_ROPE
RET_W = RET_HEADS * RET_DK
LANES = 128
KPE_BLOCK = LANES
BF16_SUBLANES = 16
VT_ROWS = KV_RANK + BF16_SUBLANES
W_IN_COLS = Q_RANK + KV_RANK + KPE_BLOCK + 4 * RET_W

F32 = jnp.float32
BF16 = jnp.bfloat16

VMEM_LIMIT = 56 * 1024 * 1024


def _params(*sem):
    return pltpu.CompilerParams(dimension_semantics=sem, vmem_limit_bytes=VMEM_LIMIT)


def _resident(block_shape, index_map):
    return pl.BlockSpec(block_shape, index_map, pipeline_mode=pl.Buffered(1))


def _rms(x):
    return x * lax.rsqrt(jnp.mean(x * x, axis=-1, keepdims=True) + EPS)


def _silu(x):
    return x * jax.nn.sigmoid(x)


def _dot(a, b):
    return jnp.dot(a, b, preferred_element_type=F32)


def _dot_nt(a, b):
    return lax.dot_general(a, b, (((1,), (1,)), ((), ())), preferred_element_type=F32)


def _dot_tn(a, b):
    return lax.dot_general(a, b, (((0,), (0,)), ((), ())), preferred_element_type=F32)


def _ada_kernel(c_ref, w_ref, b_ref, o_ref):
    sc = _silu(c_ref[...])
    o_ref[0] = jnp.dot(sc, w_ref[0], preferred_element_type=F32,
                       precision=lax.Precision.HIGHEST) + b_ref[0]


def _ada_mods(c_all, ada_w, ada_b):
    depth, d, cols = ada_w.shape
    nb = c_all.shape[0]
    n_col = cols // d
    out = pl.pallas_call(
        _ada_kernel,
        grid=(depth, n_col),
        in_specs=[
            pl.BlockSpec((nb, d), lambda l, j: (0, 0)),
            pl.BlockSpec((1, d, d), lambda l, j: (l, 0, j)),
            pl.BlockSpec((1, 1, d), lambda l, j: (l, 0, j)),
        ],
        out_specs=pl.BlockSpec((1, nb, d), lambda l, j: (l, 0, j)),
        out_shape=jax.ShapeDtypeStruct((depth, nb, cols), F32),
        compiler_params=_params("parallel", "parallel"),
        name="ada_mods",
    )(c_all, ada_w, ada_b.reshape(depth, 1, cols))
    return out.reshape(depth, nb, n_col, d)


def _ffn_kernel(x_ref, sh_ref, sc_ref, gt_ref, gpre_ref, gpost_ref, w1_ref, w3_ref, w2_ref, o_ref,
                *, coef, f_chunk, sub):
    d_ff = w1_ref.shape[1]
    starts = list(range(0, x_ref.shape[1], sub))
    rows = lambda ref, r0: ref[0] if ref.shape[1] == 1 else ref[0, r0:r0 + sub]
    hs = [(_rms(x_ref[0, r0:r0 + sub]) * gpre_ref[...] * (1.0 + rows(sc_ref, r0))
           + rows(sh_ref, r0)).astype(BF16) for r0 in starts]
    outs = [None] * len(starts)
    pending = [None] * len(starts)

    def down(i, c0, c1):
        part = _dot(pending[i], w2_ref[c0:c1, :])
        outs[i] = part if outs[i] is None else outs[i] + part

    prev = None
    for c0 in range(0, d_ff, f_chunk):
        c1 = min(c0 + f_chunk, d_ff)
        gated = []
        for i, h in enumerate(hs):
            a = _dot(h, w1_ref[:, c0:c1])
            b = _dot(h, w3_ref[:, c0:c1])
            gated.append((_silu(a) * b).astype(BF16))
        if prev is not None:
            for i in range(len(hs)):
                down(i, *prev)
        pending, prev = gated, (c0, c1)
    for i, r0 in enumerate(starts):
        down(i, *prev)
        o_ref[0, r0:r0 + sub] = (x_ref[0, r0:r0 + sub]
                                 + (coef * rows(gt_ref, r0)) * (_rms(outs[i]) * gpost_ref[...]))


def _mod_spec(mod, tm):
    if mod.shape[1] == 1:
        return pl.BlockSpec((1, 1, mod.shape[2]), lambda b, s: (b, 0, 0))
    return pl.BlockSpec((1, tm, mod.shape[2]), lambda b, s: (b, s, 0))


def _ffn(x, shift, scale, gate, g_pre, g_post, w1, w3, w2, l, k, tm):
    nseq, length, d = x.shape
    d_ff = w1.shape[-1]
    f_chunk = min(d_ff, FFN_CHUNK)
    w_up = _resident((None, None, d, d_ff), lambda b, s: (l, k, 0, 0))
    w_dn = _resident((None, None, d_ff, d), lambda b, s: (l, k, 0, 0))
    vec = pl.BlockSpec((1, d), lambda b, s: (0, 0))
    tile = pl.BlockSpec((1, tm, d), lambda b, s: (b, s, 0))
    return pl.pallas_call(
        functools.partial(_ffn_kernel, coef=0.5, f_chunk=f_chunk, sub=min(tm, FFN_SUB_ROWS)),
        grid=(nseq, length // tm),
        in_specs=[tile, _mod_spec(shift, tm), _mod_spec(scale, tm), _mod_spec(gate, tm),
                  vec, vec, w_up, w_up, w_dn],
        out_specs=tile,
        out_shape=jax.ShapeDtypeStruct(x.shape, F32),
        compiler_params=_params("parallel", "parallel"),
        name="ffn",
    )(x, shift, scale, gate, g_pre, g_post, w1, w3, w2)


def _proj_kernel(x_ref, sh_ref, sc_ref, gpre_ref, win_ref, qn_ref, kvn_ref, wuq_ref,
                 cq_ref, sq_ref, ck_ref, sk_ref, ckt_ref, skt_ref, rc_ref, rsa_ref, rsb_ref, *rest, sub):
    q_out, k_out, vt_out, kv_out, kpe_out, rq_out, rk_out, rv_out, rg_out = rest[-9:]
    qsub = q_out.shape[3]
    c0 = Q_RANK + KV_RANK
    r0 = c0 + KPE_BLOCK
    n_abs = MLA_HEADS * KV_RANK
    n_pe = MLA_HEADS * MLA_ROPE

    starts = list(range(0, x_ref.shape[1], sub))
    rows = lambda ref, s0: ref[0] if ref.shape[1] == 1 else ref[0, s0:s0 + sub]

    def project(s0):
        h = (_rms(x_ref[0, s0:s0 + sub]) * gpre_ref[...] * (1.0 + rows(sc_ref, s0))
             + rows(sh_ref, s0)).astype(BF16)
        return _dot(h, win_ref[...])

    z_next = project(starts[0])
    for k, s0 in enumerate(starts):
        z = z_next
        if k + 1 < len(starts):
            z_next = project(starts[k + 1])
        rs = slice(s0, s0 + sub)
        q_lat = (_rms(z[:, :Q_RANK]) * qn_ref[...]).astype(BF16)
        kv_lat = _rms(z[:, Q_RANK:c0]) * kvn_ref[...]
        k_pe = z[:, c0:c0 + MLA_ROPE] * ck_ref[rs] + z[:, c0 + MLA_ROPE:c0 + 2 * MLA_ROPE] * sk_ref[rs]
        kv_out[0, rs] = kv_lat
        blk_t = z[:, c0:c0 + KPE_BLOCK].T
        kpe_out[0, :, rs] = (blk_t[:MLA_ROPE] * ckt_ref[:, rs]
                             + blk_t[MLA_ROPE:2 * MLA_ROPE] * skt_ref[:, rs])
        k_out[0, rs] = jnp.concatenate([kv_lat, k_pe], axis=-1).astype(BF16)
        ones = jnp.ones((VT_ROWS - KV_RANK, sub), F32)
        vt_w = vt_out.shape[3]
        vt_out[0, s0 // vt_w, :, s0 % vt_w:s0 % vt_w + sub] = (
            jnp.concatenate([kv_lat.T, ones], axis=0).astype(BF16))

        qq = _dot(q_lat, wuq_ref[...])
        q_abs = qq[:, :n_abs]
        q_pe = qq[:, n_abs:n_abs + n_pe] * cq_ref[rs] + qq[:, n_abs + n_pe:] * sq_ref[rs]
        for hd in range(MLA_HEADS):
            qh = jnp.concatenate([q_abs[:, hd * KV_RANK:(hd + 1) * KV_RANK],
                                  q_pe[:, hd * MLA_ROPE:(hd + 1) * MLA_ROPE]], axis=-1)
            qh = (qh * Q_SCALE).astype(BF16)
            for r in range(sub // qsub):
                q_out[0, s0 // qsub + r, hd] = qh[r * qsub:(r + 1) * qsub]

        rc, rsa, rsb = rc_ref[rs], rsa_ref[rs], rsb_ref[rs]
        for off, out, scale in ((r0, rq_out, 1.0), (r0 + RET_W, rk_out, RET_SCALE)):
            for c in range(RET_W // LANES):
                xc = z[:, off + c * LANES:off + (c + 1) * LANES]
                y = (xc * rc + pltpu.roll(xc, LANES - RET_DK // 2, 1) * rsa
                     + pltpu.roll(xc, RET_DK // 2, 1) * rsb)
                out[0, rs, c * LANES:(c + 1) * LANES] = (y * scale).astype(BF16)
        rv_out[0, rs] = z[:, r0 + 2 * RET_W:r0 + 3 * RET_W].astype(BF16)
        rg_out[0, rs] = z[:, r0 + 3 * RET_W:r0 + 4 * RET_W].astype(BF16)


def _proj(x, shift, scale, g_pre, w_in, q_norm, kv_norm, w_uq, tabs, l, tm, qsub, vt_w, caches):
    nseq, length, d = x.shape
    depth = w_in.shape[0]
    grid = (nseq, length // tm)
    vec = lambda n: pl.BlockSpec((1, n), lambda b, s: (0, 0))
    tile = lambda n: pl.BlockSpec((1, tm, n), lambda b, s: (b, s, 0))
    tab = lambda n: pl.BlockSpec((tm, n), lambda b, s: (s, 0))
    wl = lambda r, c: _resident((None, r, c), lambda b, s: (l, 0, 0))
    layer_tile = lambda n: pl.BlockSpec((None, 1, tm, n), lambda b, s: (l, b, s, 0))
    bf = lambda n: jax.ShapeDtypeStruct((nseq, length, n), BF16)
    in_specs = [tile(d), _mod_spec(shift, tm), _mod_spec(scale, tm), vec(d),
                wl(d, W_IN_COLS), vec(Q_RANK), vec(KV_RANK),
                wl(Q_RANK, w_uq.shape[-1]),
                tab(MLA_HEADS * MLA_ROPE), tab(MLA_HEADS * MLA_ROPE), tab(MLA_ROPE), tab(MLA_ROPE),
                pl.BlockSpec((MLA_ROPE, tm), lambda b, s: (0, s)),
                pl.BlockSpec((MLA_ROPE, tm), lambda b, s: (0, s)),
                tab(LANES), tab(LANES), tab(LANES)]
    args = [x, shift, scale, g_pre, w_in, q_norm, kv_norm, w_uq, *tabs]
    aliases = {}
    if caches is not None:
        aliases = {len(args): 3, len(args) + 1: 4}
        in_specs += [pl.BlockSpec(memory_space=pl.ANY)] * 2
        args += list(caches)
    return pl.pallas_call(
        functools.partial(_proj_kernel, sub=max(qsub, min(tm, MIX_SUB_ROWS))),
        grid=grid,
        in_specs=in_specs,
        out_specs=[pl.BlockSpec((1, tm // qsub, MLA_HEADS, qsub, QK_DIM), lambda b, s: (b, s, 0, 0, 0)),
                   tile(QK_DIM),
                   pl.BlockSpec((1, tm // vt_w, VT_ROWS, vt_w), lambda b, s: (b, s, 0, 0)),
                   layer_tile(KV_RANK),
                   pl.BlockSpec((None, 1, MLA_ROPE, tm), lambda b, s: (l, b, 0, s)),
                   tile(RET_W), tile(RET_W), tile(RET_W), tile(RET_W)],
        out_shape=[jax.ShapeDtypeStruct((nseq, length // qsub, MLA_HEADS, qsub, QK_DIM), BF16),
                   bf(QK_DIM),
                   jax.ShapeDtypeStruct((nseq, length // vt_w, VT_ROWS, vt_w), BF16),
                   jax.ShapeDtypeStruct((depth, nseq, length, KV_RANK), F32),
                   jax.ShapeDtypeStruct((depth, nseq, MLA_ROPE, length), F32),
                   bf(RET_W), bf(RET_W), bf(RET_W), bf(RET_W)],
        input_output_aliases=aliases,
        compiler_params=_params("parallel", "parallel"),
        name="mixer_proj",
    )(*args)


def _rope_tables(pos):
    pos = np.asarray(pos, np.float32)[:, None]

    def cs(half):
        inv = (ROPE_BASE ** (-np.arange(half, dtype=np.float32) / half)).astype(np.float32)
        ang = pos * inv[None, :]
        return np.cos(ang), np.sin(ang)

    c16, s16 = cs(MLA_ROPE // 2)
    ck = np.concatenate([c16, c16], axis=-1)
    sk = np.concatenate([s16, s16], axis=-1)
    cq = np.tile(ck, (1, MLA_HEADS))
    sq = np.tile(sk, (1, MLA_HEADS))
    c32, s32 = cs(RET_DK // 2)
    zero = np.zeros_like(s32)
    reps = LANES // RET_DK
    rc = np.tile(np.concatenate([c32, c32], axis=-1), (1, reps))
    rsa = np.tile(np.concatenate([-s32, zero], axis=-1), (1, reps))
    rsb = np.tile(np.concatenate([zero, s32], axis=-1), (1, reps))
    return tuple(jnp.asarray(t, F32) for t in (cq, sq, ck, sk, ck.T, sk.T, rc, rsa, rsb))


def _attn_kernel(q_ref, k_ref, vt_ref, o_ref, m_scr, alpha_scr, p_scr, acc_scr, *, t, qs):
    i = pl.program_id(1)
    nq = pl.num_programs(1) - 1
    n_sub = t // qs
    cb = MLA_HEADS * qs
    rows = n_sub * cb
    subs = [(r, slice(r * cb, (r + 1) * cb)) for r in range(n_sub)]
    fine = [slice(c, c + FULL_TILE_COLS) for c in range(0, rows, FULL_TILE_COLS)]
    slot = i % 2
    q = q_ref[0].reshape(rows, QK_DIM)

    def k_tile(j):
        return k_ref[0, pl.ds(pl.multiple_of(j * t, t), t), :]

    def scores(j):
        return _dot_nt(k_tile(j), q)

    def scores_diag(j):
        kt = k_tile(j)
        return [_dot_nt(kt[:(r + 1) * qs], q[cs]) for r, cs in subs]

    def update(cs, sc):
        n, width = sc.shape
        m_prev = m_scr[:, cs]
        m_part = jnp.max(sc.reshape(MAX_GROUPS, n // MAX_GROUPS, width), axis=0)
        m_new = jnp.maximum(m_prev, jnp.max(m_part, axis=0, keepdims=True))
        alpha_scr[:, cs] = jnp.exp2(m_prev - m_new)
        p_scr[:n, cs] = jnp.exp2((sc - m_new).astype(BF16))
        m_scr[:, cs] = m_new

    def softmax(s):
        for cs in fine:
            update(cs, s[:, cs])

    def softmax_diag(parts):
        kc = lax.broadcasted_iota(jnp.int32, (qs, qs), 0) // CHUNK
        qc = lax.broadcasted_iota(jnp.int32, (qs, qs), 1) // CHUNK
        allowed = jnp.concatenate([kc <= qc] * MLA_HEADS, axis=1)
        for (r, cs), sc in zip(subs, parts):
            last = jnp.where(allowed, sc[r * qs:], NEG_INF)
            update(cs, last if r == 0 else jnp.concatenate([sc[:r * qs], last], axis=0))

    def values(j, acc_slot):
        for cs in fine:
            acc_scr[acc_slot, :, cs] = (acc_scr[acc_slot, :, cs] * alpha_scr[:, cs]
                                        + _dot(vt_ref[0, j], p_scr[:, cs]))

    def values_diag(j, acc_slot):
        for r, cs in subs:
            n = (r + 1) * qs
            acc_scr[acc_slot, :, cs] = (acc_scr[acc_slot, :, cs] * alpha_scr[:, cs]
                                        + _dot(vt_ref[0, j, :, :n], p_scr[:n, cs]))

    def begin():
        m_scr[...] = jnp.full(m_scr.shape, NEG_INF, F32)
        acc_scr[slot] = jnp.zeros(acc_scr.shape[1:], F32)

    def drain_previous():
        prev = 1 - slot
        values_diag(i - 1, prev)
        acc = acc_scr[prev]
        o_t = acc[:KV_RANK] * (1.0 / acc[KV_RANK:KV_RANK + 1])
        for r in range(n_sub):
            for hd in range(MLA_HEADS):
                c0 = (r * MLA_HEADS + hd) * qs
                o_ref[0, r * qs:(r + 1) * qs, hd * KV_RANK:(hd + 1) * KV_RANK] = (
                    o_t[:, c0:c0 + qs].T.astype(BF16))

    @pl.when(i == 0)
    def _():
        begin()
        softmax_diag(scores_diag(0))

    @pl.when(jnp.logical_and(i > 0, i < nq))
    def _():
        begin()
        s = scores(0)
        drain_previous()
        softmax(s)

        def body(j, carry):
            s = scores(j)
            values(j - 1, slot)
            softmax(s)
            return carry

        lax.fori_loop(1, i, body, 0)
        parts = scores_diag(i)
        values(i - 1, slot)
        softmax_diag(parts)

    @pl.when(i == nq)
    def _():
        drain_previous()


def _attention_prompt(q, k, vt, t):
    nseq, _, _, qs, _ = q.shape
    length = k.shape[1]
    rows = MLA_HEADS * t
    nq = length // t
    assert vt.shape[-1] == t and t % qs == 0 and qs % CHUNK == 0
    return pl.pallas_call(
        functools.partial(_attn_kernel, t=t, qs=qs),
        grid=(nseq, nq + 1),
        in_specs=[pl.BlockSpec((1, t // qs, MLA_HEADS, qs, QK_DIM),
                               lambda b, i: (b, jnp.minimum(i, nq - 1), 0, 0, 0)),
                  pl.BlockSpec((1, length, QK_DIM), lambda b, i: (b, 0, 0)),
                  pl.BlockSpec((1, nq, VT_ROWS, t), lambda b, i: (b, 0, 0, 0))],
        out_specs=pl.BlockSpec((1, t, MLA_HEADS * KV_RANK), lambda b, i: (b, jnp.maximum(i - 1, 0), 0)),
        out_shape=jax.ShapeDtypeStruct((nseq, length, MLA_HEADS * KV_RANK), BF16),
        scratch_shapes=[pltpu.VMEM((1, rows), F32), pltpu.VMEM((1, rows), F32),
                        pltpu.VMEM((t, rows), BF16), pltpu.VMEM((2, VT_ROWS, rows), F32)],
        compiler_params=_params("parallel", "arbitrary"),
        name="mla_attention_prompt",
    )(q, k, vt)


def _chunk_mask(s, t, q_start, k_start):
    n = s.shape[1]
    qpos = q_start + lax.broadcasted_iota(jnp.int32, (t, n), 0)
    kpos = k_start + lax.broadcasted_iota(jnp.int32, (t, n), 1)
    allowed = (kpos // CHUNK) <= (qpos // CHUNK)
    return jnp.where(allowed[None], s.reshape(MLA_HEADS, t, n), NEG_INF).reshape(s.shape)


def _attn_sample_kernel(q_ref, ckv_ref, ckpe_ref, kn_ref, o_ref, *, t, past, mask_cache, mask_new):
    rows = MLA_HEADS * t
    q = q_ref[0].reshape(rows, QK_DIM)
    ckv = ckv_ref[0].astype(BF16)
    ckpe_t = ckpe_ref[0].astype(BF16)
    kn = kn_ref[0]
    s_c = _dot_nt(q[:, :KV_RANK], ckv) + _dot(q[:, KV_RANK:], ckpe_t)
    s_n = _dot_nt(q, kn)
    if mask_cache:
        s_c = _chunk_mask(s_c, t, past, 0)
    if mask_new:
        s_n = _chunk_mask(s_n, t, past, past)
    m = jnp.maximum(jnp.max(s_c, axis=1, keepdims=True), jnp.max(s_n, axis=1, keepdims=True))
    p_c = jnp.exp2(s_c - m)
    p_n = jnp.exp2(s_n - m)
    denom = jnp.sum(p_c, axis=1, keepdims=True) + jnp.sum(p_n, axis=1, keepdims=True)
    o = _dot(p_c.astype(BF16), ckv) + _dot(p_n.astype(BF16), kn[:, :KV_RANK])
    o = (o / denom).astype(BF16)
    for hd in range(MLA_HEADS):
        o_ref[0, :, hd * KV_RANK:(hd + 1) * KV_RANK] = o[hd * t:(hd + 1) * t]


def _attention_sample(q, cache_kv, cache_kpe_t, k_new, l):
    nb, _, t, _ = q.shape
    past = cache_kv.shape[2]
    q_chunk = (past + np.arange(t)) // CHUNK
    k_chunk = np.arange(past + t) // CHUNK
    allowed = k_chunk[None, :] <= q_chunk[:, None]
    mask_cache = not allowed[:, :past].all()
    mask_new = not allowed[:, past:].all()
    return pl.pallas_call(
        functools.partial(_attn_sample_kernel, t=t, past=past, mask_cache=mask_cache, mask_new=mask_new),
        grid=(nb,),
        in_specs=[pl.BlockSpec((1, MLA_HEADS, t, QK_DIM), lambda b: (b, 0, 0, 0)),
                  pl.BlockSpec((None, 1, past, KV_RANK), lambda b: (l, b, 0, 0)),
                  pl.BlockSpec((None, 1, MLA_ROPE, past), lambda b: (l, b, 0, 0)),
                  pl.BlockSpec((1, t, QK_DIM), lambda b: (b, 0, 0))],
        out_specs=pl.BlockSpec((1, t, MLA_HEADS * KV_RANK), lambda b: (b, 0, 0)),
        out_shape=jax.ShapeDtypeStruct((nb, t, MLA_HEADS * KV_RANK), BF16),
        compiler_params=_params("parallel"),
        name="mla_attention_sample",
    )(q, cache_kv, cache_kpe_t, k_new)


def _ret_kernel(rq_ref, rk_ref, rv_ref, s0_ref, dec_ref, cross_ref, kdec_ref, gt_ref,
                o_ref, sout_ref, st_scr):
    c = pl.program_id(1)

    @pl.when(c == 0)
    def _():
        st_scr[...] = s0_ref[0]

    heads = range(RET_HEADS)
    t = dec_ref.shape[1]
    chunks = [slice(r, r + t) for r in range(0, rq_ref.shape[1], t)]
    ks = [slice(hd * RET_DK, (hd + 1) * RET_DK) for hd in heads]
    vs = [slice(hd * RET_DV, (hd + 1) * RET_DV) for hd in heads]
    qs = [[rq_ref[0, rows, ks[hd]] for hd in heads] for rows in chunks]
    kk = [[rk_ref[0, rows, ks[hd]] for hd in heads] for rows in chunks]
    vv = [[rv_ref[0, rows, vs[hd]] for hd in heads] for rows in chunks]
    scores = [[_dot_nt(q, k) for q, k in zip(qc, kc)] for qc, kc in zip(qs, kk)]
    st = [st_scr[hd] for hd in heads]
    for n, rows in enumerate(chunks):
        cross = [_dot(qs[n][hd], st[hd].astype(BF16)) for hd in heads]
        st_next = []
        for hd in heads:
            k_dec = (kk[n][hd].astype(F32) * kdec_ref[hd]).astype(BF16)
            st_next.append(gt_ref[hd] * st[hd] + _dot_tn(k_dec, vv[n][hd]))
        for hd in heads:
            sc = (scores[n][hd] * dec_ref[hd]).astype(BF16)
            o_ref[0, rows, vs[hd]] = (_dot(sc, vv[n][hd]) + cross[hd] * cross_ref[hd]).astype(BF16)
        st = st_next
    for hd in heads:
        st_scr[hd] = st[hd]

    @pl.when(c == pl.num_programs(1) - 1)
    def _():
        sout_ref[0] = st_scr[...]


def _ret_tables(t):
    f32 = np.float32
    log_g = np.log1p(-np.exp2(-5.0 - np.arange(RET_HEADS, dtype=f32))).astype(f32)
    n = np.arange(t, dtype=f32)
    diff = n[:, None] - n[None, :]
    dec = np.where(diff >= 0.0, np.exp(np.maximum(diff, 0.0)[None] * log_g[:, None, None]), 0.0)
    cross = np.exp((n + 1.0)[None, :] * log_g[:, None])
    kdec = np.exp((t - 1.0 - n)[None, :] * log_g[:, None])
    gt = np.exp(f32(t) * log_g)
    cross = np.broadcast_to(cross[:, :, None], (RET_HEADS, t, RET_DV))
    kdec = np.broadcast_to(kdec[:, :, None], (RET_HEADS, t, RET_DK))
    gt = np.broadcast_to(gt[:, None, None], (RET_HEADS, RET_DK, RET_DV))
    return tuple(jnp.asarray(a, F32) for a in (dec, cross, kdec, gt))


def _retention(rq, rk, rv, state0, tabs, t):
    nseq, length, _ = rq.shape
    assert t % tabs[0].shape[1] == 0
    tile = pl.BlockSpec((1, t, RET_W), lambda b, c: (b, c, 0))
    st = pl.BlockSpec((1, RET_HEADS, RET_DK, RET_DV), lambda b, c: (b, 0, 0, 0))
    const = lambda a: pl.BlockSpec(a.shape, lambda b, c: (0, 0, 0))
    return pl.pallas_call(
        _ret_kernel,
        grid=(nseq, length // t),
        in_specs=[tile, tile, tile, st] + [const(a) for a in tabs],
        out_specs=[tile, st],
        out_shape=[jax.ShapeDtypeStruct((nseq, length, RET_W), BF16),
                   jax.ShapeDtypeStruct(state0.shape, F32)],
        scratch_shapes=[pltpu.VMEM((RET_HEADS, RET_DK, RET_DV), F32)],
        compiler_params=_params("parallel", "arbitrary"),
        name="retention",
    )(rq, rk, rv, state0, *tabs)


def _out_kernel(x_ref, ol_ref, or_ref, rg_ref, gt_ref, gpost_ref, wuv_ref, wo_ref, grp_ref, o_ref, *, sub):
    n_mla = MLA_HEADS * MLA_V
    starts = list(range(0, x_ref.shape[1], sub))
    tiles = [slice(s0, s0 + sub) for s0 in starts]
    o_mla = [_dot(ol_ref[0, rs], wuv_ref[...]).astype(BF16) for rs in tiles]
    mix = [_dot(om, wo_ref[:n_mla, :]) for om in o_mla]
    o = [or_ref[0, rs].astype(F32) for rs in tiles]
    ms = [_dot((v * v).astype(BF16), grp_ref[...]) for v in o]
    o_ret = [(v * lax.rsqrt(m + EPS) * _silu(rg_ref[0, rs].astype(F32))).astype(BF16)
             for v, m, rs in zip(o, ms, tiles)]
    for i, rs in enumerate(tiles):
        gate = gt_ref[0] if gt_ref.shape[1] == 1 else gt_ref[0, rs]
        total = mix[i] + _dot(o_ret[i], wo_ref[n_mla:, :])
        o_ref[0, rs] = x_ref[0, rs] + gate * (_rms(total) * gpost_ref[...])


def _mixer_out(x, o_lat, o_ret, rg, gate, g_post, w_uv, w_out, l, tm):
    nseq, length, d = x.shape
    tile = lambda n: pl.BlockSpec((1, tm, n), lambda b, s: (b, s, 0))
    wl = lambda r, c: _resident((None, r, c), lambda b, s: (l, 0, 0))
    head = np.arange(RET_W) // RET_DV
    group_mean = jnp.asarray((head[:, None] == head[None, :]).astype(np.float32) / RET_DV, BF16)
    return pl.pallas_call(
        functools.partial(_out_kernel, sub=min(tm, MIX_SUB_ROWS)),
        grid=(nseq, length // tm),
        in_specs=[tile(d), tile(o_lat.shape[-1]), tile(RET_W), tile(RET_W), _mod_spec(gate, tm),
                  pl.BlockSpec((1, d), lambda b, s: (0, 0)),
                  wl(*w_uv.shape[1:]), wl(*w_out.shape[1:]),
                  _resident((RET_W, RET_W), lambda b, s: (0, 0))],
        out_specs=tile(d),
        out_shape=jax.ShapeDtypeStruct(x.shape, F32),
        compiler_params=_params("parallel", "parallel"),
        name="mixer_out",
    )(x, o_lat, o_ret, rg, gate, g_post, w_uv, w_out, group_mean)


def _prep_w_in(w_in):
    c0 = Q_RANK + KV_RANK
    half = MLA_ROPE // 2
    kpe = w_in[..., c0:c0 + MLA_ROPE]
    kpe_rot = jnp.concatenate([-kpe[..., half:], kpe[..., :half]], axis=-1)
    pad = jnp.zeros(w_in.shape[:-1] + (KPE_BLOCK - 2 * MLA_ROPE,), w_in.dtype)
    return jnp.concatenate([w_in[..., :c0], kpe, kpe_rot, pad, w_in[..., c0 + MLA_ROPE:]],
                           axis=-1).astype(BF16)


def _absorb_kernel(uq_ref, uk_ref, o_ref):
    o_ref[0, 0] = lax.dot_general(uq_ref[0, 0], uk_ref[0, 0], (((1,), (1,)), ((), ())),
                                  preferred_element_type=F32, precision=lax.Precision.HIGHEST)


def _prep_w_uq(w_uq, w_uk):
    depth = w_uq.shape[0]
    half = MLA_ROPE // 2
    w = w_uq.reshape(depth, Q_RANK, MLA_HEADS, MLA_NOPE + MLA_ROPE)
    nope = w[..., :MLA_NOPE].transpose(0, 2, 1, 3)
    uk = w_uk.reshape(depth, KV_RANK, MLA_HEADS, MLA_NOPE).transpose(0, 2, 1, 3)
    absorbed = pl.pallas_call(
        _absorb_kernel,
        grid=(depth, MLA_HEADS),
        in_specs=[pl.BlockSpec((1, 1, Q_RANK, MLA_NOPE), lambda l, h: (l, h, 0, 0)),
                  pl.BlockSpec((1, 1, KV_RANK, MLA_NOPE), lambda l, h: (l, h, 0, 0))],
        out_specs=pl.BlockSpec((1, 1, Q_RANK, KV_RANK), lambda l, h: (l, h, 0, 0)),
        out_shape=jax.ShapeDtypeStruct((depth, MLA_HEADS, Q_RANK, KV_RANK), F32),
        compiler_params=_params("parallel", "parallel"),
        name="absorb_w_uk",
    )(nope, uk)
    absorbed = absorbed.transpose(0, 2, 1, 3).reshape(depth, Q_RANK, MLA_HEADS * KV_RANK)
    pe = w[..., MLA_NOPE:]
    rot = jnp.concatenate([-pe[..., half:], pe[..., :half]], axis=-1)
    flat = lambda a: a.reshape(depth, Q_RANK, MLA_HEADS * MLA_ROPE)
    return jnp.concatenate([absorbed, flat(pe), flat(rot)], axis=-1).astype(BF16)


def _prep_w_uv(w_uv):
    depth = w_uv.shape[0]
    w = w_uv.reshape(depth, KV_RANK, MLA_HEADS, MLA_V)
    eye = jnp.eye(MLA_HEADS, dtype=w_uv.dtype)
    bd = jnp.einsum('lrhv,hg->lhrgv', w, eye)
    return bd.reshape(depth, MLA_HEADS * KV_RANK, MLA_HEADS * MLA_V).astype(BF16)


def _pick_tile(length, target):
    t = min(length, target)
    while length % t:
        t //= 2
    return t


def kernel(x_prompt, x_sample, cache_kv_latent, cache_k_rope, state_ret, c_prompt, c_sample,
           ada_w, ada_b, norm_pre, norm_post, ffn_w1, ffn_w3, ffn_w2,
           w_in, q_norm, kv_norm, w_uq, w_uk, w_uv, w_out):
    depth = ada_w.shape[0]
    nb_p, seq, d = x_prompt.shape
    nb_s, dec_seq, _ = x_sample.shape
    past = cache_kv_latent.shape[2]
    n_tok_s = nb_s * dec_seq

    w1, w3, w2 = ffn_w1.astype(BF16), ffn_w3.astype(BF16), ffn_w2.astype(BF16)
    w_in_b = _prep_w_in(w_in)
    w_uq_b = _prep_w_uq(w_uq, w_uk)
    w_uv_b = _prep_w_uv(w_uv)
    w_out_b = w_out.astype(BF16)

    mods = _ada_mods(jnp.concatenate([c_prompt, c_sample], axis=0), ada_w, ada_b)

    t_att = _pick_tile(seq, 512)
    tm_p = _pick_tile(seq, 4 * MIX_SUB_ROWS)
    tm_ffn = _pick_tile(seq, 2 * FFN_SUB_ROWS)
    t_ret = _pick_tile(seq, 256)
    ret_step = _pick_tile(seq, 2 * t_ret)
    tabs_p = _rope_tables(np.arange(seq))
    tabs_s = _rope_tables(np.tile(past + np.arange(dec_seq), nb_s))
    ret_tabs_p = _ret_tables(t_ret)
    ret_tabs_s = _ret_tables(dec_seq)
    zero_state = jnp.zeros((nb_p, RET_HEADS, RET_DK, RET_DV), F32)
    cache_kpe_t = jnp.swapaxes(cache_k_rope, 2, 3)

    def mod_p(l, r):
        return mods[l, :nb_p, r][:, None, :]

    def mod_s(l, r):
        return jnp.repeat(mods[l, nb_p:, r], dec_seq, axis=0)[None]

    y_p = x_prompt
    y_s = x_sample.reshape(1, n_tok_s, d)
    caches_p, caches_s = None, None
    ret_p, ret_s = [], []

    for l in range(depth):
        gp = lambda j: norm_pre[l, j][None, :]
        go = lambda j: norm_post[l, j][None, :]
        qn, kvn = q_norm[l][None, :], kv_norm[l][None, :]

        y_p = _ffn(y_p, mod_p(l, 0), mod_p(l, 1), mod_p(l, 2), gp(0), go(0), w1, w3, w2, l, 0, tm_ffn)
        q, k, vt, kv, kpe, rq, rk, rv, rg = _proj(y_p, mod_p(l, 3), mod_p(l, 4), gp(1), w_in_b, qn, kvn,
                                                  w_uq_b, tabs_p, l, tm_p, Q_SUB, t_att, caches_p)
        caches_p = (kv, kpe)
        o_lat = _attention_prompt(q, k, vt, t_att)
        o_ret, r_fin = _retention(rq, rk, rv, zero_state, ret_tabs_p, ret_step)
        y_p = _mixer_out(y_p, o_lat, o_ret, rg, mod_p(l, 5), go(1), w_uv_b, w_out_b, l, tm_p)
        y_p = _ffn(y_p, mod_p(l, 6), mod_p(l, 7), mod_p(l, 8), gp(2), go(2), w1, w3, w2, l, 1, tm_ffn)
        ret_p.append(r_fin)

        y_s = _ffn(y_s, mod_s(l, 0), mod_s(l, 1), mod_s(l, 2), gp(0), go(0), w1, w3, w2, l, 0, n_tok_s)
        q, k, _, kv, kpe, rq, rk, rv, rg = _proj(y_s, mod_s(l, 3), mod_s(l, 4), gp(1), w_in_b, qn, kvn,
                                                 w_uq_b, tabs_s, l, n_tok_s, dec_seq, n_tok_s, caches_s)
        caches_s = (kv, kpe)
        per_seq = lambda a: a.reshape(nb_s, dec_seq, a.shape[-1])
        q = q.reshape(nb_s, MLA_HEADS, dec_seq, QK_DIM)
        o_lat = _attention_sample(q, cache_kv_latent, cache_kpe_t, per_seq(k), l)
        o_ret, r_new = _retention(per_seq(rq), per_seq(rk), per_seq(rv),
                                  state_ret[l], ret_tabs_s, dec_seq)
        y_s = _mixer_out(y_s, o_lat.reshape(1, n_tok_s, -1), o_ret.reshape(1, n_tok_s, -1), rg,
                         mod_s(l, 5), go(1), w_uv_b, w_out_b, l, n_tok_s)
        y_s = _ffn(y_s, mod_s(l, 6), mod_s(l, 7), mod_s(l, 8), gp(2), go(2), w1, w3, w2, l, 1, n_tok_s)
        ret_s.append(r_new)

    per_layer_seq = lambda a: a.reshape(depth, nb_s, dec_seq, a.shape[-1])
    tokens_major = lambda a: jnp.swapaxes(a, 2, 3)
    return (y_p, y_s.reshape(nb_s, dec_seq, d),
            caches_p[0], tokens_major(caches_p[1]), jnp.stack(ret_p),
            per_layer_seq(caches_s[0]), per_layer_seq(tokens_major(caches_s[1])), jnp.stack(ret_s))
```

```python
import functools

import jax
import jax.numpy as jnp
import numpy as np
from jax import lax
from jax.experimental import pallas as pl
from jax.experimental.pallas import tpu as pltpu

CHUNK = 64
N_SUB = 3
MLA_HEADS = 8
MLA_NOPE = 64
MLA_ROPE = 32
MLA_V = 64
Q_RANK = 256
KV_RANK = 128
RET_HEADS = 8
RET_DK = 64
RET_DV = 64
ROPE_BASE = 10000.0
EPS = 1e-6
MLA_SCALE = (MLA_NOPE + MLA_ROPE) ** -0.5
Q_SCALE = MLA_SCALE * float(np.log2(np.e))
MAX_GROUPS = 8
MXU_DIM = 256
FFN_CHUNK = 3 * MXU_DIM
FFN_SUB_ROWS = 256
Q_SUB = 128
MIX_SUB_ROWS = 256
FULL_TILE_COLS = 256
RET_SCALE = RET_DK ** -0.5
NEG_INF = -1e30

QK_DIM = KV_RANK + MLA_ROPE
RET_W = RET_HEADS * RET_DK
LANES = 128
KPE_BLOCK = LANES
BF16_SUBLANES = 16
VT_ROWS = KV_RANK + BF16_SUBLANES
W_IN_COLS = Q_RANK + KV_RANK + KPE_BLOCK + 4 * RET_W

F32 = jnp.float32
BF16 = jnp.bfloat16

VMEM_LIMIT = 56 * 1024 * 1024


def _params(*sem):
    return pltpu.CompilerParams(dimension_semantics=sem, vmem_limit_bytes=VMEM_LIMIT)


def _resident(block_shape, index_map):
    return pl.BlockSpec(block_shape, index_map, pipeline_mode=pl.Buffered(1))


def _rms(x):
    return x * lax.rsqrt(jnp.mean(x * x, axis=-1, keepdims=True) + EPS)


def _silu(x):
    return x * jax.nn.sigmoid(x)


def _dot(a, b):
    return jnp.dot(a, b, preferred_element_type=F32)


def _dot_nt(a, b):
    return lax.dot_general(a, b, (((1,), (1,)), ((), ())), preferred_element_type=F32)


def _dot_tn(a, b):
    return lax.dot_general(a, b, (((0,), (0,)), ((), ())), preferred_element_type=F32)


def _ada_kernel(c_ref, w_ref, b_ref, o_ref):
    sc = _silu(c_ref[...])
    o_ref[0] = jnp.dot(sc, w_ref[0], preferred_element_type=F32,
                       precision=lax.Precision.HIGHEST) + b_ref[0]


def _ada_mods(c_all, ada_w, ada_b):
    depth, d, cols = ada_w.shape
    nb = c_all.shape[0]
    n_col = cols // d
    out = pl.pallas_call(
        _ada_kernel,
        grid=(depth, n_col),
        in_specs=[
            pl.BlockSpec((nb, d), lambda l, j: (0, 0)),
            pl.BlockSpec((1, d, d), lambda l, j: (l, 0, j)),
            pl.BlockSpec((1, 1, d), lambda l, j: (l, 0, j)),
        ],
        out_specs=pl.BlockSpec((1, nb, d), lambda l, j: (l, 0, j)),
        out_shape=jax.ShapeDtypeStruct((depth, nb, cols), F32),
        compiler_params=_params("parallel", "parallel"),
        name="ada_mods",
    )(c_all, ada_w, ada_b.reshape(depth, 1, cols))
    return out.reshape(depth, nb, n_col, d)


def _ffn_kernel(x_ref, sh_ref, sc_ref, gt_ref, gpre_ref, gpost_ref, w1_ref, w3_ref, w2_ref, o_ref,
                *, coef, f_chunk, sub):
    d_ff = w1_ref.shape[1]
    starts = list(range(0, x_ref.shape[1], sub))
    rows = lambda ref, r0: ref[0] if ref.shape[1] == 1 else ref[0, r0:r0 + sub]
    hs = [(_rms(x_ref[0, r0:r0 + sub]) * gpre_ref[...] * (1.0 + rows(sc_ref, r0))
           + rows(sh_ref, r0)).astype(BF16) for r0 in starts]
    outs = [None] * len(starts)
    pending = [None] * len(starts)

    def down(i, c0, c1):
        part = _dot(pending[i], w2_ref[c0:c1, :])
        outs[i] = part if outs[i] is None else outs[i] + part

    prev = None
    for c0 in range(0, d_ff, f_chunk):
        c1 = min(c0 + f_chunk, d_ff)
        gated = []
        for i, h in enumerate(hs):
            a = _dot(h, w1_ref[:, c0:c1])
            b = _dot(h, w3_ref[:, c0:c1])
            gated.append((_silu(a) * b).astype(BF16))
        if prev is not None:
            for i in range(len(hs)):
                down(i, *prev)
        pending, prev = gated, (c0, c1)
    for i, r0 in enumerate(starts):
        down(i, *prev)
        o_ref[0, r0:r0 + sub] = (x_ref[0, r0:r0 + sub]
                                 + (coef * rows(gt_ref, r0)) * (_rms(outs[i]) * gpost_ref[...]))


def _mod_spec(mod, tm):
    if mod.shape[1] == 1:
        return pl.BlockSpec((1, 1, mod.shape[2]), lambda b, s: (b, 0, 0))
    return pl.BlockSpec((1, tm, mod.shape[2]), lambda b, s: (b, s, 0))


def _ffn(x, shift, scale, gate, g_pre, g_post, w1, w3, w2, l, k, tm):
    nseq, length, d = x.shape
    d_ff = w1.shape[-1]
    f_chunk = min(d_ff, FFN_CHUNK)
    w_up = _resident((None, None, d, d_ff), lambda b, s: (l, k, 0, 0))
    w_dn = _resident((None, None, d_ff, d), lambda b, s: (l, k, 0, 0))
    vec = pl.BlockSpec((1, d), lambda b, s: (0, 0))
    tile = pl.BlockSpec((1, tm, d), lambda b, s: (b, s, 0))
    return pl.pallas_call(
        functools.partial(_ffn_kernel, coef=0.5, f_chunk=f_chunk, sub=min(tm, FFN_SUB_ROWS)),
        grid=(nseq, length // tm),
        in_specs=[tile, _mod_spec(shift, tm), _mod_spec(scale, tm), _mod_spec(gate, tm),
                  vec, vec, w_up, w_up, w_dn],
        out_specs=tile,
        out_shape=jax.ShapeDtypeStruct(x.shape, F32),
        compiler_params=_params("parallel", "parallel"),
        name="ffn",
    )(x, shift, scale, gate, g_pre, g_post, w1, w3, w2)


def _proj_kernel(x_ref, sh_ref, sc_ref, gpre_ref, win_ref, qn_ref, kvn_ref, wuq_ref,
                 cq_ref, sq_ref, ck_ref, sk_ref, ckt_ref, skt_ref, rc_ref, rsa_ref, rsb_ref, *rest, sub):
    q_out, k_out, vt_out, kv_out, kpe_out, rq_out, rk_out, rv_out, rg_out = rest[-9:]
    qsub = q_out.shape[3]
    c0 = Q_RANK + KV_RANK
    r0 = c0 + KPE_BLOCK
    n_abs = MLA_HEADS * KV_RANK
    n_pe = MLA_HEADS * MLA_ROPE

    starts = list(range(0, x_ref.shape[1], sub))
    rows = lambda ref, s0: ref[0] if ref.shape[1] == 1 else ref[0, s0:s0 + sub]

    def project(s0):
        h = (_rms(x_ref[0, s0:s0 + sub]) * gpre_ref[...] * (1.0 + rows(sc_ref, s0))
             + rows(sh_ref, s0)).astype(BF16)
        return _dot(h, win_ref[...])

    z_next = project(starts[0])
    for k, s0 in enumerate(starts):
        z = z_next
        if k + 1 < len(starts):
            z_next = project(starts[k + 1])
        rs = slice(s0, s0 + sub)
        q_lat = (_rms(z[:, :Q_RANK]) * qn_ref[...]).astype(BF16)
        kv_lat = _rms(z[:, Q_RANK:c0]) * kvn_ref[...]
        k_pe = z[:, c0:c0 + MLA_ROPE] * ck_ref[rs] + z[:, c0 + MLA_ROPE:c0 + 2 * MLA_ROPE] * sk_ref[rs]
        kv_out[0, rs] = kv_lat
        blk_t = z[:, c0:c0 + KPE_BLOCK].T
        kpe_out[0, :, rs] = (blk_t[:MLA_ROPE] * ckt_ref[:, rs]
                             + blk_t[MLA_ROPE:2 * MLA_ROPE] * skt_ref[:, rs])
        k_out[0, rs] = jnp.concatenate([kv_lat, k_pe], axis=-1).astype(BF16)
        ones = jnp.ones((VT_ROWS - KV_RANK, sub), F32)
        vt_w = vt_out.shape[3]
        vt_out[0, s0 // vt_w, :, s0 % vt_w:s0 % vt_w + sub] = (
            jnp.concatenate([kv_lat.T, ones], axis=0).astype(BF16))

        qq = _dot(q_lat, wuq_ref[...])
        q_abs = qq[:, :n_abs]
        q_pe = qq[:, n_abs:n_abs + n_pe] * cq_ref[rs] + qq[:, n_abs + n_pe:] * sq_ref[rs]
        for hd in range(MLA_HEADS):
            qh = jnp.concatenate([q_abs[:, hd * KV_RANK:(hd + 1) * KV_RANK],
                                  q_pe[:, hd * MLA_ROPE:(hd + 1) * MLA_ROPE]], axis=-1)
            qh = (qh * Q_SCALE).astype(BF16)
            for r in range(sub // qsub):
                q_out[0, s0 // qsub + r, hd] = qh[r * qsub:(r + 1) * qsub]

        rc, rsa, rsb = rc_ref[rs], rsa_ref[rs], rsb_ref[rs]
        for off, out, scale in ((r0, rq_out, 1.0), (r0 + RET_W, rk_out, RET_SCALE)):
            for c in range(RET_W // LANES):
                xc = z[:, off + c * LANES:off + (c + 1) * LANES]
                y = (xc * rc + pltpu.roll(xc, LANES - RET_DK // 2, 1) * rsa
                     + pltpu.roll(xc, RET_DK // 2, 1) * rsb)
                out[0, rs, c * LANES:(c + 1) * LANES] = (y * scale).astype(BF16)
        rv_out[0, rs] = z[:, r0 + 2 * RET_W:r0 + 3 * RET_W].astype(BF16)
        rg_out[0, rs] = z[:, r0 + 3 * RET_W:r0 + 4 * RET_W].astype(BF16)


def _proj(x, shift, scale, g_pre, w_in, q_norm, kv_norm, w_uq, tabs, l, tm, qsub, vt_w, caches):
    nseq, length, d = x.shape
    depth = w_in.shape[0]
    grid = (nseq, length // tm)
    vec = lambda n: pl.BlockSpec((1, n), lambda b, s: (0, 0))
    tile = lambda n: pl.BlockSpec((1, tm, n), lambda b, s: (b, s, 0))
    tab = lambda n: pl.BlockSpec((tm, n), lambda b, s: (s, 0))
    wl = lambda r, c: _resident((None, r, c), lambda b, s: (l, 0, 0))
    layer_tile = lambda n: pl.BlockSpec((None, 1, tm, n), lambda b, s: (l, b, s, 0))
    bf = lambda n: jax.ShapeDtypeStruct((nseq, length, n), BF16)
    in_specs = [tile(d), _mod_spec(shift, tm), _mod_spec(scale, tm), vec(d),
                wl(d, W_IN_COLS), vec(Q_RANK), vec(KV_RANK),
                wl(Q_RANK, w_uq.shape[-1]),
                tab(MLA_HEADS * MLA_ROPE), tab(MLA_HEADS * MLA_ROPE), tab(MLA_ROPE), tab(MLA_ROPE),
                pl.BlockSpec((MLA_ROPE, tm), lambda b, s: (0, s)),
                pl.BlockSpec((MLA_ROPE, tm), lambda b, s: (0, s)),
                tab(LANES), tab(LANES), tab(LANES)]
    args = [x, shift, scale, g_pre, w_in, q_norm, kv_norm, w_uq, *tabs]
    aliases = {}
    if caches is not None:
        aliases = {len(args): 3, len(args) + 1: 4}
        in_specs += [pl.BlockSpec(memory_space=pl.ANY)] * 2
        args += list(caches)
    return pl.pallas_call(
        functools.partial(_proj_kernel, sub=max(qsub, min(tm, MIX_SUB_ROWS))),
        grid=grid,
        in_specs=in_specs,
        out_specs=[pl.BlockSpec((1, tm // qsub, MLA_HEADS, qsub, QK_DIM), lambda b, s: (b, s, 0, 0, 0)),
                   tile(QK_DIM),
                   pl.BlockSpec((1, tm // vt_w, VT_ROWS, vt_w), lambda b, s: (b, s, 0, 0)),
                   layer_tile(KV_RANK),
                   pl.BlockSpec((None, 1, MLA_ROPE, tm), lambda b, s: (l, b, 0, s)),
                   tile(RET_W), tile(RET_W), tile(RET_W), tile(RET_W)],
        out_shape=[jax.ShapeDtypeStruct((nseq, length // qsub, MLA_HEADS, qsub, QK_DIM), BF16),
                   bf(QK_DIM),
                   jax.ShapeDtypeStruct((nseq, length // vt_w, VT_ROWS, vt_w), BF16),
                   jax.ShapeDtypeStruct((depth, nseq, length, KV_RANK), F32),
                   jax.ShapeDtypeStruct((depth, nseq, MLA_ROPE, length), F32),
                   bf(RET_W), bf(RET_W), bf(RET_W), bf(RET_W)],
        input_output_aliases=aliases,
        compiler_params=_params("parallel", "parallel"),
        name="mixer_proj",
    )(*args)


def _rope_tables(pos):
    pos = np.asarray(pos, np.float32)[:, None]

    def cs(half):
        inv = (ROPE_BASE ** (-np.arange(half, dtype=np.float32) / half)).astype(np.float32)
        ang = pos * inv[None, :]
        return np.cos(ang), np.sin(ang)

    c16, s16 = cs(MLA_ROPE // 2)
    ck = np.concatenate([c16, c16], axis=-1)
    sk = np.concatenate([s16, s16], axis=-1)
    cq = np.tile(ck, (1, MLA_HEADS))
    sq = np.tile(sk, (1, MLA_HEADS))
    c32, s32 = cs(RET_DK // 2)
    zero = np.zeros_like(s32)
    reps = LANES // RET_DK
    rc = np.tile(np.concatenate([c32, c32], axis=-1), (1, reps))
    rsa = np.tile(np.concatenate([-s32, zero], axis=-1), (1, reps))
    rsb = np.tile(np.concatenate([zero, s32], axis=-1), (1, reps))
    return tuple(jnp.asarray(t, F32) for t in (cq, sq, ck, sk, ck.T, sk.T, rc, rsa, rsb))


def _attn_kernel(q_ref, k_ref, vt_ref, o_ref, m_scr, alpha_scr, p_scr, acc_scr, *, t, qs):
    i = pl.program_id(1)
    nq = pl.num_programs(1) - 1
    n_sub = t // qs
    cb = MLA_HEADS * qs
    rows = n_sub * cb
    subs = [(r, slice(r * cb, (r + 1) * cb)) for r in range(n_sub)]
    fine = [slice(c, c + FULL_TILE_COLS) for c in range(0, rows, FULL_TILE_COLS)]
    slot = i % 2
    q = q_ref[0].reshape(rows, QK_DIM)

    def k_tile(j):
        return k_ref[0, pl.ds(pl.multiple_of(j * t, t), t), :]

    def scores(j):
        return _dot_nt(k_tile(j), q)

    def scores_diag(j):
        kt = k_tile(j)
        return [_dot_nt(kt[:(r + 1) * qs], q[cs]) for r, cs in subs]

    def update(cs, sc):
        n, width = sc.shape
        m_prev = m_scr[:, cs]
        m_part = jnp.max(sc.reshape(MAX_GROUPS, n // MAX_GROUPS, width), axis=0)
        m_new = jnp.maximum(m_prev, jnp.max(m_part, axis=0, keepdims=True))
        alpha_scr[:, cs] = jnp.exp2(m_prev - m_new)
        p_scr[:n, cs] = jnp.exp2((sc - m_new).astype(BF16))
        m_scr[:, cs] = m_new

    def softmax(s):
        for cs in fine:
            update(cs, s[:, cs])

    def softmax_diag(parts):
        kc = lax.broadcasted_iota(jnp.int32, (qs, qs), 0) // CHUNK
        qc = lax.broadcasted_iota(jnp.int32, (qs, qs), 1) // CHUNK
        allowed = jnp.concatenate([kc <= qc] * MLA_HEADS, axis=1)
        for (r, cs), sc in zip(subs, parts):
            last = jnp.where(allowed, sc[r * qs:], NEG_INF)
            update(cs, last if r == 0 else jnp.concatenate([sc[:r * qs], last], axis=0))

    def values(j, acc_slot):
        for cs in fine:
            acc_scr[acc_slot, :, cs] = (acc_scr[acc_slot, :, cs] * alpha_scr[:, cs]
                                        + _dot(vt_ref[0, j], p_scr[:, cs]))

    def values_diag(j, acc_slot):
        for r, cs in subs:
            n = (r + 1) * qs
            acc_scr[acc_slot, :, cs] = (acc_scr[acc_slot, :, cs] * alpha_scr[:, cs]
                                        + _dot(vt_ref[0, j, :, :n], p_scr[:n, cs]))

    def begin():
        m_scr[...] = jnp.full(m_scr.shape, NEG_INF, F32)
        acc_scr[slot] = jnp.zeros(acc_scr.shape[1:], F32)

    def drain_previous():
        prev = 1 - slot
        values_diag(i - 1, prev)
        acc = acc_scr[prev]
        o_t = acc[:KV_RANK] * (1.0 / acc[KV_RANK:KV_RANK + 1])
        for r in range(n_sub):
            for hd in range(MLA_HEADS):
                c0 = (r * MLA_HEADS + hd) * qs
                o_ref[0, r * qs:(r + 1) * qs, hd * KV_RANK:(hd + 1) * KV_RANK] = (
                    o_t[:, c0:c0 + qs].T.astype(BF16))

    @pl.when(i == 0)
    def _():
        begin()
        softmax_diag(scores_diag(0))

    @pl.when(jnp.logical_and(i > 0, i < nq))
    def _():
        begin()
        s = scores(0)
        drain_previous()
        softmax(s)

        def body(j, carry):
            s = scores(j)
            values(j - 1, slot)
            softmax(s)
            return carry

        lax.fori_loop(1, i, body, 0)
        parts = scores_diag(i)
        values(i - 1, slot)
        softmax_diag(parts)

    @pl.when(i == nq)
    def _():
        drain_previous()


def _attention_prompt(q, k, vt, t):
    nseq, _, _, qs, _ = q.shape
    length = k.shape[1]
    rows = MLA_HEADS * t
    nq = length // t
    assert vt.shape[-1] == t and t % qs == 0 and qs % CHUNK == 0
    return pl.pallas_call(
        functools.partial(_attn_kernel, t=t, qs=qs),
        grid=(nseq, nq + 1),
        in_specs=[pl.BlockSpec((1, t // qs, MLA_HEADS, qs, QK_DIM),
                               lambda b, i: (b, jnp.minimum(i, nq - 1), 0, 0, 0)),
                  pl.BlockSpec((1, length, QK_DIM), lambda b, i: (b, 0, 0)),
                  pl.BlockSpec((1, nq, VT_ROWS, t), lambda b, i: (b, 0, 0, 0))],
        out_specs=pl.BlockSpec((1, t, MLA_HEADS * KV_RANK), lambda b, i: (b, jnp.maximum(i - 1, 0), 0)),
        out_shape=jax.ShapeDtypeStruct((nseq, length, MLA_HEADS * KV_RANK), BF16),
        scratch_shapes=[pltpu.VMEM((1, rows), F32), pltpu.VMEM((1, rows), F32),
                        pltpu.VMEM((t, rows), BF16), pltpu.VMEM((2, VT_ROWS, rows), F32)],
        compiler_params=_params("parallel", "arbitrary"),
        name="mla_attention_prompt",
    )(q, k, vt)


def _chunk_mask(s, t, q_start, k_start):
    n = s.shape[1]
    qpos = q_start + lax.broadcasted_iota(jnp.int32, (t, n), 0)
    kpos = k_start + lax.broadcasted_iota(jnp.int32, (t, n), 1)
    allowed = (kpos // CHUNK) <= (qpos // CHUNK)
    return jnp.where(allowed[None], s.reshape(MLA_HEADS, t, n), NEG_INF).reshape(s.shape)


def _attn_sample_kernel(q_ref, ckv_ref, ckpe_ref, kn_ref, o_ref, *, t, past, mask_cache, mask_new):
    rows = MLA_HEADS * t
    q = q_ref[0].reshape(rows, QK_DIM)
    ckv = ckv_ref[0].astype(BF16)
    ckpe_t = ckpe_ref[0].astype(BF16)
    kn = kn_ref[0]
    s_c = _dot_nt(q[:, :KV_RANK], ckv) + _dot(q[:, KV_RANK:], ckpe_t)
    s_n = _dot_nt(q, kn)
    if mask_cache:
        s_c = _chunk_mask(s_c, t, past, 0)
    if mask_new:
        s_n = _chunk_mask(s_n, t, past, past)
    m = jnp.maximum(jnp.max(s_c, axis=1, keepdims=True), jnp.max(s_n, axis=1, keepdims=True))
    p_c = jnp.exp2(s_c - m)
    p_n = jnp.exp2(s_n - m)
    denom = jnp.sum(p_c, axis=1, keepdims=True) + jnp.sum(p_n, axis=1, keepdims=True)
    o = _dot(p_c.astype(BF16), ckv) + _dot(p_n.astype(BF16), kn[:, :KV_RANK])
    o = (o / denom).astype(BF16)
    for hd in range(MLA_HEADS):
        o_ref[0, :, hd * KV_RANK:(hd + 1) * KV_RANK] = o[hd * t:(hd + 1) * t]


def _attention_sample(q, cache_kv, cache_kpe_t, k_new, l):
    nb, _, t, _ = q.shape
    past = cache_kv.shape[2]
    q_chunk = (past + np.arange(t)) // CHUNK
    k_chunk = np.arange(past + t) // CHUNK
    allowed = k_chunk[None, :] <= q_chunk[:, None]
    mask_cache = not allowed[:, :past].all()
    mask_new = not allowed[:, past:].all()
    return pl.pallas_call(
        functools.partial(_attn_sample_kernel, t=t, past=past, mask_cache=mask_cache, mask_new=mask_new),
        grid=(nb,),
        in_specs=[pl.BlockSpec((1, MLA_HEADS, t, QK_DIM), lambda b: (b, 0, 0, 0)),
                  pl.BlockSpec((None, 1, past, KV_RANK), lambda b: (l, b, 0, 0)),
                  pl.BlockSpec((None, 1, MLA_ROPE, past), lambda b: (l, b, 0, 0)),
                  pl.BlockSpec((1, t, QK_DIM), lambda b: (b, 0, 0))],
        out_specs=pl.BlockSpec((1, t, MLA_HEADS * KV_RANK), lambda b: (b, 0, 0)),
        out_shape=jax.ShapeDtypeStruct((nb, t, MLA_HEADS * KV_RANK), BF16),
        compiler_params=_params("parallel"),
        name="mla_attention_sample",
    )(q, cache_kv, cache_kpe_t, k_new)


def _ret_kernel(rq_ref, rk_ref, rv_ref, s0_ref, dec_ref, cross_ref, kdec_ref, gt_ref,
                o_ref, sout_ref, st_scr):
    c = pl.program_id(1)

    @pl.when(c == 0)
    def _():
        st_scr[...] = s0_ref[0]

    heads = range(RET_HEADS)
    t = dec_ref.shape[1]
    chunks = [slice(r, r + t) for r in range(0, rq_ref.shape[1], t)]
    ks = [slice(hd * RET_DK, (hd + 1) * RET_DK) for hd in heads]
    vs = [slice(hd * RET_DV, (hd + 1) * RET_DV) for hd in heads]
    qs = [[rq_ref[0, rows, ks[hd]] for hd in heads] for rows in chunks]
    kk = [[rk_ref[0, rows, ks[hd]] for hd in heads] for rows in chunks]
    vv = [[rv_ref[0, rows, vs[hd]] for hd in heads] for rows in chunks]
    scores = [[_dot_nt(q, k) for q, k in zip(qc, kc)] for qc, kc in zip(qs, kk)]
    st = [st_scr[hd] for hd in heads]
    for n, rows in enumerate(chunks):
        cross = [_dot(qs[n][hd], st[hd].astype(BF16)) for hd in heads]
        st_next = []
        for hd in heads:
            k_dec = (kk[n][hd].astype(F32) * kdec_ref[hd]).astype(BF16)
            st_next.append(gt_ref[hd] * st[hd] + _dot_tn(k_dec, vv[n][hd]))
        for hd in heads:
            sc = (scores[n][hd] * dec_ref[hd]).astype(BF16)
            o_ref[0, rows, vs[hd]] = (_dot(sc, vv[n][hd]) + cross[hd] * cross_ref[hd]).astype(BF16)
        st = st_next
    for hd in heads:
        st_scr[hd] = st[hd]

    @pl.when(c == pl.num_programs(1) - 1)
    def _():
        sout_ref[0] = st_scr[...]


def _ret_tables(t):
    f32 = np.float32
    log_g = np.log1p(-np.exp2(-5.0 - np.arange(RET_HEADS, dtype=f32))).astype(f32)
    n = np.arange(t, dtype=f32)
    diff = n[:, None] - n[None, :]
    dec = np.where(diff >= 0.0, np.exp(np.maximum(diff, 0.0)[None] * log_g[:, None, None]), 0.0)
    cross = np.exp((n + 1.0)[None, :] * log_g[:, None])
    kdec = np.exp((t - 1.0 - n)[None, :] * log_g[:, None])
    gt = np.exp(f32(t) * log_g)
    cross = np.broadcast_to(cross[:, :, None], (RET_HEADS, t, RET_DV))
    kdec = np.broadcast_to(kdec[:, :, None], (RET_HEADS, t, RET_DK))
    gt = np.broadcast_to(gt[:, None, None], (RET_HEADS, RET_DK, RET_DV))
    return tuple(jnp.asarray(a, F32) for a in (dec, cross, kdec, gt))


def _retention(rq, rk, rv, state0, tabs, t):
    nseq, length, _ = rq.shape
    assert t % tabs[0].shape[1] == 0
    tile = pl.BlockSpec((1, t, RET_W), lambda b, c: (b, c, 0))
    st = pl.BlockSpec((1, RET_HEADS, RET_DK, RET_DV), lambda b, c: (b, 0, 0, 0))
    const = lambda a: pl.BlockSpec(a.shape, lambda b, c: (0, 0, 0))
    return pl.pallas_call(
        _ret_kernel,
        grid=(nseq, length // t),
        in_specs=[tile, tile, tile, st] + [const(a) for a in tabs],
        out_specs=[tile, st],
        out_shape=[jax.ShapeDtypeStruct((nseq, length, RET_W), BF16),
                   jax.ShapeDtypeStruct(state0.shape, F32)],
        scratch_shapes=[pltpu.VMEM((RET_HEADS, RET_DK, RET_DV), F32)],
        compiler_params=_params("parallel", "arbitrary"),
        name="retention",
    )(rq, rk, rv, state0, *tabs)


def _out_kernel(x_ref, ol_ref, or_ref, rg_ref, gt_ref, gpost_ref, wuv_ref, wo_ref, grp_ref, o_ref, *, sub):
    n_mla = MLA_HEADS * MLA_V
    starts = list(range(0, x_ref.shape[1], sub))
    tiles = [slice(s0, s0 + sub) for s0 in starts]
    o_mla = [_dot(ol_ref[0, rs], wuv_ref[...]).astype(BF16) for rs in tiles]
    mix = [_dot(om, wo_ref[:n_mla, :]) for om in o_mla]
    o = [or_ref[0, rs].astype(F32) for rs in tiles]
    ms = [_dot((v * v).astype(BF16), grp_ref[...]) for v in o]
    o_ret = [(v * lax.rsqrt(m + EPS) * _silu(rg_ref[0, rs].astype(F32))).astype(BF16)
             for v, m, rs in zip(o, ms, tiles)]
    for i, rs in enumerate(tiles):
        gate = gt_ref[0] if gt_ref.shape[1] == 1 else gt_ref[0, rs]
        total = mix[i] + _dot(o_ret[i], wo_ref[n_mla:, :])
        o_ref[0, rs] = x_ref[0, rs] + gate * (_rms(total) * gpost_ref[...])


def _mixer_out(x, o_lat, o_ret, rg, gate, g_post, w_uv, w_out, l, tm):
    nseq, length, d = x.shape
    tile = lambda n: pl.BlockSpec((1, tm, n), lambda b, s: (b, s, 0))
    wl = lambda r, c: _resident((None, r, c), lambda b, s: (l, 0, 0))
    head = np.arange(RET_W) // RET_DV
    group_mean = jnp.asarray((head[:, None] == head[None, :]).astype(np.float32) / RET_DV, BF16)
    return pl.pallas_call(
        functools.partial(_out_kernel, sub=min(tm, MIX_SUB_ROWS)),
        grid=(nseq, length // tm),
        in_specs=[tile(d), tile(o_lat.shape[-1]), tile(RET_W), tile(RET_W), _mod_spec(gate, tm),
                  pl.BlockSpec((1, d), lambda b, s: (0, 0)),
                  wl(*w_uv.shape[1:]), wl(*w_out.shape[1:]),
                  _resident((RET_W, RET_W), lambda b, s: (0, 0))],
        out_specs=tile(d),
        out_shape=jax.ShapeDtypeStruct(x.shape, F32),
        compiler_params=_params("parallel", "parallel"),
        name="mixer_out",
    )(x, o_lat, o_ret, rg, gate, g_post, w_uv, w_out, group_mean)


def _prep_w_in(w_in):
    c0 = Q_RANK + KV_RANK
    half = MLA_ROPE // 2
    kpe = w_in[..., c0:c0 + MLA_ROPE]
    kpe_rot = jnp.concatenate([-kpe[..., half:], kpe[..., :half]], axis=-1)
    pad = jnp.zeros(w_in.shape[:-1] + (KPE_BLOCK - 2 * MLA_ROPE,), w_in.dtype)
    return jnp.concatenate([w_in[..., :c0], kpe, kpe_rot, pad, w_in[..., c0 + MLA_ROPE:]],
                           axis=-1).astype(BF16)


def _absorb_kernel(uq_ref, uk_ref, o_ref):
    o_ref[0, 0] = lax.dot_general(uq_ref[0, 0], uk_ref[0, 0], (((1,), (1,)), ((), ())),
                                  preferred_element_type=F32, precision=lax.Precision.HIGHEST)


def _prep_w_uq(w_uq, w_uk):
    depth = w_uq.shape[0]
    half = MLA_ROPE // 2
    w = w_uq.reshape(depth, Q_RANK, MLA_HEADS, MLA_NOPE + MLA_ROPE)
    nope = w[..., :MLA_NOPE].transpose(0, 2, 1, 3)
    uk = w_uk.reshape(depth, KV_RANK, MLA_HEADS, MLA_NOPE).transpose(0, 2, 1, 3)
    absorbed = pl.pallas_call(
        _absorb_kernel,
        grid=(depth, MLA_HEADS),
        in_specs=[pl.BlockSpec((1, 1, Q_RANK, MLA_NOPE), lambda l, h: (l, h, 0, 0)),
                  pl.BlockSpec((1, 1, KV_RANK, MLA_NOPE), lambda l, h: (l, h, 0, 0))],
        out_specs=pl.BlockSpec((1, 1, Q_RANK, KV_RANK), lambda l, h: (l, h, 0, 0)),
        out_shape=jax.ShapeDtypeStruct((depth, MLA_HEADS, Q_RANK, KV_RANK), F32),
        compiler_params=_params("parallel", "parallel"),
        name="absorb_w_uk",
    )(nope, uk)
    absorbed = absorbed.transpose(0, 2, 1, 3).reshape(depth, Q_RANK, MLA_HEADS * KV_RANK)
    pe = w[..., MLA_NOPE:]
    rot = jnp.concatenate([-pe[..., half:], pe[..., :half]], axis=-1)
    flat = lambda a: a.reshape(depth, Q_RANK, MLA_HEADS * MLA_ROPE)
    return jnp.concatenate([absorbed, flat(pe), flat(rot)], axis=-1).astype(BF16)


def _prep_w_uv(w_uv):
    depth = w_uv.shape[0]
    w = w_uv.reshape(depth, KV_RANK, MLA_HEADS, MLA_V)
    eye = jnp.eye(MLA_HEADS, dtype=w_uv.dtype)
    bd = jnp.einsum('lrhv,hg->lhrgv', w, eye)
    return bd.reshape(depth, MLA_HEADS * KV_RANK, MLA_HEADS * MLA_V).astype(BF16)


def _pick_tile(length, target):
    t = min(length, target)
    while length % t:
        t //= 2
    return t


def kernel(x_prompt, x_sample, cache_kv_latent, cache_k_rope, state_ret, c_prompt, c_sample,
           ada_w, ada_b, norm_pre, norm_post, ffn_w1, ffn_w3, ffn_w2,
           w_in, q_norm, kv_norm, w_uq, w_uk, w_uv, w_out):
    depth = ada_w.shape[0]
    nb_p, seq, d = x_prompt.shape
    nb_s, dec_seq, _ = x_sample.shape
    past = cache_kv_latent.shape[2]
    n_tok_s = nb_s * dec_seq

    w1, w3, w2 = ffn_w1.astype(BF16), ffn_w3.astype(BF16), ffn_w2.astype(BF16)
    w_in_b = _prep_w_in(w_in)
    w_uq_b = _prep_w_uq(w_uq, w_uk)
    w_uv_b = _prep_w_uv(w_uv)
    w_out_b = w_out.astype(BF16)

    mods = _ada_mods(jnp.concatenate([c_prompt, c_sample], axis=0), ada_w, ada_b)

    t_att = _pick_tile(seq, 512)
    tm_p = _pick_tile(seq, 4 * MIX_SUB_ROWS)
    tm_ffn = _pick_tile(seq, 4 * FFN_SUB_ROWS)
    t_ret = _pick_tile(seq, 256)
    ret_step = _pick_tile(seq, 2 * t_ret)
    tabs_p = _rope_tables(np.arange(seq))
    tabs_s = _rope_tables(np.tile(past + np.arange(dec_seq), nb_s))
    ret_tabs_p = _ret_tables(t_ret)
    ret_tabs_s = _ret_tables(dec_seq)
    zero_state = jnp.zeros((nb_p, RET_HEADS, RET_DK, RET_DV), F32)
    cache_kpe_t = jnp.swapaxes(cache_k_rope, 2, 3)

    def mod_p(l, r):
        return mods[l, :nb_p, r][:, None, :]

    def mod_s(l, r):
        return jnp.repeat(mods[l, nb_p:, r], dec_seq, axis=0)[None]

    y_p = x_prompt
    y_s = x_sample.reshape(1, n_tok_s, d)
    caches_p, caches_s = None, None
    ret_p, ret_s = [], []

    for l in range(depth):
        gp = lambda j: norm_pre[l, j][None, :]
        go = lambda j: norm_post[l, j][None, :]
        qn, kvn = q_norm[l][None, :], kv_norm[l][None, :]

        y_p = _ffn(y_p, mod_p(l, 0), mod_p(l, 1), mod_p(l, 2), gp(0), go(0), w1, w3, w2, l, 0, tm_ffn)
        q, k, vt, kv, kpe, rq, rk, rv, rg = _proj(y_p, mod_p(l, 3), mod_p(l, 4), gp(1), w_in_b, qn, kvn,
                                                  w_uq_b, tabs_p, l, tm_p, Q_SUB, t_att, caches_p)
        caches_p = (kv, kpe)
        o_lat = _attention_prompt(q, k, vt, t_att)
        o_ret, r_fin = _retention(rq, rk, rv, zero_state, ret_tabs_p, ret_step)
        y_p = _mixer_out(y_p, o_lat, o_ret, rg, mod_p(l, 5), go(1), w_uv_b, w_out_b, l, tm_p)
        y_p = _ffn(y_p, mod_p(l, 6), mod_p(l, 7), mod_p(l, 8), gp(2), go(2), w1, w3, w2, l, 1, tm_ffn)
        ret_p.append(r_fin)

        y_s = _ffn(y_s, mod_s(l, 0), mod_s(l, 1), mod_s(l, 2), gp(0), go(0), w1, w3, w2, l, 0, n_tok_s)
        q, k, _, kv, kpe, rq, rk, rv, rg = _proj(y_s, mod_s(l, 3), mod_s(l, 4), gp(1), w_in_b, qn, kvn,
                                                 w_uq_b, tabs_s, l, n_tok_s, dec_seq, n_tok_s, caches_s)
        caches_s = (kv, kpe)
        per_seq = lambda a: a.reshape(nb_s, dec_seq, a.shape[-1])
        q = q.reshape(nb_s, MLA_HEADS, dec_seq, QK_DIM)
        o_lat = _attention_sample(q, cache_kv_latent, cache_kpe_t, per_seq(k), l)
        o_ret, r_new = _retention(per_seq(rq), per_seq(rk), per_seq(rv),
                                  state_ret[l], ret_tabs_s, dec_seq)
        y_s = _mixer_out(y_s, o_lat.reshape(1, n_tok_s, -1), o_ret.reshape(1, n_tok_s, -1), rg,
                         mod_s(l, 5), go(1), w_uv_b, w_out_b, l, n_tok_s)
        y_s = _ffn(y_s, mod_s(l, 6), mod_s(l, 7), mod_s(l, 8), gp(2), go(2), w1, w3, w2, l, 1, n_tok_s)
        ret_s.append(r_new)

    per_layer_seq = lambda a: a.reshape(depth, nb_s, dec_seq, a.shape[-1])
    tokens_major = lambda a: jnp.swapaxes(a, 2, 3)
    return (y_p, y_s.reshape(nb_s, dec_seq, d),
            caches_p[0], tokens_major(caches_p[1]), jnp.stack(ret_p),
            per_layer_seq(caches_s[0]), per_layer_seq(tokens_major(caches_s[1])), jnp.stack(ret_s))
```

```python
import functools

import jax
import jax.numpy as jnp
import numpy as np
from jax import lax
from jax.experimental import pallas as pl
from jax.experimental.pallas import tpu as pltpu

CHUNK = 64
N_SUB = 3
MLA_HEADS = 8
MLA_NOPE = 64
MLA_ROPE = 32
MLA_V = 64
Q_RANK = 256
KV_RANK = 128
RET_HEADS = 8
RET_DK = 64
RET_DV = 64
ROPE_BASE = 10000.0
EPS = 1e-6
MLA_SCALE = (MLA_NOPE + MLA_ROPE) ** -0.5
Q_SCALE = MLA_SCALE * float(np.log2(np.e))
MAX_GROUPS = 8
MXU_DIM = 256
FFN_CHUNK = 3 * MXU_DIM
FFN_SUB_ROWS = 256
Q_SUB = 128
MIX_SUB_ROWS = 256
FULL_TILE_COLS = 256
RET_SCALE = RET_DK ** -0.5
NEG_INF = -1e30

QK_DIM = KV_RANK + MLA_ROPE
RET_W = RET_HEADS * RET_DK
LANES = 128
KPE_BLOCK = LANES
BF16_SUBLANES = 16
VT_ROWS = KV_RANK + BF16_SUBLANES
W_IN_COLS = Q_RANK + KV_RANK + KPE_BLOCK + 4 * RET_W

F32 = jnp.float32
BF16 = jnp.bfloat16

VMEM_LIMIT = 56 * 1024 * 1024


def _params(*sem):
    return pltpu.CompilerParams(dimension_semantics=sem, vmem_limit_bytes=VMEM_LIMIT)


def _resident(block_shape, index_map):
    return pl.BlockSpec(block_shape, index_map, pipeline_mode=pl.Buffered(1))


def _rms(x):
    return x * lax.rsqrt(jnp.mean(x * x, axis=-1, keepdims=True) + EPS)


def _silu(x):
    return x * jax.nn.sigmoid(x)


def _dot(a, b):
    return jnp.dot(a, b, preferred_element_type=F32)


def _dot_nt(a, b):
    return lax.dot_general(a, b, (((1,), (1,)), ((), ())), preferred_element_type=F32)


def _dot_tn(a, b):
    return lax.dot_general(a, b, (((0,), (0,)), ((), ())), preferred_element_type=F32)


def _ada_kernel(c_ref, w_ref, b_ref, o_ref):
    sc = _silu(c_ref[...])
    o_ref[0] = jnp.dot(sc, w_ref[0], preferred_element_type=F32,
                       precision=lax.Precision.HIGHEST) + b_ref[0]


def _ada_mods(c_all, ada_w, ada_b):
    depth, d, cols = ada_w.shape
    nb = c_all.shape[0]
    n_col = cols // d
    out = pl.pallas_call(
        _ada_kernel,
        grid=(depth, n_col),
        in_specs=[
            pl.BlockSpec((nb, d), lambda l, j: (0, 0)),
            pl.BlockSpec((1, d, d), lambda l, j: (l, 0, j)),
            pl.BlockSpec((1, 1, d), lambda l, j: (l, 0, j)),
        ],
        out_specs=pl.BlockSpec((1, nb, d), lambda l, j: (l, 0, j)),
        out_shape=jax.ShapeDtypeStruct((depth, nb, cols), F32),
        compiler_params=_params("parallel", "parallel"),
        name="ada_mods",
    )(c_all, ada_w, ada_b.reshape(depth, 1, cols))
    return out.reshape(depth, nb, n_col, d)


def _ffn_kernel(x_ref, sh_ref, sc_ref, gt_ref, gpre_ref, gpost_ref, w1_ref, w3_ref, w2_ref, o_ref,
                *, coef, f_chunk, sub):
    d_ff = w1_ref.shape[1]
    starts = list(range(0, x_ref.shape[1], sub))
    rows = lambda ref, r0: ref[0] if ref.shape[1] == 1 else ref[0, r0:r0 + sub]
    hs = [(_rms(x_ref[0, r0:r0 + sub]) * gpre_ref[...] * (1.0 + rows(sc_ref, r0))
           + rows(sh_ref, r0)).astype(BF16) for r0 in starts]
    outs = [None] * len(starts)
    pending = [None] * len(starts)

    def down(i, c0, c1):
        part = _dot(pending[i], w2_ref[c0:c1, :])
        outs[i] = part if outs[i] is None else outs[i] + part

    prev = None
    for c0 in range(0, d_ff, f_chunk):
        c1 = min(c0 + f_chunk, d_ff)
        gated = []
        for i, h in enumerate(hs):
            a = _dot(h, w1_ref[:, c0:c1])
            b = _dot(h, w3_ref[:, c0:c1])
            gated.append((_silu(a) * b).astype(BF16))
        if prev is not None:
            for i in range(len(hs)):
                down(i, *prev)
        pending, prev = gated, (c0, c1)
    for i, r0 in enumerate(starts):
        down(i, *prev)
        o_ref[0, r0:r0 + sub] = (x_ref[0, r0:r0 + sub]
                                 + (coef * rows(gt_ref, r0)) * (_rms(outs[i]) * gpost_ref[...]))


def _mod_spec(mod, tm):
    if mod.shape[1] == 1:
        return pl.BlockSpec((1, 1, mod.shape[2]), lambda b, s: (b, 0, 0))
    return pl.BlockSpec((1, tm, mod.shape[2]), lambda b, s: (b, s, 0))


def _ffn(x, shift, scale, gate, g_pre, g_post, w1, w3, w2, l, k, tm):
    nseq, length, d = x.shape
    d_ff = w1.shape[-1]
    f_chunk = min(d_ff, FFN_CHUNK)
    w_up = _resident((None, None, d, d_ff), lambda b, s: (l, k, 0, 0))
    w_dn = _resident((None, None, d_ff, d), lambda b, s: (l, k, 0, 0))
    vec = pl.BlockSpec((1, d), lambda b, s: (0, 0))
    tile = pl.BlockSpec((1, tm, d), lambda b, s: (b, s, 0))
    return pl.pallas_call(
        functools.partial(_ffn_kernel, coef=0.5, f_chunk=f_chunk, sub=min(tm, FFN_SUB_ROWS)),
        grid=(nseq, length // tm),
        in_specs=[tile, _mod_spec(shift, tm), _mod_spec(scale, tm), _mod_spec(gate, tm),
                  vec, vec, w_up, w_up, w_dn],
        out_specs=tile,
        out_shape=jax.ShapeDtypeStruct(x.shape, F32),
        compiler_params=_params("parallel", "parallel"),
        name="ffn",
    )(x, shift, scale, gate, g_pre, g_post, w1, w3, w2)


def _ffn_stream_kernel(x_ref, sh_ref, sc_ref, gt_ref, gpre_ref, gpost_ref, w1_ref, w3_ref, w2_ref, o_ref,
                       h_scr, acc_scr, *, coef):
    c = pl.program_id(0)

    @pl.when(c == 0)
    def _():
        h_scr[...] = (_rms(x_ref[0]) * gpre_ref[...] * (1.0 + sc_ref[0]) + sh_ref[0]).astype(BF16)
        acc_scr[...] = jnp.zeros(acc_scr.shape, F32)

    h = h_scr[...]
    gated = (_silu(_dot(h, w1_ref[...])) * _dot(h, w3_ref[...])).astype(BF16)
    acc_scr[...] += _dot(gated, w2_ref[...])

    @pl.when(c == pl.num_programs(0) - 1)
    def _():
        o_ref[0] = x_ref[0] + (coef * gt_ref[0]) * (_rms(acc_scr[...]) * gpost_ref[...])


def _ffn_stream(x, shift, scale, gate, g_pre, g_post, w1, w3, w2, l, k):
    _, n_tok, d = x.shape
    d_ff = w1.shape[-1]
    slab = d_ff // 2 if (d_ff // 2) % LANES == 0 else d_ff
    whole = lambda a: pl.BlockSpec(a.shape, lambda c: (0,) * a.ndim)
    return pl.pallas_call(
        functools.partial(_ffn_stream_kernel, coef=0.5),
        grid=(d_ff // slab,),
        in_specs=[whole(x), whole(shift), whole(scale), whole(gate), whole(g_pre), whole(g_post),
                  pl.BlockSpec((None, None, d, slab), lambda c: (l, k, 0, c)),
                  pl.BlockSpec((None, None, d, slab), lambda c: (l, k, 0, c)),
                  pl.BlockSpec((None, None, slab, d), lambda c: (l, k, c, 0))],
        out_specs=whole(x),
        out_shape=jax.ShapeDtypeStruct(x.shape, F32),
        scratch_shapes=[pltpu.VMEM((n_tok, d), BF16), pltpu.VMEM((n_tok, d), F32)],
        compiler_params=_params("arbitrary"),
        name="ffn_stream",
    )(x, shift, scale, gate, g_pre, g_post, w1, w3, w2)


def _proj_kernel(x_ref, sh_ref, sc_ref, gpre_ref, win_ref, qn_ref, kvn_ref, wuq_ref,
                 cq_ref, sq_ref, ck_ref, sk_ref, ckt_ref, skt_ref, rc_ref, rsa_ref, rsb_ref, *rest, sub):
    q_out, k_out, vt_out, kv_out, kpe_out, rq_out, rk_out, rv_out, rg_out = rest[-9:]
    qsub = q_out.shape[3]
    c0 = Q_RANK + KV_RANK
    r0 = c0 + KPE_BLOCK
    n_abs = MLA_HEADS * KV_RANK
    n_pe = MLA_HEADS * MLA_ROPE

    starts = list(range(0, x_ref.shape[1], sub))
    rows = lambda ref, s0: ref[0] if ref.shape[1] == 1 else ref[0, s0:s0 + sub]

    def project(s0):
        h = (_rms(x_ref[0, s0:s0 + sub]) * gpre_ref[...] * (1.0 + rows(sc_ref, s0))
             + rows(sh_ref, s0)).astype(BF16)
        return _dot(h, win_ref[...])

    z_next = project(starts[0])
    for k, s0 in enumerate(starts):
        z = z_next
        if k + 1 < len(starts):
            z_next = project(starts[k + 1])
        rs = slice(s0, s0 + sub)
        q_lat = (_rms(z[:, :Q_RANK]) * qn_ref[...]).astype(BF16)
        kv_lat = _rms(z[:, Q_RANK:c0]) * kvn_ref[...]
        k_pe = z[:, c0:c0 + MLA_ROPE] * ck_ref[rs] + z[:, c0 + MLA_ROPE:c0 + 2 * MLA_ROPE] * sk_ref[rs]
        kv_out[0, rs] = kv_lat
        blk_t = z[:, c0:c0 + KPE_BLOCK].T
        kpe_out[0, :, rs] = (blk_t[:MLA_ROPE] * ckt_ref[:, rs]
                             + blk_t[MLA_ROPE:2 * MLA_ROPE] * skt_ref[:, rs])
        k_out[0, rs] = jnp.concatenate([kv_lat, k_pe], axis=-1).astype(BF16)
        ones = jnp.ones((VT_ROWS - KV_RANK, sub), F32)
        vt_w = vt_out.shape[3]
        vt_out[0, s0 // vt_w, :, s0 % vt_w:s0 % vt_w + sub] = (
            jnp.concatenate([kv_lat.T, ones], axis=0).astype(BF16))

        qq = _dot(q_lat, wuq_ref[...])
        q_abs = qq[:, :n_abs]
        q_pe = qq[:, n_abs:n_abs + n_pe] * cq_ref[rs] + qq[:, n_abs + n_pe:] * sq_ref[rs]
        for hd in range(MLA_HEADS):
            qh = jnp.concatenate([q_abs[:, hd * KV_RANK:(hd + 1) * KV_RANK],
                                  q_pe[:, hd * MLA_ROPE:(hd + 1) * MLA_ROPE]], axis=-1)
            qh = (qh * Q_SCALE).astype(BF16)
            for r in range(sub // qsub):
                q_out[0, s0 // qsub + r, hd] = qh[r * qsub:(r + 1) * qsub]

        rc, rsa, rsb = rc_ref[rs], rsa_ref[rs], rsb_ref[rs]
        for off, out, scale in ((r0, rq_out, 1.0), (r0 + RET_W, rk_out, RET_SCALE)):
            for c in range(RET_W // LANES):
                xc = z[:, off + c * LANES:off + (c + 1) * LANES]
                y = (xc * rc + pltpu.roll(xc, LANES - RET_DK // 2, 1) * rsa
                     + pltpu.roll(xc, RET_DK // 2, 1) * rsb)
                out[0, rs, c * LANES:(c + 1) * LANES] = (y * scale).astype(BF16)
        rv_out[0, rs] = z[:, r0 + 2 * RET_W:r0 + 3 * RET_W].astype(BF16)
        rg_out[0, rs] = z[:, r0 + 3 * RET_W:r0 + 4 * RET_W].astype(BF16)


def _proj(x, shift, scale, g_pre, w_in, q_norm, kv_norm, w_uq, tabs, l, tm, qsub, vt_w, caches):
    nseq, length, d = x.shape
    depth = w_in.shape[0]
    grid = (nseq, length // tm)
    vec = lambda n: pl.BlockSpec((1, n), lambda b, s: (0, 0))
    tile = lambda n: pl.BlockSpec((1, tm, n), lambda b, s: (b, s, 0))
    tab = lambda n: pl.BlockSpec((tm, n), lambda b, s: (s, 0))
    wl = lambda r, c: _resident((None, r, c), lambda b, s: (l, 0, 0))
    layer_tile = lambda n: pl.BlockSpec((None, 1, tm, n), lambda b, s: (l, b, s, 0))
    bf = lambda n: jax.ShapeDtypeStruct((nseq, length, n), BF16)
    in_specs = [tile(d), _mod_spec(shift, tm), _mod_spec(scale, tm), vec(d),
                wl(d, W_IN_COLS), vec(Q_RANK), vec(KV_RANK),
                wl(Q_RANK, w_uq.shape[-1]),
                tab(MLA_HEADS * MLA_ROPE), tab(MLA_HEADS * MLA_ROPE), tab(MLA_ROPE), tab(MLA_ROPE),
                pl.BlockSpec((MLA_ROPE, tm), lambda b, s: (0, s)),
                pl.BlockSpec((MLA_ROPE, tm), lambda b, s: (0, s)),
                tab(LANES), tab(LANES), tab(LANES)]
    args = [x, shift, scale, g_pre, w_in, q_norm, kv_norm, w_uq, *tabs]
    aliases = {}
    if caches is not None:
        aliases = {len(args): 3, len(args) + 1: 4}
        in_specs += [pl.BlockSpec(memory_space=pl.ANY)] * 2
        args += list(caches)
    return pl.pallas_call(
        functools.partial(_proj_kernel, sub=max(qsub, min(tm, MIX_SUB_ROWS))),
        grid=grid,
        in_specs=in_specs,
        out_specs=[pl.BlockSpec((1, tm // qsub, MLA_HEADS, qsub, QK_DIM), lambda b, s: (b, s, 0, 0, 0)),
                   tile(QK_DIM),
                   pl.BlockSpec((1, tm // vt_w, VT_ROWS, vt_w), lambda b, s: (b, s, 0, 0)),
                   layer_tile(KV_RANK),
                   pl.BlockSpec((None, 1, MLA_ROPE, tm), lambda b, s: (l, b, 0, s)),
                   tile(RET_W), tile(RET_W), tile(RET_W), tile(RET_W)],
        out_shape=[jax.ShapeDtypeStruct((nseq, length // qsub, MLA_HEADS, qsub, QK_DIM), BF16),
                   bf(QK_DIM),
                   jax.ShapeDtypeStruct((nseq, length // vt_w, VT_ROWS, vt_w), BF16),
                   jax.ShapeDtypeStruct((depth, nseq, length, KV_RANK), F32),
                   jax.ShapeDtypeStruct((depth, nseq, MLA_ROPE, length), F32),
                   bf(RET_W), bf(RET_W), bf(RET_W), bf(RET_W)],
        input_output_aliases=aliases,
        compiler_params=_params("parallel", "parallel"),
        name="mixer_proj",
    )(*args)


def _rope_tables(pos):
    pos = np.asarray(pos, np.float32)[:, None]

    def cs(half):
        inv = (ROPE_BASE ** (-np.arange(half, dtype=np.float32) / half)).astype(np.float32)
        ang = pos * inv[None, :]
        return np.cos(ang), np.sin(ang)

    c16, s16 = cs(MLA_ROPE // 2)
    ck = np.concatenate([c16, c16], axis=-1)
    sk = np.concatenate([s16, s16], axis=-1)
    cq = np.tile(ck, (1, MLA_HEADS))
    sq = np.tile(sk, (1, MLA_HEADS))
    c32, s32 = cs(RET_DK // 2)
    zero = np.zeros_like(s32)
    reps = LANES // RET_DK
    rc = np.tile(np.concatenate([c32, c32], axis=-1), (1, reps))
    rsa = np.tile(np.concatenate([-s32, zero], axis=-1), (1, reps))
    rsb = np.tile(np.concatenate([zero, s32], axis=-1), (1, reps))
    return tuple(jnp.asarray(t, F32) for t in (cq, sq, ck, sk, ck.T, sk.T, rc, rsa, rsb))


def _attn_kernel(q_ref, k_ref, vt_ref, o_ref, m_scr, alpha_scr, p_scr, acc_scr, *, t, qs):
    i = pl.program_id(1)
    nq = pl.num_programs(1) - 1
    n_sub = t // qs
    cb = MLA_HEADS * qs
    rows = n_sub * cb
    subs = [(r, slice(r * cb, (r + 1) * cb)) for r in range(n_sub)]
    fine = [slice(c, c + FULL_TILE_COLS) for c in range(0, rows, FULL_TILE_COLS)]
    slot = i % 2
    q = q_ref[0].reshape(rows, QK_DIM)

    def k_tile(j):
        return k_ref[0, pl.ds(pl.multiple_of(j * t, t), t), :]

    def scores(j):
        return _dot_nt(k_tile(j), q)

    def scores_diag(j):
        kt = k_tile(j)
        return [_dot_nt(kt[:(r + 1) * qs], q[cs]) for r, cs in subs]

    def update(cs, sc):
        n, width = sc.shape
        m_prev = m_scr[:, cs]
        m_part = jnp.max(sc.reshape(MAX_GROUPS, n // MAX_GROUPS, width), axis=0)
        m_new = jnp.maximum(m_prev, jnp.max(m_part, axis=0, keepdims=True))
        alpha_scr[:, cs] = jnp.exp2(m_prev - m_new)
        p_scr[:n, cs] = jnp.exp2((sc - m_new).astype(BF16))
        m_scr[:, cs] = m_new

    def softmax(s):
        for cs in fine:
            update(cs, s[:, cs])

    def softmax_diag(parts):
        kc = lax.broadcasted_iota(jnp.int32, (qs, qs), 0) // CHUNK
        qc = lax.broadcasted_iota(jnp.int32, (qs, qs), 1) // CHUNK
        allowed = jnp.concatenate([kc <= qc] * MLA_HEADS, axis=1)
        for (r, cs), sc in zip(subs, parts):
            last = jnp.where(allowed, sc[r * qs:], NEG_INF)
            update(cs, last if r == 0 else jnp.concatenate([sc[:r * qs], last], axis=0))

    def values(j, acc_slot):
        for cs in fine:
            acc_scr[acc_slot, :, cs] = (acc_scr[acc_slot, :, cs] * alpha_scr[:, cs]
                                        + _dot(vt_ref[0, j], p_scr[:, cs]))

    def values_diag(j, acc_slot):
        for r, cs in subs:
            n = (r + 1) * qs
            acc_scr[acc_slot, :, cs] = (acc_scr[acc_slot, :, cs] * alpha_scr[:, cs]
                                        + _dot(vt_ref[0, j, :, :n], p_scr[:n, cs]))

    def begin():
        m_scr[...] = jnp.full(m_scr.shape, NEG_INF, F32)
        acc_scr[slot] = jnp.zeros(acc_scr.shape[1:], F32)

    def drain_previous():
        prev = 1 - slot
        values_diag(i - 1, prev)
        acc = acc_scr[prev]
        o_t = acc[:KV_RANK] * (1.0 / acc[KV_RANK:KV_RANK + 1])
        for r in range(n_sub):
            for hd in range(MLA_HEADS):
                c0 = (r * MLA_HEADS + hd) * qs
                o_ref[0, r * qs:(r + 1) * qs, hd * KV_RANK:(hd + 1) * KV_RANK] = (
                    o_t[:, c0:c0 + qs].T.astype(BF16))

    @pl.when(i == 0)
    def _():
        begin()
        softmax_diag(scores_diag(0))

    @pl.when(jnp.logical_and(i > 0, i < nq))
    def _():
        begin()
        s = scores(0)
        drain_previous()
        softmax(s)

        def body(j, carry):
            s = scores(j)
            values(j - 1, slot)
            softmax(s)
            return carry

        lax.fori_loop(1, i, body, 0)
        parts = scores_diag(i)
        values(i - 1, slot)
        softmax_diag(parts)

    @pl.when(i == nq)
    def _():
        drain_previous()


def _attention_prompt(q, k, vt, t):
    nseq, _, _, qs, _ = q.shape
    length = k.shape[1]
    rows = MLA_HEADS * t
    nq = length // t
    assert vt.shape[-1] == t and t % qs == 0 and qs % CHUNK == 0
    return pl.pallas_call(
        functools.partial(_attn_kernel, t=t, qs=qs),
        grid=(nseq, nq + 1),
        in_specs=[pl.BlockSpec((1, t // qs, MLA_HEADS, qs, QK_DIM),
                               lambda b, i: (b, jnp.minimum(i, nq - 1), 0, 0, 0)),
                  pl.BlockSpec((1, length, QK_DIM), lambda b, i: (b, 0, 0)),
                  pl.BlockSpec((1, nq, VT_ROWS, t), lambda b, i: (b, 0, 0, 0))],
        out_specs=pl.BlockSpec((1, t, MLA_HEADS * KV_RANK), lambda b, i: (b, jnp.maximum(i - 1, 0), 0)),
        out_shape=jax.ShapeDtypeStruct((nseq, length, MLA_HEADS * KV_RANK), BF16),
        scratch_shapes=[pltpu.VMEM((1, rows), F32), pltpu.VMEM((1, rows), F32),
                        pltpu.VMEM((t, rows), BF16), pltpu.VMEM((2, VT_ROWS, rows), F32)],
        compiler_params=_params("parallel", "arbitrary"),
        name="mla_attention_prompt",
    )(q, k, vt)


def _chunk_mask(s, t, q_start, k_start):
    n = s.shape[1]
    qpos = q_start + lax.broadcasted_iota(jnp.int32, (t, n), 0)
    kpos = k_start + lax.broadcasted_iota(jnp.int32, (t, n), 1)
    allowed = (kpos // CHUNK) <= (qpos // CHUNK)
    return jnp.where(allowed[None], s.reshape(MLA_HEADS, t, n), NEG_INF).reshape(s.shape)


def _attn_sample_kernel(q_ref, ckv_ref, ckpe_ref, kn_ref, o_ref, *, t, past, mask_cache, mask_new):
    rows = MLA_HEADS * t
    q = q_ref[0].reshape(rows, QK_DIM)
    ckv = ckv_ref[0].astype(BF16)
    ckpe_t = ckpe_ref[0].astype(BF16)
    kn = kn_ref[0]
    s_c = _dot_nt(q[:, :KV_RANK], ckv) + _dot(q[:, KV_RANK:], ckpe_t)
    s_n = _dot_nt(q, kn)
    if mask_cache:
        s_c = _chunk_mask(s_c, t, past, 0)
    if mask_new:
        s_n = _chunk_mask(s_n, t, past, past)
    m = jnp.maximum(jnp.max(s_c, axis=1, keepdims=True), jnp.max(s_n, axis=1, keepdims=True))
    p_c = jnp.exp2(s_c - m)
    p_n = jnp.exp2(s_n - m)
    denom = jnp.sum(p_c, axis=1, keepdims=True) + jnp.sum(p_n, axis=1, keepdims=True)
    o = _dot(p_c.astype(BF16), ckv) + _dot(p_n.astype(BF16), kn[:, :KV_RANK])
    o = (o / denom).astype(BF16)
    for hd in range(MLA_HEADS):
        o_ref[0, :, hd * KV_RANK:(hd + 1) * KV_RANK] = o[hd * t:(hd + 1) * t]


def _attention_sample(q, cache_kv, cache_kpe_t, k_new, l):
    nb, _, t, _ = q.shape
    past = cache_kv.shape[2]
    q_chunk = (past + np.arange(t)) // CHUNK
    k_chunk = np.arange(past + t) // CHUNK
    allowed = k_chunk[None, :] <= q_chunk[:, None]
    mask_cache = not allowed[:, :past].all()
    mask_new = not allowed[:, past:].all()
    return pl.pallas_call(
        functools.partial(_attn_sample_kernel, t=t, past=past, mask_cache=mask_cache, mask_new=mask_new),
        grid=(nb,),
        in_specs=[pl.BlockSpec((1, MLA_HEADS, t, QK_DIM), lambda b: (b, 0, 0, 0)),
                  pl.BlockSpec((None, 1, past, KV_RANK), lambda b: (l, b, 0, 0)),
                  pl.BlockSpec((None, 1, MLA_ROPE, past), lambda b: (l, b, 0, 0)),
                  pl.BlockSpec((1, t, QK_DIM), lambda b: (b, 0, 0))],
        out_specs=pl.BlockSpec((1, t, MLA_HEADS * KV_RANK), lambda b: (b, 0, 0)),
        out_shape=jax.ShapeDtypeStruct((nb, t, MLA_HEADS * KV_RANK), BF16),
        compiler_params=_params("parallel"),
        name="mla_attention_sample",
    )(q, cache_kv, cache_kpe_t, k_new)


def _ret_kernel(rq_ref, rk_ref, rv_ref, s0_ref, dec_ref, cross_ref, kdec_ref, gt_ref,
                o_ref, sout_ref, st_scr):
    c = pl.program_id(1)

    @pl.when(c == 0)
    def _():
        st_scr[...] = s0_ref[0]

    heads = range(RET_HEADS)
    t = dec_ref.shape[1]
    chunks = [slice(r, r + t) for r in range(0, rq_ref.shape[1], t)]
    ks = [slice(hd * RET_DK, (hd + 1) * RET_DK) for hd in heads]
    vs = [slice(hd * RET_DV, (hd + 1) * RET_DV) for hd in heads]
    qs = [[rq_ref[0, rows, ks[hd]] for hd in heads] for rows in chunks]
    kk = [[rk_ref[0, rows, ks[hd]] for hd in heads] for rows in chunks]
    vv = [[rv_ref[0, rows, vs[hd]] for hd in heads] for rows in chunks]
    scores = [[_dot_nt(q, k) for q, k in zip(qc, kc)] for qc, kc in zip(qs, kk)]
    st = [st_scr[hd] for hd in heads]
    for n, rows in enumerate(chunks):
        cross = [_dot(qs[n][hd], st[hd].astype(BF16)) for hd in heads]
        st_next = []
        for hd in heads:
            k_dec = (kk[n][hd].astype(F32) * kdec_ref[hd]).astype(BF16)
            st_next.append(gt_ref[hd] * st[hd] + _dot_tn(k_dec, vv[n][hd]))
        for hd in heads:
            sc = (scores[n][hd] * dec_ref[hd]).astype(BF16)
            o_ref[0, rows, vs[hd]] = (_dot(sc, vv[n][hd]) + cross[hd] * cross_ref[hd]).astype(BF16)
        st = st_next
    for hd in heads:
        st_scr[hd] = st[hd]

    @pl.when(c == pl.num_programs(1) - 1)
    def _():
        sout_ref[0] = st_scr[...]


def _ret_tables(t):
    f32 = np.float32
    log_g = np.log1p(-np.exp2(-5.0 - np.arange(RET_HEADS, dtype=f32))).astype(f32)
    n = np.arange(t, dtype=f32)
    diff = n[:, None] - n[None, :]
    dec = np.where(diff >= 0.0, np.exp(np.maximum(diff, 0.0)[None] * log_g[:, None, None]), 0.0)
    cross = np.exp((n + 1.0)[None, :] * log_g[:, None])
    kdec = np.exp((t - 1.0 - n)[None, :] * log_g[:, None])
    gt = np.exp(f32(t) * log_g)
    cross = np.broadcast_to(cross[:, :, None], (RET_HEADS, t, RET_DV))
    kdec = np.broadcast_to(kdec[:, :, None], (RET_HEADS, t, RET_DK))
    gt = np.broadcast_to(gt[:, None, None], (RET_HEADS, RET_DK, RET_DV))
    return tuple(jnp.asarray(a, F32) for a in (dec, cross, kdec, gt))


def _retention(rq, rk, rv, state0, tabs, t):
    nseq, length, _ = rq.shape
    assert t % tabs[0].shape[1] == 0
    tile = pl.BlockSpec((1, t, RET_W), lambda b, c: (b, c, 0))
    st = pl.BlockSpec((1, RET_HEADS, RET_DK, RET_DV), lambda b, c: (b, 0, 0, 0))
    const = lambda a: pl.BlockSpec(a.shape, lambda b, c: (0, 0, 0))
    return pl.pallas_call(
        _ret_kernel,
        grid=(nseq, length // t),
        in_specs=[tile, tile, tile, st] + [const(a) for a in tabs],
        out_specs=[tile, st],
        out_shape=[jax.ShapeDtypeStruct((nseq, length, RET_W), BF16),
                   jax.ShapeDtypeStruct(state0.shape, F32)],
        scratch_shapes=[pltpu.VMEM((RET_HEADS, RET_DK, RET_DV), F32)],
        compiler_params=_params("parallel", "arbitrary"),
        name="retention",
    )(rq, rk, rv, state0, *tabs)


def _out_kernel(x_ref, ol_ref, or_ref, rg_ref, gt_ref, gpost_ref, wuv_ref, wo_ref, grp_ref, o_ref, *, sub):
    n_mla = MLA_HEADS * MLA_V
    starts = list(range(0, x_ref.shape[1], sub))
    tiles = [slice(s0, s0 + sub) for s0 in starts]
    o_mla = [_dot(ol_ref[0, rs], wuv_ref[...]).astype(BF16) for rs in tiles]
    mix = [_dot(om, wo_ref[:n_mla, :]) for om in o_mla]
    o = [or_ref[0, rs].astype(F32) for rs in tiles]
    ms = [_dot((v * v).astype(BF16), grp_ref[...]) for v in o]
    o_ret = [(v * lax.rsqrt(m + EPS) * _silu(rg_ref[0, rs].astype(F32))).astype(BF16)
             for v, m, rs in zip(o, ms, tiles)]
    for i, rs in enumerate(tiles):
        gate = gt_ref[0] if gt_ref.shape[1] == 1 else gt_ref[0, rs]
        total = mix[i] + _dot(o_ret[i], wo_ref[n_mla:, :])
        o_ref[0, rs] = x_ref[0, rs] + gate * (_rms(total) * gpost_ref[...])


def _mixer_out(x, o_lat, o_ret, rg, gate, g_post, w_uv, w_out, l, tm):
    nseq, length, d = x.shape
    tile = lambda n: pl.BlockSpec((1, tm, n), lambda b, s: (b, s, 0))
    wl = lambda r, c: _resident((None, r, c), lambda b, s: (l, 0, 0))
    head = np.arange(RET_W) // RET_DV
    group_mean = jnp.asarray((head[:, None] == head[None, :]).astype(np.float32) / RET_DV, BF16)
    return pl.pallas_call(
        functools.partial(_out_kernel, sub=min(tm, MIX_SUB_ROWS)),
        grid=(nseq, length // tm),
        in_specs=[tile(d), tile(o_lat.shape[-1]), tile(RET_W), tile(RET_W), _mod_spec(gate, tm),
                  pl.BlockSpec((1, d), lambda b, s: (0, 0)),
                  wl(*w_uv.shape[1:]), wl(*w_out.shape[1:]),
                  _resident((RET_W, RET_W), lambda b, s: (0, 0))],
        out_specs=tile(d),
        out_shape=jax.ShapeDtypeStruct(x.shape, F32),
        compiler_params=_params("parallel", "parallel"),
        name="mixer_out",
    )(x, o_lat, o_ret, rg, gate, g_post, w_uv, w_out, group_mean)


def _prep_w_in(w_in):
    c0 = Q_RANK + KV_RANK
    half = MLA_ROPE // 2
    kpe = w_in[..., c0:c0 + MLA_ROPE]
    kpe_rot = jnp.concatenate([-kpe[..., half:], kpe[..., :half]], axis=-1)
    pad = jnp.zeros(w_in.shape[:-1] + (KPE_BLOCK - 2 * MLA_ROPE,), w_in.dtype)
    return jnp.concatenate([w_in[..., :c0], kpe, kpe_rot, pad, w_in[..., c0 + MLA_ROPE:]],
                           axis=-1).astype(BF16)


def _absorb_kernel(uq_ref, uk_ref, o_ref):
    o_ref[0, 0] = lax.dot_general(uq_ref[0, 0], uk_ref[0, 0], (((1,), (1,)), ((), ())),
                                  preferred_element_type=F32, precision=lax.Precision.HIGHEST)


def _prep_w_uq(w_uq, w_uk):
    depth = w_uq.shape[0]
    half = MLA_ROPE // 2
    w = w_uq.reshape(depth, Q_RANK, MLA_HEADS, MLA_NOPE + MLA_ROPE)
    nope = w[..., :MLA_NOPE].transpose(0, 2, 1, 3)
    uk = w_uk.reshape(depth, KV_RANK, MLA_HEADS, MLA_NOPE).transpose(0, 2, 1, 3)
    absorbed = pl.pallas_call(
        _absorb_kernel,
        grid=(depth, MLA_HEADS),
        in_specs=[pl.BlockSpec((1, 1, Q_RANK, MLA_NOPE), lambda l, h: (l, h, 0, 0)),
                  pl.BlockSpec((1, 1, KV_RANK, MLA_NOPE), lambda l, h: (l, h, 0, 0))],
        out_specs=pl.BlockSpec((1, 1, Q_RANK, KV_RANK), lambda l, h: (l, h, 0, 0)),
        out_shape=jax.ShapeDtypeStruct((depth, MLA_HEADS, Q_RANK, KV_RANK), F32),
        compiler_params=_params("parallel", "parallel"),
        name="absorb_w_uk",
    )(nope, uk)
    absorbed = absorbed.transpose(0, 2, 1, 3).reshape(depth, Q_RANK, MLA_HEADS * KV_RANK)
    pe = w[..., MLA_NOPE:]
    rot = jnp.concatenate([-pe[..., half:], pe[..., :half]], axis=-1)
    flat = lambda a: a.reshape(depth, Q_RANK, MLA_HEADS * MLA_ROPE)
    return jnp.concatenate([absorbed, flat(pe), flat(rot)], axis=-1).astype(BF16)


def _prep_w_uv(w_uv):
    depth = w_uv.shape[0]
    w = w_uv.reshape(depth, KV_RANK, MLA_HEADS, MLA_V)
    eye = jnp.eye(MLA_HEADS, dtype=w_uv.dtype)
    bd = jnp.einsum('lrhv,hg->lhrgv', w, eye)
    return bd.reshape(depth, MLA_HEADS * KV_RANK, MLA_HEADS * MLA_V).astype(BF16)


def _pick_tile(length, target):
    t = min(length, target)
    while length % t:
        t //= 2
    return t


def kernel(x_prompt, x_sample, cache_kv_latent, cache_k_rope, state_ret, c_prompt, c_sample,
           ada_w, ada_b, norm_pre, norm_post, ffn_w1, ffn_w3, ffn_w2,
           w_in, q_norm, kv_norm, w_uq, w_uk, w_uv, w_out):
    depth = ada_w.shape[0]
    nb_p, seq, d = x_prompt.shape
    nb_s, dec_seq, _ = x_sample.shape
    past = cache_kv_latent.shape[2]
    n_tok_s = nb_s * dec_seq

    w1, w3, w2 = ffn_w1.astype(BF16), ffn_w3.astype(BF16), ffn_w2.astype(BF16)
    w_in_b = _prep_w_in(w_in)
    w_uq_b = _prep_w_uq(w_uq, w_uk)
    w_uv_b = _prep_w_uv(w_uv)
    w_out_b = w_out.astype(BF16)

    mods = _ada_mods(jnp.concatenate([c_prompt, c_sample], axis=0), ada_w, ada_b)

    t_att = _pick_tile(seq, 512)
    tm_p = _pick_tile(seq, 4 * MIX_SUB_ROWS)
    tm_ffn = _pick_tile(seq, 4 * FFN_SUB_ROWS)
    t_ret = _pick_tile(seq, 256)
    ret_step = _pick_tile(seq, 2 * t_ret)
    tabs_p = _rope_tables(np.arange(seq))
    tabs_s = _rope_tables(np.tile(past + np.arange(dec_seq), nb_s))
    ret_tabs_p = _ret_tables(t_ret)
    ret_tabs_s = _ret_tables(dec_seq)
    zero_state = jnp.zeros((nb_p, RET_HEADS, RET_DK, RET_DV), F32)
    cache_kpe_t = jnp.swapaxes(cache_k_rope, 2, 3)

    def mod_p(l, r):
        return mods[l, :nb_p, r][:, None, :]

    def mod_s(l, r):
        return jnp.repeat(mods[l, nb_p:, r], dec_seq, axis=0)[None]

    y_p = x_prompt
    y_s = x_sample.reshape(1, n_tok_s, d)
    caches_p, caches_s = None, None
    ret_p, ret_s = [], []

    for l in range(depth):
        gp = lambda j: norm_pre[l, j][None, :]
        go = lambda j: norm_post[l, j][None, :]
        qn, kvn = q_norm[l][None, :], kv_norm[l][None, :]

        y_p = _ffn(y_p, mod_p(l, 0), mod_p(l, 1), mod_p(l, 2), gp(0), go(0), w1, w3, w2, l, 0, tm_ffn)
        q, k, vt, kv, kpe, rq, rk, rv, rg = _proj(y_p, mod_p(l, 3), mod_p(l, 4), gp(1), w_in_b, qn, kvn,
                                                  w_uq_b, tabs_p, l, tm_p, Q_SUB, t_att, caches_p)
        caches_p = (kv, kpe)
        o_lat = _attention_prompt(q, k, vt, t_att)
        o_ret, r_fin = _retention(rq, rk, rv, zero_state, ret_tabs_p, ret_step)
        y_p = _mixer_out(y_p, o_lat, o_ret, rg, mod_p(l, 5), go(1), w_uv_b, w_out_b, l, tm_p)
        y_p = _ffn(y_p, mod_p(l, 6), mod_p(l, 7), mod_p(l, 8), gp(2), go(2), w1, w3, w2, l, 1, tm_ffn)
        ret_p.append(r_fin)

        y_s = _ffn_stream(y_s, mod_s(l, 0), mod_s(l, 1), mod_s(l, 2), gp(0), go(0), w1, w3, w2, l, 0)
        q, k, _, kv, kpe, rq, rk, rv, rg = _proj(y_s, mod_s(l, 3), mod_s(l, 4), gp(1), w_in_b, qn, kvn,
                                                 w_uq_b, tabs_s, l, n_tok_s, dec_seq, n_tok_s, caches_s)
        caches_s = (kv, kpe)
        per_seq = lambda a: a.reshape(nb_s, dec_seq, a.shape[-1])
        q = q.reshape(nb_s, MLA_HEADS, dec_seq, QK_DIM)
        o_lat = _attention_sample(q, cache_kv_latent, cache_kpe_t, per_seq(k), l)
        o_ret, r_new = _retention(per_seq(rq), per_seq(rk), per_seq(rv),
                                  state_ret[l], ret_tabs_s, dec_seq)
        y_s = _mixer_out(y_s, o_lat.reshape(1, n_tok_s, -1), o_ret.reshape(1, n_tok_s, -1), rg,
                         mod_s(l, 5), go(1), w_uv_b, w_out_b, l, n_tok_s)
        y_s = _ffn_stream(y_s, mod_s(l, 6), mod_s(l, 7), mod_s(l, 8), gp(2), go(2), w1, w3, w2, l, 1)
        ret_s.append(r_new)

    per_layer_seq = lambda a: a.reshape(depth, nb_s, dec_seq, a.shape[-1])
    tokens_major = lambda a: jnp.swapaxes(a, 2, 3)
    return (y_p, y_s.reshape(nb_s, dec_seq, d),
            caches_p[0], tokens_major(caches_p[1]), jnp.stack(ret_p),
            per_layer_seq(caches_s[0]), per_layer_seq(tokens_major(caches_s[1])), jnp.stack(ret_s))
```
